```python
import math
import jax, jax.numpy as jnp
from jax import lax
import numpy as np

D_MODEL = 4096
BATCH = 1
SEQ = 8192
DEPTH = 1

CHUNK = 64
Q_BLOCK = 128
ROPE_THETA = 10000.0
EPS = 1e-6
MIX_WIDTH = D_MODEL
MLA_HEADS = 16
MLA_Q_LORA = 768
MLA_KV_LORA = 512
MLA_NOPE = 128
MLA_ROPE = 64
MLA_QK = MLA_NOPE + MLA_ROPE
MLA_V = 128
MLA_SCALE = 1.0 / math.sqrt(MLA_QK)
DIFF_HEADS = 8
DIFF_HEAD_DIM = 128
DIFF_V = 2 * DIFF_HEAD_DIM
DIFF_SCALE = 1.0 / math.sqrt(DIFF_HEAD_DIM)
DIFF_QK_COLS = 2 * DIFF_HEADS * DIFF_HEAD_DIM
DIFF_V_COLS = DIFF_HEADS * DIFF_V
IN_SIZES = (MLA_Q_LORA, MLA_KV_LORA, MLA_ROPE, DIFF_QK_COLS, DIFF_QK_COLS, DIFF_V_COLS)
D_IN = MLA_Q_LORA + MLA_KV_LORA + MLA_ROPE + 2 * DIFF_QK_COLS + DIFF_V_COLS
D_FF = -(-8 * D_MODEL // (3 * 256)) * 256

kernel_name = "hymba_mla_diffattn_swiglu_chunk_causal"


def _rms_norm(t, g):
    tf = t.astype(jnp.float32)
    y = tf * lax.rsqrt(jnp.mean(tf * tf, axis=-1, keepdims=True) + EPS)
    return (y * g.astype(jnp.float32)).astype(t.dtype)


def _rope(t, pos):
    half = t.shape[-1] // 2
    inv_freq = ROPE_THETA ** (-jnp.arange(half, dtype=jnp.float32) / half)
    ang = pos.astype(jnp.float32)[:, None] * inv_freq[None, :]
    cos = jnp.cos(ang)[:, None, :]
    sin = jnp.sin(ang)[:, None, :]
    tf = t.astype(jnp.float32)
    t1, t2 = tf[..., :half], tf[..., half:]
    return jnp.concatenate([t1 * cos - t2 * sin, t2 * cos + t1 * sin], axis=-1).astype(t.dtype)


def _to_blocks(t):
    b, s = t.shape[:2]
    return jnp.moveaxis(t.reshape((b, s // Q_BLOCK, Q_BLOCK) + t.shape[2:]), 1, 0)


def _from_blocks(o):
    o = jnp.moveaxis(o, 0, 1)
    return o.reshape((o.shape[0], o.shape[1] * o.shape[2]) + o.shape[3:])


def _chunk_softmax(s, blk):
    q_pos = blk * Q_BLOCK + jnp.arange(Q_BLOCK)
    k_pos = jnp.arange(s.shape[-1])
    allowed = (k_pos[None, :] // CHUNK) <= (q_pos[:, None] // CHUNK)
    s = jnp.where(allowed, s, jnp.finfo(jnp.float32).min)
    return jax.nn.softmax(s, axis=-1)


def _mla_attend(q, k, v):
    def body(args):
        qb, blk = args
        s = jnp.einsum('bqhd,bkhd->bhqk', qb, k, preferred_element_type=jnp.float32) * MLA_SCALE
        p = _chunk_softmax(s, blk)
        return jnp.einsum('bhqk,bkhd->bqhd', p.astype(v.dtype), v)
    nb = q.shape[1] // Q_BLOCK
    return _from_blocks(lax.map(body, (_to_blocks(q), jnp.arange(nb))))


def _diff_attend(q1, q2, k1, k2, v, lam):
    def body(args):
        qb1, qb2, blk = args
        s1 = jnp.einsum('bqhd,bkhd->bhqk', qb1, k1, preferred_element_type=jnp.float32) * DIFF_SCALE
        s2 = jnp.einsum('bqhd,bkhd->bhqk', qb2, k2, preferred_element_type=jnp.float32) * DIFF_SCALE
        p = _chunk_softmax(s1, blk) - lam * _chunk_softmax(s2, blk)
        return jnp.einsum('bhqk,bkhd->bqhd', p.astype(v.dtype), v)
    nb = q1.shape[1] // Q_BLOCK
    return _from_blocks(lax.map(body, (_to_blocks(q1), _to_blocks(q2), jnp.arange(nb))))


def setup_inputs(seed: int = 0) -> dict:
    key = jax.random.key(seed)
    ks = jax.random.split(key, 22)
    f32 = jnp.float32

    def w(k, fan_in, fan_out):
        return jax.random.normal(k, (DEPTH, fan_in, fan_out), f32) * fan_in ** -0.5

    def gain(k, n):
        return 1.0 + 0.02 * jax.random.normal(k, (DEPTH, n), f32)

    def lam_vec(k):
        return 0.1 * jax.random.normal(k, (DEPTH, DIFF_HEAD_DIM), f32)

    return {
        "x": jax.random.normal(ks[0], (BATCH, SEQ, D_MODEL), f32),
        "attn_norm_g": gain(ks[1], D_MODEL),
        "w_in": w(ks[2], D_MODEL, D_IN),
        "q_latent_norm_g": gain(ks[3], MLA_Q_LORA),
        "kv_latent_norm_g": gain(ks[4], MLA_KV_LORA),
        "w_uq": w(ks[5], MLA_Q_LORA, MLA_HEADS * MLA_QK),
        "w_ukv": w(ks[6], MLA_KV_LORA, MLA_HEADS * (MLA_NOPE + MLA_V)),
        "mla_q_norm_g": gain(ks[7], MLA_QK),
        "mla_k_norm_g": gain(ks[8], MLA_QK),
        "diff_q_norm_g": gain(ks[9], DIFF_HEAD_DIM),
        "diff_k_norm_g": gain(ks[10], DIFF_HEAD_DIM),
        "lambda_q1": lam_vec(ks[11]),
        "lambda_k1": lam_vec(ks[12]),
        "lambda_q2": lam_vec(ks[13]),
        "lambda_k2": lam_vec(ks[14]),
        "diff_subln_g": gain(ks[15], DIFF_V),
        "w_o": w(ks[16], MIX_WIDTH, D_MODEL),
        "ffn_norm_g": gain(ks[17], D_MODEL),
        "w_gate": w(ks[18], D_MODEL, D_FF),
        "w_up": w(ks[19], D_MODEL, D_FF),
        "w_down": w(ks[20], D_FF, D_MODEL),
    }


def reference(x, attn_norm_g, w_in, q_latent_norm_g, kv_latent_norm_g, w_uq, w_ukv,
              mla_q_norm_g, mla_k_norm_g, diff_q_norm_g, diff_k_norm_g,
              lambda_q1, lambda_k1, lambda_q2, lambda_k2, diff_subln_g, w_o,
              ffn_norm_g, w_gate, w_up, w_down):
    b, s, _ = x.shape
    pos = jnp.arange(s)
    split_idx = list(np.cumsum(IN_SIZES)[:-1])
    h = x
    for layer in range(DEPTH):
        lambda_init = 0.8 - 0.6 * math.exp(-0.3 * layer)
        n = _rms_norm(h, attn_norm_g[layer])
        z = n @ w_in[layer]
        c_q, c_kv, k_pe, dq, dk, dv = jnp.split(z, split_idx, axis=-1)

        q = (_rms_norm(c_q, q_latent_norm_g[layer]) @ w_uq[layer]).reshape(b, s, MLA_HEADS, MLA_QK)
        kv = (_rms_norm(c_kv, kv_latent_norm_g[layer]) @ w_ukv[layer]).reshape(b, s, MLA_HEADS, MLA_NOPE + MLA_V)
        k_nope, v_a = kv[..., :MLA_NOPE], kv[..., MLA_NOPE:]
        k_pe = jnp.broadcast_to(k_pe[:, :, None, :], (b, s, MLA_HEADS, MLA_ROPE))
        k = jnp.concatenate([k_nope, k_pe], axis=-1)
        q = _rms_norm(q, mla_q_norm_g[layer])
        k = _rms_norm(k, mla_k_norm_g[layer])
        q = jnp.concatenate([q[..., :MLA_NOPE], _rope(q[..., MLA_NOPE:], pos)], axis=-1)
        k = jnp.concatenate([k[..., :MLA_NOPE], _rope(k[..., MLA_NOPE:], pos)], axis=-1)
        o_a = _mla_attend(q, k, v_a).reshape(b, s, MLA_HEADS * MLA_V)

        dq = _rope(_rms_norm(dq.reshape(b, s, 2 * DIFF_HEADS, DIFF_HEAD_DIM), diff_q_norm_g[layer]), pos)
        dk = _rope(_rms_norm(dk.reshape(b, s, 2 * DIFF_HEADS, DIFF_HEAD_DIM), diff_k_norm_g[layer]), pos)
        dq = dq.reshape(b, s, DIFF_HEADS, 2, DIFF_HEAD_DIM)
        dk = dk.reshape(b, s, DIFF_HEADS, 2, DIFF_HEAD_DIM)
        dv = dv.reshape(b, s, DIFF_HEADS, DIFF_V)
        lam = (jnp.exp(jnp.sum(lambda_q1[layer].astype(jnp.float32) * lambda_k1[layer].astype(jnp.float32)))
               - jnp.exp(jnp.sum(lambda_q2[layer].astype(jnp.float32) * lambda_k2[layer].astype(jnp.float32)))
               + lambda_init)
        o_b = _diff_attend(dq[..., 0, :], dq[..., 1, :], dk[..., 0, :], dk[..., 1, :], dv, lam)
        o_b = (_rms_norm(o_b, diff_subln_g[layer]) * (1.0 - lambda_init)).reshape(b, s, DIFF_HEADS * DIFF_V)

        h = h + jnp.concatenate([o_a, o_b], axis=-1) @ w_o[layer]

        m = _rms_norm(h, ffn_norm_g[layer])
        h = h + (jax.nn.silu(m @ w_gate[layer]) * (m @ w_up[layer])) @ w_down[layer]
    return h
```

```python
import functools
import math

import jax
import jax.numpy as jnp
from jax import lax
from jax.experimental import pallas as pl
from jax.experimental.pallas import tpu as pltpu

F32 = jnp.float32
BF16 = jnp.bfloat16

D_MODEL = 4096
CHUNK = 64
ROPE_THETA = 10000.0
EPS = 1e-6
MLA_HEADS = 16
MLA_Q_LORA = 768
MLA_KV_LORA = 512
MLA_NOPE = 128
MLA_ROPE = 64
MLA_QK = MLA_NOPE + MLA_ROPE
MLA_V = 128
MLA_SCALE = 1.0 / math.sqrt(MLA_QK)
DIFF_HEADS = 8
DIFF_HEAD_DIM = 128
DIFF_V = 2 * DIFF_HEAD_DIM
DIFF_SCALE = 1.0 / math.sqrt(DIFF_HEAD_DIM)
DIFF_QK_COLS = 2 * DIFF_HEADS * DIFF_HEAD_DIM
DIFF_V_COLS = DIFF_HEADS * DIFF_V
D_FF = -(-8 * D_MODEL // (3 * 256)) * 256
LAMBDA_INIT = 0.8 - 0.6 * math.exp(-0.3 * 0)

LANES = 128
MLA_HEAD_PAD = 2 * LANES
Z_LAT = 1536
Z_KPE = MLA_Q_LORA + MLA_KV_LORA
Z_DQ = Z_LAT
Z_DV = Z_LAT + 2 * DIFF_QK_COLS
Z_COLS = Z_DV + DIFF_V_COLS
D_FF_PAD = 11264
NEG = -1e30
VMEM_LIMIT = 56 * 1024 * 1024


def _cparams(sem):
    return pltpu.CompilerParams(dimension_semantics=sem, vmem_limit_bytes=VMEM_LIMIT)


def _rmsnorm_kernel(x_ref, g_ref, o_ref):
    x = x_ref[...]
    ms = jnp.mean(x * x, axis=-1, keepdims=True)
    o_ref[...] = (x * lax.rsqrt(ms + EPS) * g_ref[...]).astype(o_ref.dtype)


def _rmsnorm(x, g, tm):
    m, d = x.shape
    return pl.pallas_call(
        _rmsnorm_kernel,
        grid=(m // tm,),
        in_specs=[pl.BlockSpec((tm, d), lambda i: (i, 0)),
                  pl.BlockSpec((1, d), lambda i: (0, 0))],
        out_specs=pl.BlockSpec((tm, d), lambda i: (i, 0)),
        out_shape=jax.ShapeDtypeStruct((m, d), BF16),
        compiler_params=_cparams(("parallel",)),
        name="rmsnorm",
    )(x, g.reshape(1, d))


def _mm_kernel(a_ref, b_ref, o_ref):
    o_ref[...] = jnp.dot(a_ref[...], b_ref[...],
                         preferred_element_type=F32).astype(o_ref.dtype)


def _matmul(a, b, tm, tn, out_dtype, name):
    m, k = a.shape
    _, n = b.shape
    return pl.pallas_call(
        _mm_kernel,
        grid=(m // tm, n // tn),
        in_specs=[pl.BlockSpec((tm, k), lambda i, j: (i, 0)),
                  pl.BlockSpec((k, tn), lambda i, j: (0, j))],
        out_specs=pl.BlockSpec((tm, tn), lambda i, j: (i, j)),
        out_shape=jax.ShapeDtypeStruct((m, n), out_dtype),
        compiler_params=_cparams(("parallel", "arbitrary")),
        name=name,
    )(a, b)


def _latent_norm(c, g):
    ms = jnp.mean(c * c, axis=-1, keepdims=True)
    return (c * lax.rsqrt(ms + EPS) * g).astype(BF16)


def _mla_rope(y, cos, sa, sb):
    return y * cos + pltpu.roll(y, 96, 1) * sa + pltpu.roll(y, 32, 1) * sb


def _q_prep_kernel(z_ref, gl_ref, w_ref, ga_ref, gb_ref, cos_ref, sa_ref, sb_ref, o_ref):
    c = z_ref[:, :MLA_Q_LORA].astype(F32)
    lat = _latent_norm(c, gl_ref[...])
    q = jnp.dot(lat, w_ref[...], preferred_element_type=F32)
    cos, sa, sb = cos_ref[...], sa_ref[...], sb_ref[...]
    ga, gb = ga_ref[...], gb_ref[...]
    for h in range(MLA_HEADS):
        lo = h * MLA_HEAD_PAD
        qa = q[:, lo:lo + LANES]
        qb = q[:, lo + LANES:lo + 2 * LANES]
        ss = (jnp.sum(qa * qa, axis=-1, keepdims=True)
              + jnp.sum(qb * qb, axis=-1, keepdims=True))
        r = lax.rsqrt(ss * (1.0 / MLA_QK) + EPS)
        ya = qa * r * ga
        yb = _mla_rope(qb * r * gb, cos, sa, sb)
        o_ref[:, lo:lo + LANES] = (ya * MLA_SCALE).astype(o_ref.dtype)
        o_ref[:, lo + LANES:lo + 2 * LANES] = (yb * MLA_SCALE).astype(o_ref.dtype)


def _kv_prep_kernel(z_ref, gl_ref, wk_ref, wv_ref, ga_ref, gb_ref, cos_ref, sa_ref, sb_ref,
                    k_ref, v_ref):
    c = z_ref[:, MLA_Q_LORA:Z_KPE].astype(F32)
    lat = _latent_norm(c, gl_ref[...])
    kn = jnp.dot(lat, wk_ref[...], preferred_element_type=F32)
    v_ref[...] = jnp.dot(lat, wv_ref[...], preferred_element_type=F32).astype(v_ref.dtype)
    kpe = z_ref[:, Z_KPE:Z_KPE + LANES].astype(F32)
    ss_pe = jnp.sum(kpe * kpe, axis=-1, keepdims=True)
    pe_rot = _mla_rope(kpe * gb_ref[...], cos_ref[...], sa_ref[...], sb_ref[...])
    ga = ga_ref[...]
    for h in range(MLA_HEADS):
        a = kn[:, h * LANES:(h + 1) * LANES]
        ss = jnp.sum(a * a, axis=-1, keepdims=True) + ss_pe
        r = lax.rsqrt(ss * (1.0 / MLA_QK) + EPS)
        lo = h * MLA_HEAD_PAD
        k_ref[:, lo:lo + LANES] = (a * r * ga).astype(k_ref.dtype)
        k_ref[:, lo + LANES:lo + 2 * LANES] = (pe_rot * r).astype(k_ref.dtype)


def _row_spec(tm, n, col=0):
    return pl.BlockSpec((tm, n), lambda i: (i, col))


def _const_spec(shape):
    return pl.BlockSpec(shape, lambda i: (0, 0))


def _q_prep(z, gl, w, ga, gb, cos, sa, sb, tm):
    m = z.shape[0]
    n = MLA_HEADS * MLA_HEAD_PAD
    return pl.pallas_call(
        _q_prep_kernel,
        grid=(m // tm,),
        in_specs=[_row_spec(tm, Z_LAT), _const_spec(gl.shape), _const_spec(w.shape),
                  _const_spec(ga.shape), _const_spec(gb.shape),
                  _row_spec(tm, LANES), _row_spec(tm, LANES), _row_spec(tm, LANES)],
        out_specs=_row_spec(tm, n),
        out_shape=jax.ShapeDtypeStruct((m, n), BF16),
        compiler_params=_cparams(("parallel",)),
        name="mla_q_prep",
    )(z, gl, w, ga, gb, cos, sa, sb)


def _kv_prep(z, gl, wk, wv, ga, gb, cos, sa, sb, tm):
    m = z.shape[0]
    nk = MLA_HEADS * MLA_HEAD_PAD
    nv = MLA_HEADS * MLA_V
    return pl.pallas_call(
        _kv_prep_kernel,
        grid=(m // tm,),
        in_specs=[_row_spec(tm, Z_LAT), _const_spec(gl.shape), _const_spec(wk.shape),
                  _const_spec(wv.shape), _const_spec(ga.shape), _const_spec(gb.shape),
                  _row_spec(tm, LANES), _row_spec(tm, LANES), _row_spec(tm, LANES)],
        out_specs=[_row_spec(tm, nk), _row_spec(tm, nv)],
        out_shape=[jax.ShapeDtypeStruct((m, nk), BF16), jax.ShapeDtypeStruct((m, nv), BF16)],
        compiler_params=_cparams(("parallel",)),
        name="mla_kv_prep",
    )(z, gl, wk, wv, ga, gb, cos, sa, sb)


DIFF_PREP_COLS = 512


def _diff_prep_kernel(x_ref, g_ref, cos_ref, sin_ref, o_ref):
    cos, sin = cos_ref[...], sin_ref[...]
    for c in range(DIFF_PREP_COLS // LANES):
        sl = slice(c * LANES, (c + 1) * LANES)
        x = x_ref[:, sl].astype(F32)
        ms = jnp.mean(x * x, axis=-1, keepdims=True)
        y = x * lax.rsqrt(ms + EPS) * g_ref[:, sl]
        o_ref[:, sl] = (y * cos + pltpu.roll(y, 64, 1) * sin).astype(o_ref.dtype)


def _diff_prep(z, g, cos, sin, tm):
    m = z.shape[0]
    n = 2 * DIFF_QK_COLS
    nb = n // DIFF_PREP_COLS
    off = Z_DQ // DIFF_PREP_COLS
    return pl.pallas_call(
        _diff_prep_kernel,
        grid=(m // tm, nb),
        in_specs=[pl.BlockSpec((tm, DIFF_PREP_COLS), lambda i, j: (i, j + off)),
                  pl.BlockSpec((1, DIFF_PREP_COLS), lambda i, j: (0, j)),
                  pl.BlockSpec((tm, LANES), lambda i, j: (i, 0)),
                  pl.BlockSpec((tm, LANES), lambda i, j: (i, 0))],
        out_specs=pl.BlockSpec((tm, DIFF_PREP_COLS), lambda i, j: (i, j)),
        out_shape=jax.ShapeDtypeStruct((m, n), BF16),
        compiler_params=_cparams(("parallel", "arbitrary")),
        name="diff_qk_prep",
    )(z, g, cos, sin)


def _chunk_mask(q0, k0, tq, tk):
    row = (q0 + lax.broadcasted_iota(jnp.int32, (tq, tk), 0)) >> 6
    col = (k0 + lax.broadcasted_iota(jnp.int32, (tq, tk), 1)) >> 6
    return col <= row


def _online_update(s, v, m_sc, l_sc, acc_sc):
    m_prev = m_sc[...]
    m_new = jnp.maximum(m_prev, jnp.max(s, axis=-1, keepdims=True))
    p = jnp.exp(s - m_new)
    alpha = jnp.exp(m_prev - m_new)
    l_sc[...] = alpha * l_sc[...] + jnp.sum(p, axis=-1, keepdims=True)
    acc_sc[...] = alpha * acc_sc[...] + jnp.dot(p.astype(BF16), v, preferred_element_type=F32)
    m_sc[...] = m_new


def _qk(q, k):
    return lax.dot_general(q, k, (((1,), (1,)), ((), ())), preferred_element_type=F32)


def _mla_attn_kernel(q_ref, k_ref, v_ref, o_ref, m_sc, l_sc, acc_sc, *, tq, tk):
    i = pl.program_id(1)
    q = q_ref[...]
    m_sc[...] = jnp.full(m_sc.shape, NEG, F32)
    l_sc[...] = jnp.zeros(l_sc.shape, F32)
    acc_sc[...] = jnp.zeros(acc_sc.shape, F32)

    def step(j, masked):
        k0 = pl.multiple_of(j * tk, tk)
        s = _qk(q, k_ref[pl.ds(k0, tk), :])
        if masked:
            s = jnp.where(_chunk_mask(i * tq, k0, tq, tk), s, NEG)
        _online_update(s, v_ref[pl.ds(k0, tk), :], m_sc, l_sc, acc_sc)

    n_full = i * (tq // tk)

    def body(j, carry):
        step(j, False)
        return carry

    lax.fori_loop(0, n_full, body, 0)
    for d in range(tq // tk):
        step(n_full + d, True)
    o_ref[...] = (acc_sc[...] / l_sc[...]).astype(o_ref.dtype)


def _mla_attn(q, k, v, tq, tk):
    s = q.shape[0]
    kern = functools.partial(_mla_attn_kernel, tq=tq, tk=tk)
    return pl.pallas_call(
        kern,
        grid=(MLA_HEADS, s // tq),
        in_specs=[pl.BlockSpec((tq, MLA_HEAD_PAD), lambda h, i: (i, h)),
                  pl.BlockSpec((s, MLA_HEAD_PAD), lambda h, i: (0, h)),
                  pl.BlockSpec((s, MLA_V), lambda h, i: (0, h))],
        out_specs=pl.BlockSpec((tq, MLA_V), lambda h, i: (i, h)),
        out_shape=jax.ShapeDtypeStruct((s, MLA_HEADS * MLA_V), BF16),
        scratch_shapes=[pltpu.VMEM((tq, 1), F32), pltpu.VMEM((tq, 1), F32),
                        pltpu.VMEM((tq, MLA_V), F32)],
        compiler_params=_cparams(("parallel", "arbitrary")),
        name="mla_attn",
    )(q, k, v)


def _diff_attn_kernel(q_ref, k_ref, v_ref, lq1_ref, lk1_ref, lq2_ref, lk2_ref, g_ref, o_ref,
                      m1_sc, l1_sc, a1_sc, m2_sc, l2_sc, a2_sc, *, tq, tk):
    i = pl.program_id(1)
    q1 = q_ref[:, :DIFF_HEAD_DIM]
    q2 = q_ref[:, DIFF_HEAD_DIM:]
    for m_sc, l_sc, a_sc in ((m1_sc, l1_sc, a1_sc), (m2_sc, l2_sc, a2_sc)):
        m_sc[...] = jnp.full(m_sc.shape, NEG, F32)
        l_sc[...] = jnp.zeros(l_sc.shape, F32)
        a_sc[...] = jnp.zeros(a_sc.shape, F32)

    def step(j, masked):
        k0 = pl.multiple_of(j * tk, tk)
        v = v_ref[pl.ds(k0, tk), :]
        s1 = _qk(q1, k_ref[pl.ds(k0, tk), :DIFF_HEAD_DIM])
        s2 = _qk(q2, k_ref[pl.ds(k0, tk), DIFF_HEAD_DIM:])
        if masked:
            mask = _chunk_mask(i * tq, k0, tq, tk)
            s1 = jnp.where(mask, s1, NEG)
            s2 = jnp.where(mask, s2, NEG)
        _online_update(s1, v, m1_sc, l1_sc, a1_sc)
        _online_update(s2, v, m2_sc, l2_sc, a2_sc)

    n_full = i * (tq // tk)

    def body(j, carry):
        step(j, False)
        return carry

    lax.fori_loop(0, n_full, body, 0)
    for d in range(tq // tk):
        step(n_full + d, True)

    lam = (jnp.exp(jnp.sum(lq1_ref[...] * lk1_ref[...], axis=-1, keepdims=True))
           - jnp.exp(jnp.sum(lq2_ref[...] * lk2_ref[...], axis=-1, keepdims=True))
           + LAMBDA_INIT)
    o = a1_sc[...] / l1_sc[...] - lam * (a2_sc[...] / l2_sc[...])
    ms = jnp.mean(o * o, axis=-1, keepdims=True)
    o = o * lax.rsqrt(ms + EPS) * g_ref[...]
    o_ref[...] = (o * (1.0 - LAMBDA_INIT)).astype(o_ref.dtype)


def _diff_attn(qk, z, lq1, lk1, lq2, lk2, g, tq, tk):
    s = qk.shape[0]
    kern = functools.partial(_diff_attn_kernel, tq=tq, tk=tk)
    k_off = DIFF_QK_COLS // DIFF_V
    v_off = Z_DV // DIFF_V
    vec = pl.BlockSpec((1, DIFF_HEAD_DIM), lambda h, i: (0, 0))
    return pl.pallas_call(
        kern,
        grid=(DIFF_HEADS, s // tq),
        in_specs=[pl.BlockSpec((tq, DIFF_V), lambda h, i: (i, h)),
                  pl.BlockSpec((s, DIFF_V), lambda h, i: (0, h + k_off)),
                  pl.BlockSpec((s, DIFF_V), lambda h, i: (0, h + v_off)),
                  vec, vec, vec, vec,
                  pl.BlockSpec((1, DIFF_V), lambda h, i: (0, 0))],
        out_specs=pl.BlockSpec((tq, DIFF_V), lambda h, i: (i, h)),
        out_shape=jax.ShapeDtypeStruct((s, DIFF_HEADS * DIFF_V), BF16),
        scratch_shapes=[pltpu.VMEM((tq, 1), F32), pltpu.VMEM((tq, 1), F32),
                        pltpu.VMEM((tq, DIFF_V), F32),
                        pltpu.VMEM((tq, 1), F32), pltpu.VMEM((tq, 1), F32),
                        pltpu.VMEM((tq, DIFF_V), F32)],
        compiler_params=_cparams(("parallel", "arbitrary")),
        name="diff_attn",
    )(qk, qk, z, lq1, lk1, lq2, lk2, g)


def _wo_kernel(oa_ref, ob_ref, wa_ref, wb_ref, x_ref, h_ref):
    acc = jnp.dot(oa_ref[...], wa_ref[...], preferred_element_type=F32)
    acc = acc + jnp.dot(ob_ref[...], wb_ref[...], preferred_element_type=F32)
    h_ref[...] = x_ref[...] + acc


def _wo_proj(oa, ob, wa, wb, x, tm, tn):
    m, ka = oa.shape
    kb = ob.shape[1]
    n = wa.shape[1]
    return pl.pallas_call(
        _wo_kernel,
        grid=(m // tm, n // tn),
        in_specs=[pl.BlockSpec((tm, ka), lambda i, j: (i, 0)),
                  pl.BlockSpec((tm, kb), lambda i, j: (i, 0)),
                  pl.BlockSpec((ka, tn), lambda i, j: (0, j)),
                  pl.BlockSpec((kb, tn), lambda i, j: (0, j)),
                  pl.BlockSpec((tm, tn), lambda i, j: (i, j))],
        out_specs=pl.BlockSpec((tm, tn), lambda i, j: (i, j)),
        out_shape=jax.ShapeDtypeStruct((m, n), F32),
        compiler_params=_cparams(("parallel", "arbitrary")),
        name="wo_proj",
    )(oa, ob, wa, wb, x)


def _gate_up_kernel(a_ref, wg_ref, wu_ref, o_ref):
    a = a_ref[...]
    g = jnp.dot(a, wg_ref[...], preferred_element_type=F32)
    u = jnp.dot(a, wu_ref[...], preferred_element_type=F32)
    o_ref[...] = (g * (1.0 / (1.0 + jnp.exp(-g))) * u).astype(o_ref.dtype)


def _gate_up(a, wg, wu, tm, tn):
    m, k = a.shape
    n = wg.shape[1]
    return pl.pallas_call(
        _gate_up_kernel,
        grid=(m // tm, n // tn),
        in_specs=[pl.BlockSpec((tm, k), lambda i, j: (i, 0)),
                  pl.BlockSpec((k, tn), lambda i, j: (0, j)),
                  pl.BlockSpec((k, tn), lambda i, j: (0, j))],
        out_specs=pl.BlockSpec((tm, tn), lambda i, j: (i, j)),
        out_shape=jax.ShapeDtypeStruct((m, n), BF16),
        compiler_params=_cparams(("parallel", "arbitrary")),
        name="ffn_gate_up",
    )(a, wg, wu)


def _down_kernel(a_ref, w_ref, h_ref, o_ref, acc_sc):
    kk = pl.program_id(2)

    @pl.when(kk == 0)
    def _():
        acc_sc[...] = jnp.zeros(acc_sc.shape, F32)

    acc_sc[...] += jnp.dot(a_ref[...], w_ref[...], preferred_element_type=F32)

    @pl.when(kk == pl.num_programs(2) - 1)
    def _():
        o_ref[...] = h_ref[...] + acc_sc[...]


def _down_proj(a, w, h, tm, tn, tk):
    m, k = a.shape
    n = w.shape[1]
    return pl.pallas_call(
        _down_kernel,
        grid=(m // tm, n // tn, k // tk),
        in_specs=[pl.BlockSpec((tm, tk), lambda i, j, kk: (i, kk)),
                  pl.BlockSpec((tk, tn), lambda i, j, kk: (kk, j)),
                  pl.BlockSpec((tm, tn), lambda i, j, kk: (i, j))],
        out_specs=pl.BlockSpec((tm, tn), lambda i, j, kk: (i, j)),
        out_shape=jax.ShapeDtypeStruct((m, n), F32),
        scratch_shapes=[pltpu.VMEM((tm, tn), F32)],
        compiler_params=_cparams(("parallel", "arbitrary", "arbitrary")),
        name="ffn_down",
    )(a, w, h)


def _rope_tables(s):
    pos = jnp.arange(s).astype(F32)

    def cs(half):
        inv_freq = ROPE_THETA ** (-jnp.arange(half, dtype=F32) / half)
        ang = pos[:, None] * inv_freq[None, :]
        return jnp.cos(ang), jnp.sin(ang)

    c, sn = cs(MLA_ROPE // 2)
    z32 = jnp.zeros_like(c)
    z64 = jnp.zeros((s, 64), F32)
    mla_cos = jnp.concatenate([c, c, z64], axis=-1)
    mla_sa = jnp.concatenate([-sn, z32, z64], axis=-1)
    mla_sb = jnp.concatenate([z32, sn, z64], axis=-1)
    c, sn = cs(DIFF_HEAD_DIM // 2)
    diff_cos = jnp.concatenate([c, c], axis=-1)
    diff_sin = jnp.concatenate([-sn, sn], axis=-1)
    return mla_cos, mla_sa, mla_sb, diff_cos, diff_sin


def _pad_gain(g, lo, hi, width):
    return jnp.pad(g[lo:hi], (0, width - (hi - lo))).reshape(1, width)


def kernel(x, attn_norm_g, w_in, q_latent_norm_g, kv_latent_norm_g, w_uq, w_ukv, mla_q_norm_g, mla_k_norm_g, diff_q_norm_g, diff_k_norm_g, lambda_q1, lambda_k1, lambda_q2, lambda_k2, diff_subln_g, w_o, ffn_norm_g, w_gate, w_up, w_down):
    b, s, d = x.shape
    assert b == 1 and d == D_MODEL and s % 512 == 0
    x2 = x.reshape(s, d)
    tm = min(1024, s)
    tp = min(512, s)
    tq = tk = min(512, s)

    w_in0 = w_in[0]
    w_in_p = jnp.concatenate(
        [w_in0[:, :Z_KPE + MLA_ROPE],
         jnp.zeros((d, Z_LAT - Z_KPE - MLA_ROPE), F32),
         w_in0[:, Z_KPE + MLA_ROPE:]], axis=1).astype(BF16)
    w_uq_p = jnp.pad(w_uq[0].reshape(MLA_Q_LORA, MLA_HEADS, MLA_QK),
                     ((0, 0), (0, 0), (0, MLA_HEAD_PAD - MLA_QK))
                     ).reshape(MLA_Q_LORA, MLA_HEADS * MLA_HEAD_PAD).astype(BF16)
    w_ukv3 = w_ukv[0].reshape(MLA_KV_LORA, MLA_HEADS, MLA_NOPE + MLA_V)
    w_uk = w_ukv3[:, :, :MLA_NOPE].reshape(MLA_KV_LORA, MLA_HEADS * MLA_NOPE).astype(BF16)
    w_uv = w_ukv3[:, :, MLA_NOPE:].reshape(MLA_KV_LORA, MLA_HEADS * MLA_V).astype(BF16)
    w_o0 = w_o[0].astype(BF16)
    wo_a, wo_b = w_o0[:MLA_HEADS * MLA_V], w_o0[MLA_HEADS * MLA_V:]
    ff_pad = D_FF_PAD - D_FF
    w_gate_p = jnp.pad(w_gate[0], ((0, 0), (0, ff_pad))).astype(BF16)
    w_up_p = jnp.pad(w_up[0], ((0, 0), (0, ff_pad))).astype(BF16)
    w_down_p = jnp.pad(w_down[0], ((0, ff_pad), (0, 0))).astype(BF16)

    mla_cos, mla_sa, mla_sb, diff_cos, diff_sin = _rope_tables(s)
    gq_a = _pad_gain(mla_q_norm_g[0], 0, MLA_NOPE, LANES)
    gq_b = _pad_gain(mla_q_norm_g[0], MLA_NOPE, MLA_QK, LANES)
    gk_a = _pad_gain(mla_k_norm_g[0], 0, MLA_NOPE, LANES)
    gk_b = _pad_gain(mla_k_norm_g[0], MLA_NOPE, MLA_QK, LANES)
    g_diff = jnp.concatenate([jnp.tile(diff_q_norm_g[0] * DIFF_SCALE, 2 * DIFF_HEADS),
                              jnp.tile(diff_k_norm_g[0], 2 * DIFF_HEADS)]).reshape(1, -1)

    n = _rmsnorm(x2, attn_norm_g[0], min(256, s))
    z = _matmul(n, w_in_p, tm, 768, BF16, "in_proj")
    q_a = _q_prep(z, q_latent_norm_g[0].reshape(1, -1), w_uq_p, gq_a, gq_b,
                  mla_cos, mla_sa, mla_sb, tp)
    k_a, v_a = _kv_prep(z, kv_latent_norm_g[0].reshape(1, -1), w_uk, w_uv, gk_a, gk_b,
                        mla_cos, mla_sa, mla_sb, tp)
    o_a = _mla_attn(q_a, k_a, v_a, tq, tk)
    qk_b = _diff_prep(z, g_diff, diff_cos, diff_sin, tp)
    o_b = _diff_attn(qk_b, z, lambda_q1[0].reshape(1, -1), lambda_k1[0].reshape(1, -1),
                     lambda_q2[0].reshape(1, -1), lambda_k2[0].reshape(1, -1),
                     diff_subln_g[0].reshape(1, -1), tq, tk)
    h = _wo_proj(o_a, o_b, wo_a, wo_b, x2, tm, 512)

    m = _rmsnorm(h, ffn_norm_g[0], min(256, s))
    a = _gate_up(m, w_gate_p, w_up_p, tm, 512)
    out = _down_proj(a, w_down_p, h, tm, 1024, 2816)
    return out.reshape(b, s, d)
```

```python
import functools
import math
from typing import Any, NamedTuple

import jax
import jax.numpy as jnp
from jax import lax
from jax.experimental import pallas as pl
from jax.experimental.pallas import tpu as pltpu

F32 = jnp.float32
BF16 = jnp.bfloat16

D_MODEL = 4096
CHUNK = 64
ROPE_THETA = 10000.0
EPS = 1e-6
MLA_HEADS = 16
MLA_Q_LORA = 768
MLA_KV_LORA = 512
MLA_NOPE = 128
MLA_ROPE = 64
MLA_QK = MLA_NOPE + MLA_ROPE
MLA_V = 128
LOG2E = math.log2(math.e)
MLA_SCALE = 1.0 / math.sqrt(MLA_QK)
MLA_QSCALE = MLA_SCALE * LOG2E
DIFF_HEADS = 8
DIFF_HEAD_DIM = 128
DIFF_V = 2 * DIFF_HEAD_DIM
DIFF_SCALE = 1.0 / math.sqrt(DIFF_HEAD_DIM)
DIFF_QSCALE = DIFF_SCALE * LOG2E
DIFF_QK_COLS = 2 * DIFF_HEADS * DIFF_HEAD_DIM
DIFF_V_COLS = DIFF_HEADS * DIFF_V
D_FF = -(-8 * D_MODEL // (3 * 256)) * 256
LAMBDA_INIT = 0.8 - 0.6 * math.exp(-0.3 * 0)

LANES = 128
MLA_HEAD_PAD = 2 * LANES
Z_LAT = 1536
Z_KPE = MLA_Q_LORA + MLA_KV_LORA
Z_DQ = Z_LAT
Z_DV = Z_LAT + 2 * DIFF_QK_COLS
Z_COLS = Z_DV + DIFF_V_COLS
D_FF_PAD = 11264
NEG = -1e30
VMEM_LIMIT = 56 * 1024 * 1024


def _cparams(sem):
    return pltpu.CompilerParams(dimension_semantics=sem, vmem_limit_bytes=VMEM_LIMIT)


def _rmsnorm_kernel(x_ref, g_ref, o_ref):
    x = x_ref[...]
    ms = jnp.mean(x * x, axis=-1, keepdims=True)
    o_ref[...] = (x * lax.rsqrt(ms + EPS) * g_ref[...]).astype(o_ref.dtype)


def _rmsnorm(x, g, tm):
    m, d = x.shape
    return pl.pallas_call(
        _rmsnorm_kernel,
        grid=(m // tm,),
        in_specs=[pl.BlockSpec((tm, d), lambda i: (i, 0)),
                  pl.BlockSpec((1, d), lambda i: (0, 0))],
        out_specs=pl.BlockSpec((tm, d), lambda i: (i, 0)),
        out_shape=jax.ShapeDtypeStruct((m, d), BF16),
        compiler_params=_cparams(("parallel",)),
        name="rmsnorm",
    )(x, g.reshape(1, d))


def _mm_kernel(a_ref, b_ref, o_ref):
    o_ref[...] = jnp.dot(a_ref[...], b_ref[...],
                         preferred_element_type=F32).astype(o_ref.dtype)


def _matmul(a, b, tm, tn, out_dtype, name):
    m, k = a.shape
    _, n = b.shape
    return pl.pallas_call(
        _mm_kernel,
        grid=(m // tm, n // tn),
        in_specs=[pl.BlockSpec((tm, k), lambda i, j: (i, 0)),
                  pl.BlockSpec((k, tn), lambda i, j: (0, j))],
        out_specs=pl.BlockSpec((tm, tn), lambda i, j: (i, j)),
        out_shape=jax.ShapeDtypeStruct((m, n), out_dtype),
        compiler_params=_cparams(("parallel", "arbitrary")),
        name=name,
    )(a, b)


def _latent_norm(c, g):
    ms = jnp.mean(c * c, axis=-1, keepdims=True)
    return (c * lax.rsqrt(ms + EPS) * g).astype(BF16)


def _mla_rope(y, cos, sa, sb):
    return y * cos + pltpu.roll(y, 96, 1) * sa + pltpu.roll(y, 32, 1) * sb


def _q_prep_kernel(z_ref, gl_ref, w_ref, ga_ref, gb_ref, cos_ref, sa_ref, sb_ref, o_ref):
    c = z_ref[:, :MLA_Q_LORA].astype(F32)
    lat = _latent_norm(c, gl_ref[...])
    q = jnp.dot(lat, w_ref[...], preferred_element_type=F32)
    cos, sa, sb = cos_ref[...], sa_ref[...], sb_ref[...]
    ga, gb = ga_ref[...], gb_ref[...]
    for h in range(MLA_HEADS):
        lo = h * MLA_HEAD_PAD
        qa = q[:, lo:lo + LANES]
        qb = q[:, lo + LANES:lo + 2 * LANES]
        ss = (jnp.sum(qa * qa, axis=-1, keepdims=True)
              + jnp.sum(qb * qb, axis=-1, keepdims=True))
        r = lax.rsqrt(ss * (1.0 / MLA_QK) + EPS)
        ya = qa * r * ga
        yb = _mla_rope(qb * r * gb, cos, sa, sb)
        o_ref[:, lo:lo + LANES] = (ya * MLA_QSCALE).astype(o_ref.dtype)
        o_ref[:, lo + LANES:lo + 2 * LANES] = (yb * MLA_QSCALE).astype(o_ref.dtype)


def _kv_prep_kernel(z_ref, gl_ref, wk_ref, wv_ref, ga_ref, gb_ref, cos_ref, sa_ref, sb_ref,
                    k_ref, v_ref):
    c = z_ref[:, MLA_Q_LORA:Z_KPE].astype(F32)
    lat = _latent_norm(c, gl_ref[...])
    kn = jnp.dot(lat, wk_ref[...], preferred_element_type=F32)
    v_ref[...] = jnp.dot(lat, wv_ref[...], preferred_element_type=F32).astype(v_ref.dtype)
    kpe = z_ref[:, Z_KPE:Z_KPE + LANES].astype(F32)
    ss_pe = jnp.sum(kpe * kpe, axis=-1, keepdims=True)
    pe_rot = _mla_rope(kpe * gb_ref[...], cos_ref[...], sa_ref[...], sb_ref[...])
    ga = ga_ref[...]
    for h in range(MLA_HEADS):
        a = kn[:, h * LANES:(h + 1) * LANES]
        ss = jnp.sum(a * a, axis=-1, keepdims=True) + ss_pe
        r = lax.rsqrt(ss * (1.0 / MLA_QK) + EPS)
        lo = h * MLA_HEAD_PAD
        k_ref[:, lo:lo + LANES] = (a * r * ga).astype(k_ref.dtype)
        k_ref[:, lo + LANES:lo + 2 * LANES] = (pe_rot * r).astype(k_ref.dtype)


def _row_spec(tm, n, col=0):
    return pl.BlockSpec((tm, n), lambda i: (i, col))


def _const_spec(shape):
    return pl.BlockSpec(shape, lambda i: (0, 0))


def _q_prep(z, gl, w, ga, gb, cos, sa, sb, tm):
    m = z.shape[0]
    n = MLA_HEADS * MLA_HEAD_PAD
    return pl.pallas_call(
        _q_prep_kernel,
        grid=(m // tm,),
        in_specs=[_row_spec(tm, Z_LAT), _const_spec(gl.shape), _const_spec(w.shape),
                  _const_spec(ga.shape), _const_spec(gb.shape),
                  _row_spec(tm, LANES), _row_spec(tm, LANES), _row_spec(tm, LANES)],
        out_specs=_row_spec(tm, n),
        out_shape=jax.ShapeDtypeStruct((m, n), BF16),
        compiler_params=_cparams(("parallel",)),
        name="mla_q_prep",
    )(z, gl, w, ga, gb, cos, sa, sb)


def _kv_prep(z, gl, wk, wv, ga, gb, cos, sa, sb, tm):
    m = z.shape[0]
    nk = MLA_HEADS * MLA_HEAD_PAD
    nv = MLA_HEADS * MLA_V
    return pl.pallas_call(
        _kv_prep_kernel,
        grid=(m // tm,),
        in_specs=[_row_spec(tm, Z_LAT), _const_spec(gl.shape), _const_spec(wk.shape),
                  _const_spec(wv.shape), _const_spec(ga.shape), _const_spec(gb.shape),
                  _row_spec(tm, LANES), _row_spec(tm, LANES), _row_spec(tm, LANES)],
        out_specs=[_row_spec(tm, nk), _row_spec(tm, nv)],
        out_shape=[jax.ShapeDtypeStruct((m, nk), BF16), jax.ShapeDtypeStruct((m, nv), BF16)],
        compiler_params=_cparams(("parallel",)),
        name="mla_kv_prep",
    )(z, gl, wk, wv, ga, gb, cos, sa, sb)


DIFF_PREP_COLS = 512


def _diff_prep_kernel(x_ref, g_ref, cos_ref, sin_ref, o_ref):
    cos, sin = cos_ref[...], sin_ref[...]
    for c in range(DIFF_PREP_COLS // LANES):
        sl = slice(c * LANES, (c + 1) * LANES)
        x = x_ref[:, sl].astype(F32)
        ms = jnp.mean(x * x, axis=-1, keepdims=True)
        y = x * lax.rsqrt(ms + EPS) * g_ref[:, sl]
        o_ref[:, sl] = (y * cos + pltpu.roll(y, 64, 1) * sin).astype(o_ref.dtype)


def _diff_prep(z, g, cos, sin, tm):
    m = z.shape[0]
    n = 2 * DIFF_QK_COLS
    nb = n // DIFF_PREP_COLS
    off = Z_DQ // DIFF_PREP_COLS
    return pl.pallas_call(
        _diff_prep_kernel,
        grid=(m // tm, nb),
        in_specs=[pl.BlockSpec((tm, DIFF_PREP_COLS), lambda i, j: (i, j + off)),
                  pl.BlockSpec((1, DIFF_PREP_COLS), lambda i, j: (0, j)),
                  pl.BlockSpec((tm, LANES), lambda i, j: (i, 0)),
                  pl.BlockSpec((tm, LANES), lambda i, j: (i, 0))],
        out_specs=pl.BlockSpec((tm, DIFF_PREP_COLS), lambda i, j: (i, j)),
        out_shape=jax.ShapeDtypeStruct((m, n), BF16),
        compiler_params=_cparams(("parallel", "arbitrary")),
        name="diff_qk_prep",
    )(z, g, cos, sin)


def _chunk_mask_t(q0, k0, tq, tk):
    key = (k0 + lax.broadcasted_iota(jnp.int32, (tk, tq), 0)) >> 6
    qry = (q0 + lax.broadcasted_iota(jnp.int32, (tk, tq), 1)) >> 6
    return key <= qry


def _scores_t(k, q):
    return lax.dot_general(k, q, (((1,), (1,)), ((), ())), preferred_element_type=F32)


def _online_update_t(s_ref, mask, v, m_sc, l_sc, acc_sc):
    def scores():
        s_t = s_ref[...]
        return s_t if mask is None else jnp.where(mask, s_t, NEG)

    m_prev = m_sc[...]
    m_new = jnp.maximum(m_prev, jnp.max(scores(), axis=0, keepdims=True))
    p_t = jnp.exp2(scores() - m_new)
    alpha = jnp.exp2(m_prev - m_new)
    l_sc[...] = alpha * l_sc[...] + jnp.sum(p_t, axis=0, keepdims=True)
    pv = lax.dot_general(v, p_t.astype(BF16), (((0,), (0,)), ((), ())),
                         preferred_element_type=F32)
    acc_sc[...] = alpha * acc_sc[...] + pv
    m_sc[...] = m_new


def _init_state(m_sc, l_sc, acc_sc):
    m_sc[...] = jnp.full(m_sc.shape, NEG, F32)
    l_sc[...] = jnp.zeros(l_sc.shape, F32)
    acc_sc[...] = jnp.zeros(acc_sc.shape, F32)


class _Chain(NamedTuple):
    q: Any
    k_blk: Any
    v_blk: Any
    m: Any
    l: Any
    acc: Any
    s_a: Any
    s_b: Any


def _flash_pairs(i, tq, tk, chains):
    assert tq == 2 * tk

    def k_start(j):
        return pl.multiple_of(j * tk, tk)

    def qk(use_b, j):
        k0 = k_start(j)
        for c in chains:
            (c.s_b if use_b else c.s_a)[...] = _scores_t(c.k_blk(k0), c.q)

    def update(use_b, j, masked):
        k0 = k_start(j)
        mask = _chunk_mask_t(i * tq, k0, tq, tk) if masked else None
        for c in chains:
            _online_update_t(c.s_b if use_b else c.s_a, mask, c.v_blk(k0), c.m, c.l, c.acc)

    for c in chains:
        _init_state(c.m, c.l, c.acc)
    qk(False, 0)

    def body(p, carry):
        qk(True, 2 * p + 1)
        update(False, 2 * p, False)
        qk(False, 2 * p + 2)
        update(True, 2 * p + 1, False)
        return carry

    lax.fori_loop(0, i, body, 0)
    qk(True, 2 * i + 1)
    update(False, 2 * i, True)
    update(True, 2 * i + 1, True)


MLA_HEADS_PER_STEP = 1


def _mla_attn_kernel(q_ref, k_ref, v_ref, o_ref, m_sc, l_sc, acc_sc, sa_sc, sb_sc, *, tq, tk):
    nh = m_sc.shape[0]
    chains = []
    for h in range(nh):
        qk_cols = slice(h * MLA_HEAD_PAD, (h + 1) * MLA_HEAD_PAD)
        v_cols = slice(h * MLA_V, (h + 1) * MLA_V)
        chains.append(_Chain(
            q=q_ref[:, qk_cols],
            k_blk=functools.partial(lambda k0, cols: k_ref[pl.ds(k0, tk), cols], cols=qk_cols),
            v_blk=functools.partial(lambda k0, cols: v_ref[pl.ds(k0, tk), cols], cols=v_cols),
            m=m_sc.at[h], l=l_sc.at[h], acc=acc_sc.at[h], s_a=sa_sc.at[h], s_b=sb_sc.at[h]))
    _flash_pairs(pl.program_id(1), tq, tk, chains)
    for h in range(nh):
        o_ref[:, h * MLA_V:(h + 1) * MLA_V] = (acc_sc[h] / l_sc[h]).T.astype(o_ref.dtype)


def _mla_attn(q, k, v, tq, tk):
    s = q.shape[0]
    nh = MLA_HEADS_PER_STEP
    kern = functools.partial(_mla_attn_kernel, tq=tq, tk=tk)
    return pl.pallas_call(
        kern,
        grid=(MLA_HEADS // nh, s // tq),
        in_specs=[pl.BlockSpec((tq, nh * MLA_HEAD_PAD), lambda h, i: (i, h)),
                  pl.BlockSpec((s, nh * MLA_HEAD_PAD), lambda h, i: (0, h)),
                  pl.BlockSpec((s, nh * MLA_V), lambda h, i: (0, h))],
        out_specs=pl.BlockSpec((tq, nh * MLA_V), lambda h, i: (i, h)),
        out_shape=jax.ShapeDtypeStruct((s, MLA_HEADS * MLA_V), BF16),
        scratch_shapes=[pltpu.VMEM((nh, 1, tq), F32), pltpu.VMEM((nh, 1, tq), F32),
                        pltpu.VMEM((nh, MLA_V, tq), F32),
                        pltpu.VMEM((nh, tk, tq), F32), pltpu.VMEM((nh, tk, tq), F32)],
        compiler_params=_cparams(("parallel", "arbitrary")),
        name="mla_attn",
    )(q, k, v)


def _diff_attn_kernel(q_ref, k_ref, v_ref, lq1_ref, lk1_ref, lq2_ref, lk2_ref, g_ref, o_ref,
                      m_sc, l_sc, acc_sc, sa_sc, sb_sc, *, tq, tk):
    chains = []
    for t in range(2):
        cols = slice(t * DIFF_HEAD_DIM, (t + 1) * DIFF_HEAD_DIM)
        chains.append(_Chain(
            q=q_ref[:, cols],
            k_blk=functools.partial(lambda k0, cols: k_ref[pl.ds(k0, tk), cols], cols=cols),
            v_blk=lambda k0: v_ref[pl.ds(k0, tk), :],
            m=m_sc.at[t], l=l_sc.at[t], acc=acc_sc.at[t], s_a=sa_sc.at[t], s_b=sb_sc.at[t]))
    _flash_pairs(pl.program_id(1), tq, tk, chains)

    lam = (jnp.exp(jnp.sum(lq1_ref[...] * lk1_ref[...], axis=-1, keepdims=True))
           - jnp.exp(jnp.sum(lq2_ref[...] * lk2_ref[...], axis=-1, keepdims=True))
           + LAMBDA_INIT)
    o = (acc_sc[0] / l_sc[0] - lam * (acc_sc[1] / l_sc[1])).T
    ms = jnp.mean(o * o, axis=-1, keepdims=True)
    o = o * lax.rsqrt(ms + EPS) * g_ref[...]
    o_ref[...] = (o * (1.0 - LAMBDA_INIT)).astype(o_ref.dtype)


def _diff_attn(qk, z, lq1, lk1, lq2, lk2, g, tq, tk):
    s = qk.shape[0]
    kern = functools.partial(_diff_attn_kernel, tq=tq, tk=tk)
    k_off = DIFF_QK_COLS // DIFF_V
    v_off = Z_DV // DIFF_V
    vec = pl.BlockSpec((1, DIFF_HEAD_DIM), lambda h, i: (0, 0))
    return pl.pallas_call(
        kern,
        grid=(DIFF_HEADS, s // tq),
        in_specs=[pl.BlockSpec((tq, DIFF_V), lambda h, i: (i, h)),
                  pl.BlockSpec((s, DIFF_V), lambda h, i: (0, h + k_off)),
                  pl.BlockSpec((s, DIFF_V), lambda h, i: (0, h + v_off)),
                  vec, vec, vec, vec,
                  pl.BlockSpec((1, DIFF_V), lambda h, i: (0, 0))],
        out_specs=pl.BlockSpec((tq, DIFF_V), lambda h, i: (i, h)),
        out_shape=jax.ShapeDtypeStruct((s, DIFF_HEADS * DIFF_V), BF16),
        scratch_shapes=[pltpu.VMEM((2, 1, tq), F32), pltpu.VMEM((2, 1, tq), F32),
                        pltpu.VMEM((2, DIFF_V, tq), F32),
                        pltpu.VMEM((2, tk, tq), F32), pltpu.VMEM((2, tk, tq), F32)],
        compiler_params=_cparams(("parallel", "arbitrary")),
        name="diff_attn",
    )(qk, qk, z, lq1, lk1, lq2, lk2, g)


def _wo_kernel(oa_ref, ob_ref, wa_ref, wb_ref, x_ref, h_ref):
    acc = jnp.dot(oa_ref[...], wa_ref[...], preferred_element_type=F32)
    acc = acc + jnp.dot(ob_ref[...], wb_ref[...], preferred_element_type=F32)
    h_ref[...] = x_ref[...] + acc


def _wo_proj(oa, ob, wa, wb, x, tm, tn):
    m, ka = oa.shape
    kb = ob.shape[1]
    n = wa.shape[1]
    return pl.pallas_call(
        _wo_kernel,
        grid=(m // tm, n // tn),
        in_specs=[pl.BlockSpec((tm, ka), lambda i, j: (i, 0)),
                  pl.BlockSpec((tm, kb), lambda i, j: (i, 0)),
                  pl.BlockSpec((ka, tn), lambda i, j: (0, j)),
                  pl.BlockSpec((kb, tn), lambda i, j: (0, j)),
                  pl.BlockSpec((tm, tn), lambda i, j: (i, j))],
        out_specs=pl.BlockSpec((tm, tn), lambda i, j: (i, j)),
        out_shape=jax.ShapeDtypeStruct((m, n), F32),
        compiler_params=_cparams(("parallel", "arbitrary")),
        name="wo_proj",
    )(oa, ob, wa, wb, x)


def _gate_up_kernel(a_ref, wg_ref, wu_ref, o_ref):
    a = a_ref[...]
    g = jnp.dot(a, wg_ref[...], preferred_element_type=F32)
    u = jnp.dot(a, wu_ref[...], preferred_element_type=F32)
    o_ref[...] = (g * (1.0 / (1.0 + jnp.exp(-g))) * u).astype(o_ref.dtype)


def _gate_up(a, wg, wu, tm, tn):
    m, k = a.shape
    n = wg.shape[1]
    return pl.pallas_call(
        _gate_up_kernel,
        grid=(m // tm, n // tn),
        in_specs=[pl.BlockSpec((tm, k), lambda i, j: (i, 0)),
                  pl.BlockSpec((k, tn), lambda i, j: (0, j)),
                  pl.BlockSpec((k, tn), lambda i, j: (0, j))],
        out_specs=pl.BlockSpec((tm, tn), lambda i, j: (i, j)),
        out_shape=jax.ShapeDtypeStruct((m, n), BF16),
        compiler_params=_cparams(("parallel", "arbitrary")),
        name="ffn_gate_up",
    )(a, wg, wu)


def _down_kernel(a_ref, w_ref, h_ref, o_ref, acc_sc):
    kk = pl.program_id(2)

    @pl.when(kk == 0)
    def _():
        acc_sc[...] = jnp.zeros(acc_sc.shape, F32)

    acc_sc[...] += jnp.dot(a_ref[...], w_ref[...], preferred_element_type=F32)

    @pl.when(kk == pl.num_programs(2) - 1)
    def _():
        o_ref[...] = h_ref[...] + acc_sc[...]


def _down_proj(a, w, h, tm, tn, tk):
    m, k = a.shape
    n = w.shape[1]
    return pl.pallas_call(
        _down_kernel,
        grid=(m // tm, n // tn, k // tk),
        in_specs=[pl.BlockSpec((tm, tk), lambda i, j, kk: (i, kk)),
                  pl.BlockSpec((tk, tn), lambda i, j, kk: (kk, j)),
                  pl.BlockSpec((tm, tn), lambda i, j, kk: (i, j))],
        out_specs=pl.BlockSpec((tm, tn), lambda i, j, kk: (i, j)),
        out_shape=jax.ShapeDtypeStruct((m, n), F32),
        scratch_shapes=[pltpu.VMEM((tm, tn), F32)],
        compiler_params=_cparams(("parallel", "arbitrary", "arbitrary")),
        name="ffn_down",
    )(a, w, h)


def _rope_tables(s):
    pos = jnp.arange(s).astype(F32)

    def cs(half):
        inv_freq = ROPE_THETA ** (-jnp.arange(half, dtype=F32) / half)
        ang = pos[:, None] * inv_freq[None, :]
        return jnp.cos(ang), jnp.sin(ang)

    c, sn = cs(MLA_ROPE // 2)
    z32 = jnp.zeros_like(c)
    z64 = jnp.zeros((s, 64), F32)
    mla_cos = jnp.concatenate([c, c, z64], axis=-1)
    mla_sa = jnp.concatenate([-sn, z32, z64], axis=-1)
    mla_sb = jnp.concatenate([z32, sn, z64], axis=-1)
    c, sn = cs(DIFF_HEAD_DIM // 2)
    diff_cos = jnp.concatenate([c, c], axis=-1)
    diff_sin = jnp.concatenate([-sn, sn], axis=-1)
    return mla_cos, mla_sa, mla_sb, diff_cos, diff_sin


def _pad_gain(g, lo, hi, width):
    return jnp.pad(g[lo:hi], (0, width - (hi - lo))).reshape(1, width)


def kernel(x, attn_norm_g, w_in, q_latent_norm_g, kv_latent_norm_g, w_uq, w_ukv, mla_q_norm_g, mla_k_norm_g, diff_q_norm_g, diff_k_norm_g, lambda_q1, lambda_k1, lambda_q2, lambda_k2, diff_subln_g, w_o, ffn_norm_g, w_gate, w_up, w_down):
    b, s, d = x.shape
    assert b == 1 and d == D_MODEL and s % 512 == 0
    x2 = x.reshape(s, d)
    tm = min(1024, s)
    tp = min(512, s)
    tq = min(1024, s)
    tk = tq // 2

    w_in0 = w_in[0]
    w_in_p = jnp.concatenate(
        [w_in0[:, :Z_KPE + MLA_ROPE],
         jnp.zeros((d, Z_LAT - Z_KPE - MLA_ROPE), F32),
         w_in0[:, Z_KPE + MLA_ROPE:]], axis=1).astype(BF16)
    w_uq_p = jnp.pad(w_uq[0].reshape(MLA_Q_LORA, MLA_HEADS, MLA_QK),
                     ((0, 0), (0, 0), (0, MLA_HEAD_PAD - MLA_QK))
                     ).reshape(MLA_Q_LORA, MLA_HEADS * MLA_HEAD_PAD).astype(BF16)
    w_ukv3 = w_ukv[0].reshape(MLA_KV_LORA, MLA_HEADS, MLA_NOPE + MLA_V)
    w_uk = w_ukv3[:, :, :MLA_NOPE].reshape(MLA_KV_LORA, MLA_HEADS * MLA_NOPE).astype(BF16)
    w_uv = w_ukv3[:, :, MLA_NOPE:].reshape(MLA_KV_LORA, MLA_HEADS * MLA_V).astype(BF16)
    w_o0 = w_o[0].astype(BF16)
    wo_a, wo_b = w_o0[:MLA_HEADS * MLA_V], w_o0[MLA_HEADS * MLA_V:]
    ff_pad = D_FF_PAD - D_FF
    w_gate_p = jnp.pad(w_gate[0], ((0, 0), (0, ff_pad))).astype(BF16)
    w_up_p = jnp.pad(w_up[0], ((0, 0), (0, ff_pad))).astype(BF16)
    w_down_p = jnp.pad(w_down[0], ((0, ff_pad), (0, 0))).astype(BF16)

    mla_cos, mla_sa, mla_sb, diff_cos, diff_sin = _rope_tables(s)
    gq_a = _pad_gain(mla_q_norm_g[0], 0, MLA_NOPE, LANES)
    gq_b = _pad_gain(mla_q_norm_g[0], MLA_NOPE, MLA_QK, LANES)
    gk_a = _pad_gain(mla_k_norm_g[0], 0, MLA_NOPE, LANES)
    gk_b = _pad_gain(mla_k_norm_g[0], MLA_NOPE, MLA_QK, LANES)
    g_diff = jnp.concatenate([jnp.tile(diff_q_norm_g[0] * DIFF_QSCALE, 2 * DIFF_HEADS),
                              jnp.tile(diff_k_norm_g[0], 2 * DIFF_HEADS)]).reshape(1, -1)

    n = _rmsnorm(x2, attn_norm_g[0], min(256, s))
    z = _matmul(n, w_in_p, tm, 768, BF16, "in_proj")
    q_a = _q_prep(z, q_latent_norm_g[0].reshape(1, -1), w_uq_p, gq_a, gq_b,
                  mla_cos, mla_sa, mla_sb, tp)
    k_a, v_a = _kv_prep(z, kv_latent_norm_g[0].reshape(1, -1), w_uk, w_uv, gk_a, gk_b,
                        mla_cos, mla_sa, mla_sb, tp)
    o_a = _mla_attn(q_a, k_a, v_a, tq, tk)
    qk_b = _diff_prep(z, g_diff, diff_cos, diff_sin, tp)
    o_b = _diff_attn(qk_b, z, lambda_q1[0].reshape(1, -1), lambda_k1[0].reshape(1, -1),
                     lambda_q2[0].reshape(1, -1), lambda_k2[0].reshape(1, -1),
                     diff_subln_g[0].reshape(1, -1), tq, tk)
    h = _wo_proj(o_a, o_b, wo_a, wo_b, x2, tm, 512)

    m = _rmsnorm(h, ffn_norm_g[0], min(256, s))
    a = _gate_up(m, w_gate_p, w_up_p, tm, 512)
    out = _down_proj(a, w_down_p, h, tm, 1024, 2816)
    return out.reshape(b, s, d)
```

```python
import functools
import math
from typing import Any, NamedTuple

import jax
import jax.numpy as jnp
from jax import lax
from jax.experimental import pallas as pl
from jax.experimental.pallas import tpu as pltpu

F32 = jnp.float32
BF16 = jnp.bfloat16

D_MODEL = 4096
CHUNK = 64
ROPE_THETA = 10000.0
EPS = 1e-6
MLA_HEADS = 16
MLA_Q_LORA = 768
MLA_KV_LORA = 512
MLA_NOPE = 128
MLA_ROPE = 64
MLA_QK = MLA_NOPE + MLA_ROPE
MLA_V = 128
LOG2E = math.log2(math.e)
MLA_SCALE = 1.0 / math.sqrt(MLA_QK)
MLA_QSCALE = MLA_SCALE * LOG2E
DIFF_HEADS = 8
DIFF_HEAD_DIM = 128
DIFF_V = 2 * DIFF_HEAD_DIM
DIFF_SCALE = 1.0 / math.sqrt(DIFF_HEAD_DIM)
DIFF_QSCALE = DIFF_SCALE * LOG2E
DIFF_QK_COLS = 2 * DIFF_HEADS * DIFF_HEAD_DIM
DIFF_V_COLS = DIFF_HEADS * DIFF_V
D_FF = -(-8 * D_MODEL // (3 * 256)) * 256
LAMBDA_INIT = 0.8 - 0.6 * math.exp(-0.3 * 0)

LANES = 128
MLA_HEAD_PAD = 2 * LANES
Z_LAT = 1536
Z_KPE = MLA_Q_LORA + MLA_KV_LORA
Z_LAT_USED = Z_KPE + MLA_ROPE
NEG = -1e30
VMEM_LIMIT = 56 * 1024 * 1024


def _cparams(sem):
    return pltpu.CompilerParams(dimension_semantics=sem, vmem_limit_bytes=VMEM_LIMIT)


def _cast_kernel(w_ref, o_ref):
    o_ref[...] = w_ref[...].astype(o_ref.dtype)


def _cast_bf16(w, tr, tc):
    r, c = w.shape
    return pl.pallas_call(
        _cast_kernel,
        grid=(r // tr, c // tc),
        in_specs=[pl.BlockSpec((tr, tc), lambda i, j: (i, j))],
        out_specs=pl.BlockSpec((tr, tc), lambda i, j: (i, j)),
        out_shape=jax.ShapeDtypeStruct((r, c), BF16),
        compiler_params=_cparams(("parallel", "parallel")),
        name="cast_bf16",
    )(w)


def _rmsnorm_kernel(x_ref, g_ref, o_ref):
    x = x_ref[...]
    ms = jnp.mean(x * x, axis=-1, keepdims=True)
    o_ref[...] = (x * lax.rsqrt(ms + EPS) * g_ref[...]).astype(o_ref.dtype)


def _rmsnorm(x, g, tm):
    m, d = x.shape
    return pl.pallas_call(
        _rmsnorm_kernel,
        grid=(m // tm,),
        in_specs=[pl.BlockSpec((tm, d), lambda i: (i, 0)),
                  pl.BlockSpec((1, d), lambda i: (0, 0))],
        out_specs=pl.BlockSpec((tm, d), lambda i: (i, 0)),
        out_shape=jax.ShapeDtypeStruct((m, d), BF16),
        compiler_params=_cparams(("parallel",)),
        name="rmsnorm",
    )(x, g.reshape(1, d))


def _mm_kernel(a_ref, b_ref, o_ref):
    o_ref[...] = jnp.dot(a_ref[...], b_ref[...],
                         preferred_element_type=F32).astype(o_ref.dtype)


def _matmul(a, b, tm, tn, out_dtype, name):
    m, k = a.shape
    _, n = b.shape
    return pl.pallas_call(
        _mm_kernel,
        grid=(m // tm, n // tn),
        in_specs=[pl.BlockSpec((tm, k), lambda i, j: (i, 0)),
                  pl.BlockSpec((k, tn), lambda i, j: (0, j))],
        out_specs=pl.BlockSpec((tm, tn), lambda i, j: (i, j)),
        out_shape=jax.ShapeDtypeStruct((m, n), out_dtype),
        compiler_params=_cparams(("parallel", "arbitrary")),
        name=name,
    )(a, b)


def _latent_norm(c, g):
    ms = jnp.mean(c * c, axis=-1, keepdims=True)
    return (c * lax.rsqrt(ms + EPS) * g).astype(BF16)


def _mla_rope(y, cos, sa, sb):
    return y * cos + pltpu.roll(y, 96, 1) * sa + pltpu.roll(y, 32, 1) * sb


def _q_prep_kernel(z_ref, gl_ref, w_ref, ga_ref, gb_ref, cos_ref, sa_ref, sb_ref, o_ref):
    c = z_ref[:, :MLA_Q_LORA].astype(F32)
    lat = _latent_norm(c, gl_ref[...])
    q = jnp.dot(lat, w_ref[...], preferred_element_type=F32)
    cos, sa, sb = cos_ref[...], sa_ref[...], sb_ref[...]
    ga, gb = ga_ref[...], gb_ref[...]
    for h in range(MLA_HEADS):
        lo = h * MLA_HEAD_PAD
        qa = q[:, lo:lo + LANES]
        qb = q[:, lo + LANES:lo + 2 * LANES]
        ss = (jnp.sum(qa * qa, axis=-1, keepdims=True)
              + jnp.sum(qb * qb, axis=-1, keepdims=True))
        r = lax.rsqrt(ss * (1.0 / MLA_QK) + EPS)
        ya = qa * r * ga
        yb = _mla_rope(qb * r * gb, cos, sa, sb)
        o_ref[:, lo:lo + LANES] = (ya * MLA_QSCALE).astype(o_ref.dtype)
        o_ref[:, lo + LANES:lo + 2 * LANES] = (yb * MLA_QSCALE).astype(o_ref.dtype)


def _kv_prep_kernel(z_ref, gl_ref, wk_ref, wv_ref, ga_ref, gb_ref, cos_ref, sa_ref, sb_ref,
                    k_ref, v_ref):
    c = z_ref[:, MLA_Q_LORA:Z_KPE].astype(F32)
    lat = _latent_norm(c, gl_ref[...])
    kn = jnp.dot(lat, wk_ref[...], preferred_element_type=F32)
    v_ref[...] = jnp.dot(lat, wv_ref[...], preferred_element_type=F32).astype(v_ref.dtype)
    kpe = z_ref[:, Z_KPE:Z_KPE + LANES].astype(F32)
    ss_pe = jnp.sum(kpe * kpe, axis=-1, keepdims=True)
    pe_rot = _mla_rope(kpe * gb_ref[...], cos_ref[...], sa_ref[...], sb_ref[...])
    ga = ga_ref[...]
    for h in range(MLA_HEADS):
        a = kn[:, h * LANES:(h + 1) * LANES]
        ss = jnp.sum(a * a, axis=-1, keepdims=True) + ss_pe
        r = lax.rsqrt(ss * (1.0 / MLA_QK) + EPS)
        lo = h * MLA_HEAD_PAD
        k_ref[:, lo:lo + LANES] = (a * r * ga).astype(k_ref.dtype)
        k_ref[:, lo + LANES:lo + 2 * LANES] = (pe_rot * r).astype(k_ref.dtype)


def _row_spec(tm, n, col=0):
    return pl.BlockSpec((tm, n), lambda i: (i, col))


def _const_spec(shape):
    return pl.BlockSpec(shape, lambda i: (0, 0))


def _q_prep(z, gl, w, ga, gb, cos, sa, sb, tm):
    m = z.shape[0]
    n = MLA_HEADS * MLA_HEAD_PAD
    return pl.pallas_call(
        _q_prep_kernel,
        grid=(m // tm,),
        in_specs=[_row_spec(tm, Z_LAT), _const_spec(gl.shape), _const_spec(w.shape),
                  _const_spec(ga.shape), _const_spec(gb.shape),
                  _row_spec(tm, LANES), _row_spec(tm, LANES), _row_spec(tm, LANES)],
        out_specs=_row_spec(tm, n),
        out_shape=jax.ShapeDtypeStruct((m, n), BF16),
        compiler_params=_cparams(("parallel",)),
        name="mla_q_prep",
    )(z, gl, w, ga, gb, cos, sa, sb)


def _kv_prep(z, gl, wk, wv, ga, gb, cos, sa, sb, tm):
    m = z.shape[0]
    nk = MLA_HEADS * MLA_HEAD_PAD
    nv = MLA_HEADS * MLA_V
    return pl.pallas_call(
        _kv_prep_kernel,
        grid=(m // tm,),
        in_specs=[_row_spec(tm, Z_LAT), _const_spec(gl.shape), _const_spec(wk.shape),
                  _const_spec(wv.shape), _const_spec(ga.shape), _const_spec(gb.shape),
                  _row_spec(tm, LANES), _row_spec(tm, LANES), _row_spec(tm, LANES)],
        out_specs=[_row_spec(tm, nk), _row_spec(tm, nv)],
        out_shape=[jax.ShapeDtypeStruct((m, nk), BF16), jax.ShapeDtypeStruct((m, nv), BF16)],
        compiler_params=_cparams(("parallel",)),
        name="mla_kv_prep",
    )(z, gl, wk, wv, ga, gb, cos, sa, sb)


DIFF_PREP_COLS = 512


def _diff_prep_kernel(x_ref, g_ref, cos_ref, sin_ref, o_ref):
    cos, sin = cos_ref[...], sin_ref[...]
    for c in range(DIFF_PREP_COLS // LANES):
        sl = slice(c * LANES, (c + 1) * LANES)
        x = x_ref[:, sl].astype(F32)
        ms = jnp.mean(x * x, axis=-1, keepdims=True)
        y = x * lax.rsqrt(ms + EPS) * g_ref[:, sl]
        o_ref[:, sl] = (y * cos + pltpu.roll(y, 64, 1) * sin).astype(o_ref.dtype)


def _diff_prep(z, g, cos, sin, tm):
    m = z.shape[0]
    n = 2 * DIFF_QK_COLS
    nb = n // DIFF_PREP_COLS
    return pl.pallas_call(
        _diff_prep_kernel,
        grid=(m // tm, nb),
        in_specs=[pl.BlockSpec((tm, DIFF_PREP_COLS), lambda i, j: (i, j)),
                  pl.BlockSpec((1, DIFF_PREP_COLS), lambda i, j: (0, j)),
                  pl.BlockSpec((tm, LANES), lambda i, j: (i, 0)),
                  pl.BlockSpec((tm, LANES), lambda i, j: (i, 0))],
        out_specs=pl.BlockSpec((tm, DIFF_PREP_COLS), lambda i, j: (i, j)),
        out_shape=jax.ShapeDtypeStruct((m, n), BF16),
        compiler_params=_cparams(("parallel", "arbitrary")),
        name="diff_qk_prep",
    )(z, g, cos, sin)


def _chunk_mask_t(q0, k0, tq, tk):
    key = (k0 + lax.broadcasted_iota(jnp.int32, (tk, tq), 0)) >> 6
    qry = (q0 + lax.broadcasted_iota(jnp.int32, (tk, tq), 1)) >> 6
    return key <= qry


def _scores_t(k, q):
    return lax.dot_general(k, q, (((1,), (1,)), ((), ())), preferred_element_type=F32)


def _online_update_t(s_ref, mask, v, m_sc, l_sc, acc_sc):
    def scores():
        s_t = s_ref[...]
        return s_t if mask is None else jnp.where(mask, s_t, NEG)

    m_prev = m_sc[...]
    m_new = jnp.maximum(m_prev, jnp.max(scores(), axis=0, keepdims=True))
    p_t = jnp.exp2(scores() - m_new)
    alpha = jnp.exp2(m_prev - m_new)
    l_sc[...] = alpha * l_sc[...] + jnp.sum(p_t, axis=0, keepdims=True)
    pv = lax.dot_general(v, p_t.astype(BF16), (((0,), (0,)), ((), ())),
                         preferred_element_type=F32)
    acc_sc[...] = alpha * acc_sc[...] + pv
    m_sc[...] = m_new


def _init_state(m_sc, l_sc, acc_sc):
    m_sc[...] = jnp.full(m_sc.shape, NEG, F32)
    l_sc[...] = jnp.zeros(l_sc.shape, F32)
    acc_sc[...] = jnp.zeros(acc_sc.shape, F32)


class _Chain(NamedTuple):
    q: Any
    k_blk: Any
    v_blk: Any
    m: Any
    l: Any
    acc: Any
    s_a: Any
    s_b: Any


def _flash_pairs(i, tq, tk, chains):
    assert tq == 2 * tk

    def k_start(j):
        return pl.multiple_of(j * tk, tk)

    def qk(use_b, j):
        k0 = k_start(j)
        for c in chains:
            (c.s_b if use_b else c.s_a)[...] = _scores_t(c.k_blk(k0), c.q)

    def update(use_b, j, masked):
        k0 = k_start(j)
        mask = _chunk_mask_t(i * tq, k0, tq, tk) if masked else None
        for c in chains:
            _online_update_t(c.s_b if use_b else c.s_a, mask, c.v_blk(k0), c.m, c.l, c.acc)

    for c in chains:
        _init_state(c.m, c.l, c.acc)
    qk(False, 0)

    def body(p, carry):
        qk(True, 2 * p + 1)
        update(False, 2 * p, False)
        qk(False, 2 * p + 2)
        update(True, 2 * p + 1, False)
        return carry

    lax.fori_loop(0, i, body, 0)
    half = pl.ds(tk, tk)
    k_last = k_start(2 * i + 1)
    for c in chains:
        c.s_b.at[:, half][...] = _scores_t(c.k_blk(k_last), c.q[tk:, :])
    update(False, 2 * i, True)
    mask_last = _chunk_mask_t(i * tq + tk, k_last, tk, tk)
    for c in chains:
        _online_update_t(c.s_b.at[:, half], mask_last, c.v_blk(k_last),
                         c.m.at[:, half], c.l.at[:, half], c.acc.at[:, half])


MLA_HEADS_PER_STEP = 2


def _mla_attn_kernel(q_ref, k_ref, v_ref, o_ref, m_sc, l_sc, acc_sc, sa_sc, sb_sc, *, tq, tk):
    nh = m_sc.shape[0]
    chains = []
    for h in range(nh):
        qk_cols = slice(h * MLA_HEAD_PAD, (h + 1) * MLA_HEAD_PAD)
        v_cols = slice(h * MLA_V, (h + 1) * MLA_V)
        chains.append(_Chain(
            q=q_ref[:, qk_cols],
            k_blk=functools.partial(lambda k0, cols: k_ref[pl.ds(k0, tk), cols], cols=qk_cols),
            v_blk=functools.partial(lambda k0, cols: v_ref[pl.ds(k0, tk), cols], cols=v_cols),
            m=m_sc.at[h], l=l_sc.at[h], acc=acc_sc.at[h], s_a=sa_sc.at[h], s_b=sb_sc.at[h]))
    _flash_pairs(pl.program_id(1), tq, tk, chains)
    for h in range(nh):
        o_ref[:, h * MLA_V:(h + 1) * MLA_V] = (acc_sc[h] / l_sc[h]).T.astype(o_ref.dtype)


def _mla_attn(q, k, v, tq, tk):
    s = q.shape[0]
    nh = MLA_HEADS_PER_STEP
    kern = functools.partial(_mla_attn_kernel, tq=tq, tk=tk)
    return pl.pallas_call(
        kern,
        grid=(MLA_HEADS // nh, s // tq),
        in_specs=[pl.BlockSpec((tq, nh * MLA_HEAD_PAD), lambda h, i: (i, h)),
                  pl.BlockSpec((s, nh * MLA_HEAD_PAD), lambda h, i: (0, h)),
                  pl.BlockSpec((s, nh * MLA_V), lambda h, i: (0, h))],
        out_specs=pl.BlockSpec((tq, nh * MLA_V), lambda h, i: (i, h)),
        out_shape=jax.ShapeDtypeStruct((s, MLA_HEADS * MLA_V), BF16),
        scratch_shapes=[pltpu.VMEM((nh, 1, tq), F32), pltpu.VMEM((nh, 1, tq), F32),
                        pltpu.VMEM((nh, MLA_V, tq), F32),
                        pltpu.VMEM((nh, tk, tq), F32), pltpu.VMEM((nh, tk, tq), F32)],
        compiler_params=_cparams(("parallel", "arbitrary")),
        name="mla_attn",
    )(q, k, v)


def _diff_attn_kernel(q_ref, k_ref, v_ref, lq1_ref, lk1_ref, lq2_ref, lk2_ref, g_ref, o_ref,
                      m_sc, l_sc, acc_sc, sa_sc, sb_sc, *, tq, tk):
    chains = []
    for t in range(2):
        cols = slice(t * DIFF_HEAD_DIM, (t + 1) * DIFF_HEAD_DIM)
        chains.append(_Chain(
            q=q_ref[:, cols],
            k_blk=functools.partial(lambda k0, cols: k_ref[pl.ds(k0, tk), cols], cols=cols),
            v_blk=lambda k0: v_ref[pl.ds(k0, tk), :],
            m=m_sc.at[t], l=l_sc.at[t], acc=acc_sc.at[t], s_a=sa_sc.at[t], s_b=sb_sc.at[t]))
    _flash_pairs(pl.program_id(1), tq, tk, chains)

    lam = (jnp.exp(jnp.sum(lq1_ref[...] * lk1_ref[...], axis=-1, keepdims=True))
           - jnp.exp(jnp.sum(lq2_ref[...] * lk2_ref[...], axis=-1, keepdims=True))
           + LAMBDA_INIT)
    o = (acc_sc[0] / l_sc[0] - lam * (acc_sc[1] / l_sc[1])).T
    ms = jnp.mean(o * o, axis=-1, keepdims=True)
    o = o * lax.rsqrt(ms + EPS) * g_ref[...]
    o_ref[...] = (o * (1.0 - LAMBDA_INIT)).astype(o_ref.dtype)


def _diff_attn(qk, z, lq1, lk1, lq2, lk2, g, tq, tk):
    s = qk.shape[0]
    kern = functools.partial(_diff_attn_kernel, tq=tq, tk=tk)
    k_off = DIFF_QK_COLS // DIFF_V
    v_off = 2 * DIFF_QK_COLS // DIFF_V
    vec = pl.BlockSpec((1, DIFF_HEAD_DIM), lambda h, i: (0, 0))
    return pl.pallas_call(
        kern,
        grid=(DIFF_HEADS, s // tq),
        in_specs=[pl.BlockSpec((tq, DIFF_V), lambda h, i: (i, h)),
                  pl.BlockSpec((s, DIFF_V), lambda h, i: (0, h + k_off)),
                  pl.BlockSpec((s, DIFF_V), lambda h, i: (0, h + v_off)),
                  vec, vec, vec, vec,
                  pl.BlockSpec((1, DIFF_V), lambda h, i: (0, 0))],
        out_specs=pl.BlockSpec((tq, DIFF_V), lambda h, i: (i, h)),
        out_shape=jax.ShapeDtypeStruct((s, DIFF_HEADS * DIFF_V), BF16),
        scratch_shapes=[pltpu.VMEM((2, 1, tq), F32), pltpu.VMEM((2, 1, tq), F32),
                        pltpu.VMEM((2, DIFF_V, tq), F32),
                        pltpu.VMEM((2, tk, tq), F32), pltpu.VMEM((2, tk, tq), F32)],
        compiler_params=_cparams(("parallel", "arbitrary")),
        name="diff_attn",
    )(qk, qk, z, lq1, lk1, lq2, lk2, g)


def _wo_kernel(oa_ref, ob_ref, wa_ref, wb_ref, x_ref, h_ref):
    acc = jnp.dot(oa_ref[...], wa_ref[...], preferred_element_type=F32)
    acc = acc + jnp.dot(ob_ref[...], wb_ref[...], preferred_element_type=F32)
    h_ref[...] = x_ref[...] + acc


def _wo_proj(oa, ob, w, x, tm, tn):
    m, ka = oa.shape
    kb = ob.shape[1]
    assert ka == kb and w.shape[0] == ka + kb
    n = w.shape[1]
    return pl.pallas_call(
        _wo_kernel,
        grid=(m // tm, n // tn),
        in_specs=[pl.BlockSpec((tm, ka), lambda i, j: (i, 0)),
                  pl.BlockSpec((tm, kb), lambda i, j: (i, 0)),
                  pl.BlockSpec((ka, tn), lambda i, j: (0, j)),
                  pl.BlockSpec((kb, tn), lambda i, j: (1, j)),
                  pl.BlockSpec((tm, tn), lambda i, j: (i, j))],
        out_specs=pl.BlockSpec((tm, tn), lambda i, j: (i, j)),
        out_shape=jax.ShapeDtypeStruct((m, n), F32),
        compiler_params=_cparams(("parallel", "arbitrary")),
        name="wo_proj",
    )(oa, ob, w, w, x)


def _gate_up_kernel(a_ref, wg_ref, wu_ref, o_ref):
    a = a_ref[...]
    g = jnp.dot(a, wg_ref[...], preferred_element_type=F32)
    u = jnp.dot(a, wu_ref[...], preferred_element_type=F32)
    o_ref[...] = (g * (1.0 / (1.0 + jnp.exp(-g))) * u).astype(o_ref.dtype)


def _gate_up(a, wg, wu, tm, tn):
    m, k = a.shape
    n = wg.shape[1]
    return pl.pallas_call(
        _gate_up_kernel,
        grid=(m // tm, n // tn),
        in_specs=[pl.BlockSpec((tm, k), lambda i, j: (i, 0)),
                  pl.BlockSpec((k, tn), lambda i, j: (0, j)),
                  pl.BlockSpec((k, tn), lambda i, j: (0, j))],
        out_specs=pl.BlockSpec((tm, tn), lambda i, j: (i, j)),
        out_shape=jax.ShapeDtypeStruct((m, n), BF16),
        compiler_params=_cparams(("parallel", "arbitrary")),
        name="ffn_gate_up",
    )(a, wg, wu)


def _down_kernel(a_ref, w_ref, h_ref, o_ref, acc_sc):
    kk = pl.program_id(2)

    @pl.when(kk == 0)
    def _():
        acc_sc[...] = jnp.zeros(acc_sc.shape, F32)

    acc_sc[...] += jnp.dot(a_ref[...], w_ref[...], preferred_element_type=F32)

    @pl.when(kk == pl.num_programs(2) - 1)
    def _():
        o_ref[...] = h_ref[...] + acc_sc[...]


def _down_proj(a, w, h, tm, tn, tk):
    m, k = a.shape
    n = w.shape[1]
    return pl.pallas_call(
        _down_kernel,
        grid=(m // tm, n // tn, k // tk),
        in_specs=[pl.BlockSpec((tm, tk), lambda i, j, kk: (i, kk)),
                  pl.BlockSpec((tk, tn), lambda i, j, kk: (kk, j)),
                  pl.BlockSpec((tm, tn), lambda i, j, kk: (i, j))],
        out_specs=pl.BlockSpec((tm, tn), lambda i, j, kk: (i, j)),
        out_shape=jax.ShapeDtypeStruct((m, n), F32),
        scratch_shapes=[pltpu.VMEM((tm, tn), F32)],
        compiler_params=_cparams(("parallel", "arbitrary", "arbitrary")),
        name="ffn_down",
    )(a, w, h)


def _rope_tables(s):
    pos = jnp.arange(s).astype(F32)

    def cs(half):
        inv_freq = ROPE_THETA ** (-jnp.arange(half, dtype=F32) / half)
        ang = pos[:, None] * inv_freq[None, :]
        return jnp.cos(ang), jnp.sin(ang)

    c, sn = cs(MLA_ROPE // 2)
    z32 = jnp.zeros_like(c)
    z64 = jnp.zeros((s, 64), F32)
    mla_cos = jnp.concatenate([c, c, z64], axis=-1)
    mla_sa = jnp.concatenate([-sn, z32, z64], axis=-1)
    mla_sb = jnp.concatenate([z32, sn, z64], axis=-1)
    c, sn = cs(DIFF_HEAD_DIM // 2)
    diff_cos = jnp.concatenate([c, c], axis=-1)
    diff_sin = jnp.concatenate([-sn, sn], axis=-1)
    return mla_cos, mla_sa, mla_sb, diff_cos, diff_sin


def _pad_gain(g, lo, hi, width):
    return jnp.pad(g[lo:hi], (0, width - (hi - lo))).reshape(1, width)


def kernel(x, attn_norm_g, w_in, q_latent_norm_g, kv_latent_norm_g, w_uq, w_ukv, mla_q_norm_g, mla_k_norm_g, diff_q_norm_g, diff_k_norm_g, lambda_q1, lambda_k1, lambda_q2, lambda_k2, diff_subln_g, w_o, ffn_norm_g, w_gate, w_up, w_down):
    b, s, d = x.shape
    assert b == 1 and d == D_MODEL and s % 512 == 0
    x2 = x.reshape(s, d)
    tm = min(1024, s)
    tp = min(512, s)
    tq = min(1024, s)
    tk = tq // 2

    w_in0 = w_in[0]
    w_lat = jnp.pad(w_in0[:, :Z_LAT_USED].astype(BF16), ((0, 0), (0, Z_LAT - Z_LAT_USED)))
    w_diff = w_in0[:, Z_LAT_USED:].astype(BF16)
    w_uq_p = jnp.pad(w_uq[0].reshape(MLA_Q_LORA, MLA_HEADS, MLA_QK),
                     ((0, 0), (0, 0), (0, MLA_HEAD_PAD - MLA_QK))
                     ).reshape(MLA_Q_LORA, MLA_HEADS * MLA_HEAD_PAD).astype(BF16)
    w_ukv3 = w_ukv[0].reshape(MLA_KV_LORA, MLA_HEADS, MLA_NOPE + MLA_V)
    w_uk = w_ukv3[:, :, :MLA_NOPE].reshape(MLA_KV_LORA, MLA_HEADS * MLA_NOPE).astype(BF16)
    w_uv = w_ukv3[:, :, MLA_NOPE:].reshape(MLA_KV_LORA, MLA_HEADS * MLA_V).astype(BF16)
    w_o_b = _cast_bf16(w_o[0], 512, d)
    w_gate_b = _cast_bf16(w_gate[0], 512, D_FF // 2)
    w_up_b = _cast_bf16(w_up[0], 512, D_FF // 2)
    w_down_b = _cast_bf16(w_down[0], D_FF // 8, d // 2)

    mla_cos, mla_sa, mla_sb, diff_cos, diff_sin = _rope_tables(s)
    gq_a = _pad_gain(mla_q_norm_g[0], 0, MLA_NOPE, LANES)
    gq_b = _pad_gain(mla_q_norm_g[0], MLA_NOPE, MLA_QK, LANES)
    gk_a = _pad_gain(mla_k_norm_g[0], 0, MLA_NOPE, LANES)
    gk_b = _pad_gain(mla_k_norm_g[0], MLA_NOPE, MLA_QK, LANES)
    g_diff = jnp.concatenate([jnp.tile(diff_q_norm_g[0] * DIFF_QSCALE, 2 * DIFF_HEADS),
                              jnp.tile(diff_k_norm_g[0], 2 * DIFF_HEADS)]).reshape(1, -1)

    n = _rmsnorm(x2, attn_norm_g[0], min(256, s))
    z_lat = _matmul(n, w_lat, tm, Z_LAT // 2, BF16, "in_proj_lat")
    z_diff = _matmul(n, w_diff, tm, 1024, BF16, "in_proj_diff")
    q_a = _q_prep(z_lat, q_latent_norm_g[0].reshape(1, -1), w_uq_p, gq_a, gq_b,
                  mla_cos, mla_sa, mla_sb, tp)
    k_a, v_a = _kv_prep(z_lat, kv_latent_norm_g[0].reshape(1, -1), w_uk, w_uv, gk_a, gk_b,
                        mla_cos, mla_sa, mla_sb, tp)
    o_a = _mla_attn(q_a, k_a, v_a, tq, tk)
    qk_b = _diff_prep(z_diff, g_diff, diff_cos, diff_sin, tp)
    o_b = _diff_attn(qk_b, z_diff, lambda_q1[0].reshape(1, -1), lambda_k1[0].reshape(1, -1),
                     lambda_q2[0].reshape(1, -1), lambda_k2[0].reshape(1, -1),
                     diff_subln_g[0].reshape(1, -1), tq, tk)
    h = _wo_proj(o_a, o_b, w_o_b, x2, tm, 512)

    m = _rmsnorm(h, ffn_norm_g[0], min(256, s))
    a = _gate_up(m, w_gate_b, w_up_b, min(2048, s), 256)
    out = _down_proj(a, w_down_b, h, tm, 512, D_FF // 2)
    return out.reshape(b, s, d)
```

```python
import functools
import math
from typing import Any, NamedTuple

import jax
import jax.numpy as jnp
from jax import lax
from jax.experimental import pallas as pl
from jax.experimental.pallas import tpu as pltpu

F32 = jnp.float32
BF16 = jnp.bfloat16

D_MODEL = 4096
CHUNK = 64
ROPE_THETA = 10000.0
EPS = 1e-6
MLA_HEADS = 16
MLA_Q_LORA = 768
MLA_KV_LORA = 512
MLA_NOPE = 128
MLA_ROPE = 64
MLA_QK = MLA_NOPE + MLA_ROPE
MLA_V = 128
LOG2E = math.log2(math.e)
MLA_SCALE = 1.0 / math.sqrt(MLA_QK)
MLA_QSCALE = MLA_SCALE * LOG2E
DIFF_HEADS = 8
DIFF_HEAD_DIM = 128
DIFF_V = 2 * DIFF_HEAD_DIM
DIFF_SCALE = 1.0 / math.sqrt(DIFF_HEAD_DIM)
DIFF_QSCALE = DIFF_SCALE * LOG2E
DIFF_QK_COLS = 2 * DIFF_HEADS * DIFF_HEAD_DIM
DIFF_V_COLS = DIFF_HEADS * DIFF_V
D_FF = -(-8 * D_MODEL // (3 * 256)) * 256
LAMBDA_INIT = 0.8 - 0.6 * math.exp(-0.3 * 0)

LANES = 128
MLA_HEAD_PAD = 2 * LANES
Z_LAT = 1536
Z_KPE = MLA_Q_LORA + MLA_KV_LORA
Z_LAT_USED = Z_KPE + MLA_ROPE
NEG = -1e30
VMEM_LIMIT = 56 * 1024 * 1024


def _cparams(sem):
    return pltpu.CompilerParams(dimension_semantics=sem, vmem_limit_bytes=VMEM_LIMIT)


def _cast_kernel(w_ref, o_ref):
    o_ref[...] = w_ref[...].astype(o_ref.dtype)


def _cast_bf16(w, tr, tc):
    r, c = w.shape
    return pl.pallas_call(
        _cast_kernel,
        grid=(r // tr, c // tc),
        in_specs=[pl.BlockSpec((tr, tc), lambda i, j: (i, j))],
        out_specs=pl.BlockSpec((tr, tc), lambda i, j: (i, j)),
        out_shape=jax.ShapeDtypeStruct((r, c), BF16),
        compiler_params=_cparams(("parallel", "parallel")),
        name="cast_bf16",
    )(w)


def _rmsnorm_kernel(x_ref, g_ref, o_ref):
    x = x_ref[...]
    ms = jnp.mean(x * x, axis=-1, keepdims=True)
    o_ref[...] = (x * lax.rsqrt(ms + EPS) * g_ref[...]).astype(o_ref.dtype)


def _rmsnorm(x, g, tm):
    m, d = x.shape
    return pl.pallas_call(
        _rmsnorm_kernel,
        grid=(m // tm,),
        in_specs=[pl.BlockSpec((tm, d), lambda i: (i, 0)),
                  pl.BlockSpec((1, d), lambda i: (0, 0))],
        out_specs=pl.BlockSpec((tm, d), lambda i: (i, 0)),
        out_shape=jax.ShapeDtypeStruct((m, d), BF16),
        compiler_params=_cparams(("parallel",)),
        name="rmsnorm",
    )(x, g.reshape(1, d))


def _mm_kernel(a_ref, b_ref, o_ref):
    o_ref[...] = jnp.dot(a_ref[...], b_ref[...],
                         preferred_element_type=F32).astype(o_ref.dtype)


def _matmul(a, b, tm, tn, out_dtype, name):
    m, k = a.shape
    _, n = b.shape
    return pl.pallas_call(
        _mm_kernel,
        grid=(m // tm, n // tn),
        in_specs=[pl.BlockSpec((tm, k), lambda i, j: (i, 0)),
                  pl.BlockSpec((k, tn), lambda i, j: (0, j))],
        out_specs=pl.BlockSpec((tm, tn), lambda i, j: (i, j)),
        out_shape=jax.ShapeDtypeStruct((m, n), out_dtype),
        compiler_params=_cparams(("parallel", "arbitrary")),
        name=name,
    )(a, b)


def _latent_norm(c, g):
    ms = jnp.mean(c * c, axis=-1, keepdims=True)
    return (c * lax.rsqrt(ms + EPS) * g).astype(BF16)


def _mla_rope(y, cos, sa, sb):
    return y * cos + pltpu.roll(y, 96, 1) * sa + pltpu.roll(y, 32, 1) * sb


def _q_prep_kernel(z_ref, gl_ref, w_ref, ga_ref, gb_ref, cos_ref, sa_ref, sb_ref, o_ref):
    c = z_ref[:, :MLA_Q_LORA].astype(F32)
    lat = _latent_norm(c, gl_ref[...])
    q = jnp.dot(lat, w_ref[...], preferred_element_type=F32)
    cos, sa, sb = cos_ref[...], sa_ref[...], sb_ref[...]
    ga, gb = ga_ref[...], gb_ref[...]
    for h in range(MLA_HEADS):
        lo = h * MLA_HEAD_PAD
        qa = q[:, lo:lo + LANES]
        qb = q[:, lo + LANES:lo + 2 * LANES]
        ss = jnp.sum(qa * qa + qb * qb, axis=-1, keepdims=True)
        r = lax.rsqrt(ss * (1.0 / MLA_QK) + EPS)
        ya = qa * r * ga
        yb = _mla_rope(qb * r * gb, cos, sa, sb)
        o_ref[:, lo:lo + LANES] = (ya * MLA_QSCALE).astype(o_ref.dtype)
        o_ref[:, lo + LANES:lo + 2 * LANES] = (yb * MLA_QSCALE).astype(o_ref.dtype)


def _kv_prep_kernel(z_ref, gl_ref, wk_ref, wv_ref, ga_ref, gb_ref, cos_ref, sa_ref, sb_ref,
                    k_ref, v_ref):
    c = z_ref[:, MLA_Q_LORA:Z_KPE].astype(F32)
    lat = _latent_norm(c, gl_ref[...])
    kn = jnp.dot(lat, wk_ref[...], preferred_element_type=F32)
    v_ref[...] = jnp.dot(lat, wv_ref[...], preferred_element_type=F32).astype(v_ref.dtype)
    kpe = z_ref[:, Z_KPE:Z_KPE + LANES].astype(F32)
    ss_pe = jnp.sum(kpe * kpe, axis=-1, keepdims=True)
    pe_rot = _mla_rope(kpe * gb_ref[...], cos_ref[...], sa_ref[...], sb_ref[...])
    ga = ga_ref[...]
    for h in range(MLA_HEADS):
        a = kn[:, h * LANES:(h + 1) * LANES]
        ss = jnp.sum(a * a, axis=-1, keepdims=True) + ss_pe
        r = lax.rsqrt(ss * (1.0 / MLA_QK) + EPS)
        lo = h * MLA_HEAD_PAD
        k_ref[:, lo:lo + LANES] = (a * r * ga).astype(k_ref.dtype)
        k_ref[:, lo + LANES:lo + 2 * LANES] = (pe_rot * r).astype(k_ref.dtype)


def _row_spec(tm, n, col=0):
    return pl.BlockSpec((tm, n), lambda i: (i, col))


def _const_spec(shape):
    return pl.BlockSpec(shape, lambda i: (0, 0))


def _q_prep(z, gl, w, ga, gb, cos, sa, sb, tm):
    m = z.shape[0]
    n = MLA_HEADS * MLA_HEAD_PAD
    return pl.pallas_call(
        _q_prep_kernel,
        grid=(m // tm,),
        in_specs=[_row_spec(tm, Z_LAT), _const_spec(gl.shape), _const_spec(w.shape),
                  _const_spec(ga.shape), _const_spec(gb.shape),
                  _row_spec(tm, LANES), _row_spec(tm, LANES), _row_spec(tm, LANES)],
        out_specs=_row_spec(tm, n),
        out_shape=jax.ShapeDtypeStruct((m, n), BF16),
        compiler_params=_cparams(("parallel",)),
        name="mla_q_prep",
    )(z, gl, w, ga, gb, cos, sa, sb)


def _kv_prep(z, gl, wk, wv, ga, gb, cos, sa, sb, tm):
    m = z.shape[0]
    nk = MLA_HEADS * MLA_HEAD_PAD
    nv = MLA_HEADS * MLA_V
    return pl.pallas_call(
        _kv_prep_kernel,
        grid=(m // tm,),
        in_specs=[_row_spec(tm, Z_LAT), _const_spec(gl.shape), _const_spec(wk.shape),
                  _const_spec(wv.shape), _const_spec(ga.shape), _const_spec(gb.shape),
                  _row_spec(tm, LANES), _row_spec(tm, LANES), _row_spec(tm, LANES)],
        out_specs=[_row_spec(tm, nk), _row_spec(tm, nv)],
        out_shape=[jax.ShapeDtypeStruct((m, nk), BF16), jax.ShapeDtypeStruct((m, nv), BF16)],
        compiler_params=_cparams(("parallel",)),
        name="mla_kv_prep",
    )(z, gl, wk, wv, ga, gb, cos, sa, sb)


DIFF_PREP_COLS = 1024


def _diff_prep_kernel(x_ref, g_ref, cos_ref, sin_ref, o_ref):
    cos, sin = cos_ref[...], sin_ref[...]
    for c in range(DIFF_PREP_COLS // LANES):
        sl = slice(c * LANES, (c + 1) * LANES)
        x = x_ref[:, sl].astype(F32)
        ms = jnp.mean(x * x, axis=-1, keepdims=True)
        y = x * lax.rsqrt(ms + EPS) * g_ref[:, sl]
        o_ref[:, sl] = (y * cos + pltpu.roll(y, 64, 1) * sin).astype(o_ref.dtype)


def _diff_prep(z, g, cos, sin, tm):
    m = z.shape[0]
    n = 2 * DIFF_QK_COLS
    nb = n // DIFF_PREP_COLS
    return pl.pallas_call(
        _diff_prep_kernel,
        grid=(m // tm, nb),
        in_specs=[pl.BlockSpec((tm, DIFF_PREP_COLS), lambda i, j: (i, j)),
                  pl.BlockSpec((1, DIFF_PREP_COLS), lambda i, j: (0, j)),
                  pl.BlockSpec((tm, LANES), lambda i, j: (i, 0)),
                  pl.BlockSpec((tm, LANES), lambda i, j: (i, 0))],
        out_specs=pl.BlockSpec((tm, DIFF_PREP_COLS), lambda i, j: (i, j)),
        out_shape=jax.ShapeDtypeStruct((m, n), BF16),
        compiler_params=_cparams(("parallel", "arbitrary")),
        name="diff_qk_prep",
    )(z, g, cos, sin)


def _chunk_mask_t(q0, k0, tq, tk):
    key = (k0 + lax.broadcasted_iota(jnp.int32, (tk, tq), 0)) >> 6
    qry = (q0 + lax.broadcasted_iota(jnp.int32, (tk, tq), 1)) >> 6
    return key <= qry


def _scores_t(k, q):
    return lax.dot_general(k, q, (((1,), (1,)), ((), ())), preferred_element_type=F32)


ROW_CHUNK = 32


class _Buf(NamedTuple):
    s: Any
    bmax: Any
    p: Any

    def cols(self, sl):
        return _Buf(self.s.at[:, sl], self.bmax.at[:, sl], self.p.at[:, sl])


def _scores_into(buf, k, q, mask):
    s = _scores_t(k, q)
    if mask is not None:
        s = jnp.where(mask, s, NEG)
    buf.s[...] = s
    buf.bmax[...] = jnp.max(s, axis=0, keepdims=True)


def _remask(buf, mask):
    s = jnp.where(mask, buf.s[...], NEG)
    buf.s[...] = s
    buf.bmax[...] = jnp.max(s, axis=0, keepdims=True)


def _online_update_t(buf, v, m_sc, l_sc, acc_sc):
    tk, tq = buf.s.shape
    m_prev = m_sc[...]
    m_new = jnp.maximum(m_prev, buf.bmax[...])
    alpha = jnp.exp2(m_prev - m_new)
    p_t = jnp.exp2(buf.s[...] - m_new)
    l_sc[...] = alpha * l_sc[...] + jnp.sum(p_t, axis=0, keepdims=True)
    pv = lax.dot_general(v, p_t.astype(BF16), (((0,), (0,)), ((), ())),
                         preferred_element_type=F32)
    acc_sc[...] = alpha * acc_sc[...] + pv
    m_sc[...] = m_new


def _init_state(m_sc, l_sc, acc_sc):
    m_sc[...] = jnp.full(m_sc.shape, NEG, F32)
    l_sc[...] = jnp.zeros(l_sc.shape, F32)
    acc_sc[...] = jnp.zeros(acc_sc.shape, F32)


class _Chain(NamedTuple):
    q: Any
    k_blk: Any
    v_blk: Any
    m: Any
    l: Any
    acc: Any
    buf_a: Any
    buf_b: Any


def _flash_pairs(i, tq, tk, chains):
    assert tq == 2 * tk

    def k_start(j):
        return pl.multiple_of(j * tk, tk)

    def qk(use_b, j):
        k0 = k_start(j)
        for c in chains:
            _scores_into(c.buf_b if use_b else c.buf_a, c.k_blk(k0), c.q, None)

    def update(use_b, j):
        k0 = k_start(j)
        for c in chains:
            _online_update_t(c.buf_b if use_b else c.buf_a, c.v_blk(k0), c.m, c.l, c.acc)

    for c in chains:
        _init_state(c.m, c.l, c.acc)
    qk(False, 0)

    def body(p, carry):
        qk(True, 2 * p + 1)
        update(False, 2 * p)
        qk(False, 2 * p + 2)
        update(True, 2 * p + 1)
        return carry

    lax.fori_loop(0, i, body, 0)
    mask_a = _chunk_mask_t(i * tq, k_start(2 * i), tq, tk)
    half = pl.ds(tk, tk)
    k_last = k_start(2 * i + 1)
    mask_b = _chunk_mask_t(i * tq + tk, k_last, tk, tk)
    for c in chains:
        _remask(c.buf_a, mask_a)
        _scores_into(c.buf_b.cols(half), c.k_blk(k_last), c.q[tk:, :], mask_b)
    update(False, 2 * i)
    for c in chains:
        _online_update_t(c.buf_b.cols(half), c.v_blk(k_last),
                         c.m.at[:, half], c.l.at[:, half], c.acc.at[:, half])


def _score_scratch(n, tq, tk):
    one = [pltpu.VMEM((n, tk, tq), F32), pltpu.VMEM((n, 1, tq), F32), pltpu.VMEM((n, tk, tq), BF16)]
    return one + one


def _bufs(refs, idx):
    sa, ma, pa, sb, mb, pb = refs
    return (_Buf(sa.at[idx], ma.at[idx], pa.at[idx]), _Buf(sb.at[idx], mb.at[idx], pb.at[idx]))


MLA_HEADS_PER_STEP = 2


def _mla_attn_kernel(q_ref, k_ref, v_ref, o_ref, m_sc, l_sc, acc_sc, *score_refs, tq, tk):
    nh = m_sc.shape[0]
    chains = []
    for h in range(nh):
        qk_cols = slice(h * MLA_HEAD_PAD, (h + 1) * MLA_HEAD_PAD)
        v_cols = slice(h * MLA_V, (h + 1) * MLA_V)
        buf_a, buf_b = _bufs(score_refs, h)
        chains.append(_Chain(
            q=q_ref[:, qk_cols],
            k_blk=functools.partial(lambda k0, cols: k_ref[pl.ds(k0, tk), cols], cols=qk_cols),
            v_blk=functools.partial(lambda k0, cols: v_ref[pl.ds(k0, tk), cols], cols=v_cols),
            m=m_sc.at[h], l=l_sc.at[h], acc=acc_sc.at[h], buf_a=buf_a, buf_b=buf_b))
    _flash_pairs(pl.program_id(1), tq, tk, chains)
    for h in range(nh):
        o_ref[:, h * MLA_V:(h + 1) * MLA_V] = (acc_sc[h] / l_sc[h]).T.astype(o_ref.dtype)


def _mla_attn(q, k, v, tq, tk):
    s = q.shape[0]
    nh = MLA_HEADS_PER_STEP
    kern = functools.partial(_mla_attn_kernel, tq=tq, tk=tk)
    return pl.pallas_call(
        kern,
        grid=(MLA_HEADS // nh, s // tq),
        in_specs=[pl.BlockSpec((tq, nh * MLA_HEAD_PAD), lambda h, i: (i, h)),
                  pl.BlockSpec((s, nh * MLA_HEAD_PAD), lambda h, i: (0, h)),
                  pl.BlockSpec((s, nh * MLA_V), lambda h, i: (0, h))],
        out_specs=pl.BlockSpec((tq, nh * MLA_V), lambda h, i: (i, h)),
        out_shape=jax.ShapeDtypeStruct((s, MLA_HEADS * MLA_V), BF16),
        scratch_shapes=[pltpu.VMEM((nh, 1, tq), F32), pltpu.VMEM((nh, 1, tq), F32),
                        pltpu.VMEM((nh, MLA_V, tq), F32)] + _score_scratch(nh, tq, tk),
        compiler_params=_cparams(("parallel", "arbitrary")),
        name="mla_attn",
    )(q, k, v)


def _diff_attn_kernel(q_ref, k_ref, v_ref, lq1_ref, lk1_ref, lq2_ref, lk2_ref, g_ref, o_ref,
                      m_sc, l_sc, acc_sc, *score_refs, tq, tk):
    chains = []
    for t in range(2):
        cols = slice(t * DIFF_HEAD_DIM, (t + 1) * DIFF_HEAD_DIM)
        buf_a, buf_b = _bufs(score_refs, t)
        chains.append(_Chain(
            q=q_ref[:, cols],
            k_blk=functools.partial(lambda k0, cols: k_ref[pl.ds(k0, tk), cols], cols=cols),
            v_blk=lambda k0: v_ref[pl.ds(k0, tk), :],
            m=m_sc.at[t], l=l_sc.at[t], acc=acc_sc.at[t], buf_a=buf_a, buf_b=buf_b))
    _flash_pairs(pl.program_id(1), tq, tk, chains)

    lam = (jnp.exp(jnp.sum(lq1_ref[...] * lk1_ref[...], axis=-1, keepdims=True))
           - jnp.exp(jnp.sum(lq2_ref[...] * lk2_ref[...], axis=-1, keepdims=True))
           + LAMBDA_INIT)
    o = (acc_sc[0] / l_sc[0] - lam * (acc_sc[1] / l_sc[1])).T
    ms = jnp.mean(o * o, axis=-1, keepdims=True)
    o = o * lax.rsqrt(ms + EPS) * g_ref[...]
    o_ref[...] = (o * (1.0 - LAMBDA_INIT)).astype(o_ref.dtype)


def _diff_attn(qk, z, lq1, lk1, lq2, lk2, g, tq, tk):
    s = qk.shape[0]
    kern = functools.partial(_diff_attn_kernel, tq=tq, tk=tk)
    k_off = DIFF_QK_COLS // DIFF_V
    v_off = 2 * DIFF_QK_COLS // DIFF_V
    vec = pl.BlockSpec((1, DIFF_HEAD_DIM), lambda h, i: (0, 0))
    return pl.pallas_call(
        kern,
        grid=(DIFF_HEADS, s // tq),
        in_specs=[pl.BlockSpec((tq, DIFF_V), lambda h, i: (i, h)),
                  pl.BlockSpec((s, DIFF_V), lambda h, i: (0, h + k_off)),
                  pl.BlockSpec((s, DIFF_V), lambda h, i: (0, h + v_off)),
                  vec, vec, vec, vec,
                  pl.BlockSpec((1, DIFF_V), lambda h, i: (0, 0))],
        out_specs=pl.BlockSpec((tq, DIFF_V), lambda h, i: (i, h)),
        out_shape=jax.ShapeDtypeStruct((s, DIFF_HEADS * DIFF_V), BF16),
        scratch_shapes=[pltpu.VMEM((2, 1, tq), F32), pltpu.VMEM((2, 1, tq), F32),
                        pltpu.VMEM((2, DIFF_V, tq), F32)] + _score_scratch(2, tq, tk),
        compiler_params=_cparams(("parallel", "arbitrary")),
        name="diff_attn",
    )(qk, qk, z, lq1, lk1, lq2, lk2, g)


def _wo_kernel(oa_ref, ob_ref, wa_ref, wb_ref, x_ref, h_ref):
    acc = jnp.dot(oa_ref[...], wa_ref[...], preferred_element_type=F32)
    acc = acc + jnp.dot(ob_ref[...], wb_ref[...], preferred_element_type=F32)
    h_ref[...] = x_ref[...] + acc


def _wo_proj(oa, ob, w, x, tm, tn):
    m, ka = oa.shape
    kb = ob.shape[1]
    assert ka == kb and w.shape[0] == ka + kb
    n = w.shape[1]
    return pl.pallas_call(
        _wo_kernel,
        grid=(m // tm, n // tn),
        in_specs=[pl.BlockSpec((tm, ka), lambda i, j: (i, 0)),
                  pl.BlockSpec((tm, kb), lambda i, j: (i, 0)),
                  pl.BlockSpec((ka, tn), lambda i, j: (0, j)),
                  pl.BlockSpec((kb, tn), lambda i, j: (1, j)),
                  pl.BlockSpec((tm, tn), lambda i, j: (i, j))],
        out_specs=pl.BlockSpec((tm, tn), lambda i, j: (i, j)),
        out_shape=jax.ShapeDtypeStruct((m, n), F32),
        compiler_params=_cparams(("parallel", "arbitrary")),
        name="wo_proj",
    )(oa, ob, w, w, x)


def _gate_up_kernel(a_ref, wg_ref, wu_ref, o_ref):
    a = a_ref[...]
    g = jnp.dot(a, wg_ref[...].astype(BF16), preferred_element_type=F32)
    u = jnp.dot(a, wu_ref[...].astype(BF16), preferred_element_type=F32)
    o_ref[...] = (g * (1.0 / (1.0 + jnp.exp(-g))) * u).astype(o_ref.dtype)


def _gate_up(a, wg, wu, tm, tn):
    m, k = a.shape
    n = wg.shape[1]
    return pl.pallas_call(
        _gate_up_kernel,
        grid=(m // tm, n // tn),
        in_specs=[pl.BlockSpec((tm, k), lambda i, j: (i, 0), pipeline_mode=pl.Buffered(1)),
                  pl.BlockSpec((k, tn), lambda i, j: (0, j)),
                  pl.BlockSpec((k, tn), lambda i, j: (0, j))],
        out_specs=pl.BlockSpec((tm, tn), lambda i, j: (i, j)),
        out_shape=jax.ShapeDtypeStruct((m, n), BF16),
        compiler_params=_cparams(("parallel", "arbitrary")),
        name="ffn_gate_up",
    )(a, wg, wu)


def _down_kernel(a_ref, w_ref, h_ref, o_ref, acc_sc):
    kk = pl.program_id(2)

    @pl.when(kk == 0)
    def _():
        acc_sc[...] = jnp.zeros(acc_sc.shape, F32)

    acc_sc[...] += jnp.dot(a_ref[...], w_ref[...], preferred_element_type=F32)

    @pl.when(kk == pl.num_programs(2) - 1)
    def _():
        o_ref[...] = h_ref[...] + acc_sc[...]


def _down_proj(a, w, h, tm, tn, tk):
    m, k = a.shape
    n = w.shape[1]
    return pl.pallas_call(
        _down_kernel,
        grid=(m // tm, n // tn, k // tk),
        in_specs=[pl.BlockSpec((tm, tk), lambda i, j, kk: (i, kk)),
                  pl.BlockSpec((tk, tn), lambda i, j, kk: (kk, j)),
                  pl.BlockSpec((tm, tn), lambda i, j, kk: (i, j))],
        out_specs=pl.BlockSpec((tm, tn), lambda i, j, kk: (i, j)),
        out_shape=jax.ShapeDtypeStruct((m, n), F32),
        scratch_shapes=[pltpu.VMEM((tm, tn), F32)],
        compiler_params=_cparams(("parallel", "arbitrary", "arbitrary")),
        name="ffn_down",
    )(a, w, h)


def _rope_tables(s):
    pos = jnp.arange(s).astype(F32)

    def cs(half):
        inv_freq = ROPE_THETA ** (-jnp.arange(half, dtype=F32) / half)
        ang = pos[:, None] * inv_freq[None, :]
        return jnp.cos(ang), jnp.sin(ang)

    c, sn = cs(MLA_ROPE // 2)
    z32 = jnp.zeros_like(c)
    z64 = jnp.zeros((s, 64), F32)
    mla_cos = jnp.concatenate([c, c, z64], axis=-1)
    mla_sa = jnp.concatenate([-sn, z32, z64], axis=-1)
    mla_sb = jnp.concatenate([z32, sn, z64], axis=-1)
    c, sn = cs(DIFF_HEAD_DIM // 2)
    diff_cos = jnp.concatenate([c, c], axis=-1)
    diff_sin = jnp.concatenate([-sn, sn], axis=-1)
    return mla_cos, mla_sa, mla_sb, diff_cos, diff_sin


def _pad_gain(g, lo, hi, width):
    return jnp.pad(g[lo:hi], (0, width - (hi - lo))).reshape(1, width)


def kernel(x, attn_norm_g, w_in, q_latent_norm_g, kv_latent_norm_g, w_uq, w_ukv, mla_q_norm_g, mla_k_norm_g, diff_q_norm_g, diff_k_norm_g, lambda_q1, lambda_k1, lambda_q2, lambda_k2, diff_subln_g, w_o, ffn_norm_g, w_gate, w_up, w_down):
    b, s, d = x.shape
    assert b == 1 and d == D_MODEL and s % 512 == 0
    x2 = x.reshape(s, d)
    tm = min(1024, s)
    tp = min(512, s)
    tq = min(1024, s)
    tk = tq // 2

    w_in0 = w_in[0]
    w_lat = jnp.pad(w_in0[:, :Z_LAT_USED].astype(BF16), ((0, 0), (0, Z_LAT - Z_LAT_USED)))
    w_diff = w_in0[:, Z_LAT_USED:].astype(BF16)
    w_uq_p = jnp.pad(w_uq[0].reshape(MLA_Q_LORA, MLA_HEADS, MLA_QK),
                     ((0, 0), (0, 0), (0, MLA_HEAD_PAD - MLA_QK))
                     ).reshape(MLA_Q_LORA, MLA_HEADS * MLA_HEAD_PAD).astype(BF16)
    w_ukv3 = w_ukv[0].reshape(MLA_KV_LORA, MLA_HEADS, MLA_NOPE + MLA_V)
    w_uk = w_ukv3[:, :, :MLA_NOPE].reshape(MLA_KV_LORA, MLA_HEADS * MLA_NOPE).astype(BF16)
    w_uv = w_ukv3[:, :, MLA_NOPE:].reshape(MLA_KV_LORA, MLA_HEADS * MLA_V).astype(BF16)
    w_o_b = _cast_bf16(w_o[0], 512, d)
    w_down_b = _cast_bf16(w_down[0], D_FF // 8, d // 2)

    mla_cos, mla_sa, mla_sb, diff_cos, diff_sin = _rope_tables(s)
    gq_a = _pad_gain(mla_q_norm_g[0], 0, MLA_NOPE, LANES)
    gq_b = _pad_gain(mla_q_norm_g[0], MLA_NOPE, MLA_QK, LANES)
    gk_a = _pad_gain(mla_k_norm_g[0], 0, MLA_NOPE, LANES)
    gk_b = _pad_gain(mla_k_norm_g[0], MLA_NOPE, MLA_QK, LANES)
    g_diff = jnp.concatenate([jnp.tile(diff_q_norm_g[0] * DIFF_QSCALE, 2 * DIFF_HEADS),
                              jnp.tile(diff_k_norm_g[0], 2 * DIFF_HEADS)]).reshape(1, -1)

    n = _rmsnorm(x2, attn_norm_g[0], min(256, s))
    z_lat = _matmul(n, w_lat, tm, Z_LAT // 2, BF16, "in_proj_lat")
    z_diff = _matmul(n, w_diff, tm, 1024, BF16, "in_proj_diff")
    q_a = _q_prep(z_lat, q_latent_norm_g[0].reshape(1, -1), w_uq_p, gq_a, gq_b,
                  mla_cos, mla_sa, mla_sb, tp)
    k_a, v_a = _kv_prep(z_lat, kv_latent_norm_g[0].reshape(1, -1), w_uk, w_uv, gk_a, gk_b,
                        mla_cos, mla_sa, mla_sb, tp)
    o_a = _mla_attn(q_a, k_a, v_a, tq, tk)
    qk_b = _diff_prep(z_diff, g_diff, diff_cos, diff_sin, tm)
    o_b = _diff_attn(qk_b, z_diff, lambda_q1[0].reshape(1, -1), lambda_k1[0].reshape(1, -1),
                     lambda_q2[0].reshape(1, -1), lambda_k2[0].reshape(1, -1),
                     diff_subln_g[0].reshape(1, -1), tq, tk)
    h = _wo_proj(o_a, o_b, w_o_b, x2, tm, 512)

    m = _rmsnorm(h, ffn_norm_g[0], min(256, s))
    a = _gate_up(m, w_gate[0], w_up[0], min(2048, s), 256)
    out = _down_proj(a, w_down_b, h, tm, 512, D_FF // 2)
    return out.reshape(b, s, d)
```

```python
import functools
import math
from typing import Any, NamedTuple

import jax
import jax.numpy as jnp
import numpy as np
from jax import lax
from jax.experimental import pallas as pl
from jax.experimental.pallas import tpu as pltpu

F32 = jnp.float32
BF16 = jnp.bfloat16

D_MODEL = 4096
CHUNK = 64
ROPE_THETA = 10000.0
EPS = 1e-6
MLA_HEADS = 16
MLA_Q_LORA = 768
MLA_KV_LORA = 512
MLA_NOPE = 128
MLA_ROPE = 64
MLA_QK = MLA_NOPE + MLA_ROPE
MLA_V = 128
LOG2E = math.log2(math.e)
MLA_SCALE = 1.0 / math.sqrt(MLA_QK)
MLA_QSCALE = MLA_SCALE * LOG2E
DIFF_HEADS = 8
DIFF_HEAD_DIM = 128
DIFF_V = 2 * DIFF_HEAD_DIM
DIFF_SCALE = 1.0 / math.sqrt(DIFF_HEAD_DIM)
DIFF_QSCALE = DIFF_SCALE * LOG2E
DIFF_QK_COLS = 2 * DIFF_HEADS * DIFF_HEAD_DIM
DIFF_V_COLS = DIFF_HEADS * DIFF_V
D_FF = -(-8 * D_MODEL // (3 * 256)) * 256
LAMBDA_INIT = 0.8 - 0.6 * math.exp(-0.3 * 0)

LANES = 128
MLA_HEAD_PAD = 2 * LANES
Z_LAT = 1536
Z_KPE = MLA_Q_LORA + MLA_KV_LORA
Z_LAT_USED = Z_KPE + MLA_ROPE
NEG = -1e30
VMEM_LIMIT = 56 * 1024 * 1024


def _cparams(sem):
    return pltpu.CompilerParams(dimension_semantics=sem, vmem_limit_bytes=VMEM_LIMIT)


def _cast_kernel(w_ref, o_ref):
    o_ref[...] = w_ref[...].astype(o_ref.dtype)


def _cast_bf16(w, tr, tc):
    r, c = w.shape
    return pl.pallas_call(
        _cast_kernel,
        grid=(r // tr, c // tc),
        in_specs=[pl.BlockSpec((tr, tc), lambda i, j: (i, j))],
        out_specs=pl.BlockSpec((tr, tc), lambda i, j: (i, j)),
        out_shape=jax.ShapeDtypeStruct((r, c), BF16),
        compiler_params=_cparams(("parallel", "parallel")),
        name="cast_bf16",
    )(w)


def _rmsnorm_kernel(x_ref, g_ref, o_ref):
    x = x_ref[...]
    ms = jnp.mean(x * x, axis=-1, keepdims=True)
    o_ref[...] = (x * lax.rsqrt(ms + EPS) * g_ref[...]).astype(o_ref.dtype)


def _rmsnorm(x, g, tm):
    m, d = x.shape
    return pl.pallas_call(
        _rmsnorm_kernel,
        grid=(m // tm,),
        in_specs=[pl.BlockSpec((tm, d), lambda i: (i, 0)),
                  pl.BlockSpec((1, d), lambda i: (0, 0))],
        out_specs=pl.BlockSpec((tm, d), lambda i: (i, 0)),
        out_shape=jax.ShapeDtypeStruct((m, d), BF16),
        compiler_params=_cparams(("parallel",)),
        name="rmsnorm",
    )(x, g.reshape(1, d))


def _mm_kernel(a_ref, b_ref, o_ref):
    o_ref[...] = jnp.dot(a_ref[...], b_ref[...],
                         preferred_element_type=F32).astype(o_ref.dtype)


def _matmul(a, b, tm, tn, out_dtype, name):
    m, k = a.shape
    _, n = b.shape
    return pl.pallas_call(
        _mm_kernel,
        grid=(m // tm, n // tn),
        in_specs=[pl.BlockSpec((tm, k), lambda i, j: (i, 0)),
                  pl.BlockSpec((k, tn), lambda i, j: (0, j))],
        out_specs=pl.BlockSpec((tm, tn), lambda i, j: (i, j)),
        out_shape=jax.ShapeDtypeStruct((m, n), out_dtype),
        compiler_params=_cparams(("parallel", "arbitrary")),
        name=name,
    )(a, b)


def _latent_norm(c, g):
    ms = jnp.mean(c * c, axis=-1, keepdims=True)
    return (c * lax.rsqrt(ms + EPS) * g).astype(BF16)


def _mla_rope(y, cos, sa, sb):
    return y * cos + pltpu.roll(y, 96, 1) * sa + pltpu.roll(y, 32, 1) * sb


def _q_prep_kernel(z_ref, gl_ref, w_ref, ga_ref, gb_ref, cos_ref, sa_ref, sb_ref, o_ref):
    c = z_ref[:, :MLA_Q_LORA].astype(F32)
    lat = _latent_norm(c, gl_ref[...])
    q = jnp.dot(lat, w_ref[...], preferred_element_type=F32)
    cos, sa, sb = cos_ref[...], sa_ref[...], sb_ref[...]
    ga, gb = ga_ref[...], gb_ref[...]
    for h in range(MLA_HEADS):
        lo = h * MLA_HEAD_PAD
        qa = q[:, lo:lo + LANES]
        qb = q[:, lo + LANES:lo + 2 * LANES]
        ss = jnp.sum(qa * qa + qb * qb, axis=-1, keepdims=True)
        r = lax.rsqrt(ss * (1.0 / MLA_QK) + EPS)
        ya = qa * r * ga
        yb = _mla_rope(qb * r * gb, cos, sa, sb)
        o_ref[:, lo:lo + LANES] = (ya * MLA_QSCALE).astype(o_ref.dtype)
        o_ref[:, lo + LANES:lo + 2 * LANES] = (yb * MLA_QSCALE).astype(o_ref.dtype)


def _kv_prep_kernel(z_ref, gl_ref, wk_ref, wv_ref, ga_ref, gb_ref, cos_ref, sa_ref, sb_ref,
                    k_ref, v_ref):
    c = z_ref[:, MLA_Q_LORA:Z_KPE].astype(F32)
    lat = _latent_norm(c, gl_ref[...])
    kn = jnp.dot(lat, wk_ref[...], preferred_element_type=F32)
    v_ref[...] = jnp.dot(lat, wv_ref[...], preferred_element_type=F32).astype(v_ref.dtype)
    kpe = z_ref[:, Z_KPE:Z_KPE + LANES].astype(F32)
    ss_pe = jnp.sum(kpe * kpe, axis=-1, keepdims=True)
    pe_rot = _mla_rope(kpe * gb_ref[...], cos_ref[...], sa_ref[...], sb_ref[...])
    ga = ga_ref[...]
    for h in range(MLA_HEADS):
        a = kn[:, h * LANES:(h + 1) * LANES]
        ss = jnp.sum(a * a, axis=-1, keepdims=True) + ss_pe
        r = lax.rsqrt(ss * (1.0 / MLA_QK) + EPS)
        lo = h * MLA_HEAD_PAD
        k_ref[:, lo:lo + LANES] = (a * r * ga).astype(k_ref.dtype)
        k_ref[:, lo + LANES:lo + 2 * LANES] = (pe_rot * r).astype(k_ref.dtype)


def _row_spec(tm, n, col=0):
    return pl.BlockSpec((tm, n), lambda i: (i, col))


def _const_spec(shape):
    return pl.BlockSpec(shape, lambda i: (0, 0))


def _q_prep(z, gl, w, ga, gb, cos, sa, sb, tm):
    m = z.shape[0]
    n = MLA_HEADS * MLA_HEAD_PAD
    return pl.pallas_call(
        _q_prep_kernel,
        grid=(m // tm,),
        in_specs=[_row_spec(tm, Z_LAT), _const_spec(gl.shape), _const_spec(w.shape),
                  _const_spec(ga.shape), _const_spec(gb.shape),
                  _row_spec(tm, LANES), _row_spec(tm, LANES), _row_spec(tm, LANES)],
        out_specs=_row_spec(tm, n),
        out_shape=jax.ShapeDtypeStruct((m, n), BF16),
        compiler_params=_cparams(("parallel",)),
        name="mla_q_prep",
    )(z, gl, w, ga, gb, cos, sa, sb)


def _kv_prep(z, gl, wk, wv, ga, gb, cos, sa, sb, tm):
    m = z.shape[0]
    nk = MLA_HEADS * MLA_HEAD_PAD
    nv = MLA_HEADS * MLA_V
    return pl.pallas_call(
        _kv_prep_kernel,
        grid=(m // tm,),
        in_specs=[_row_spec(tm, Z_LAT), _const_spec(gl.shape), _const_spec(wk.shape),
                  _const_spec(wv.shape), _const_spec(ga.shape), _const_spec(gb.shape),
                  _row_spec(tm, LANES), _row_spec(tm, LANES), _row_spec(tm, LANES)],
        out_specs=[_row_spec(tm, nk), _row_spec(tm, nv)],
        out_shape=[jax.ShapeDtypeStruct((m, nk), BF16), jax.ShapeDtypeStruct((m, nv), BF16)],
        compiler_params=_cparams(("parallel",)),
        name="mla_kv_prep",
    )(z, gl, wk, wv, ga, gb, cos, sa, sb)


DIFF_PREP_COLS = 1024


def _diff_prep_kernel(x_ref, g_ref, cos_ref, sin_ref, o_ref):
    cos, sin = cos_ref[...], sin_ref[...]
    for c in range(DIFF_PREP_COLS // LANES):
        sl = slice(c * LANES, (c + 1) * LANES)
        x = x_ref[:, sl].astype(F32)
        ms = jnp.mean(x * x, axis=-1, keepdims=True)
        y = x * lax.rsqrt(ms + EPS) * g_ref[:, sl]
        o_ref[:, sl] = (y * cos + pltpu.roll(y, 64, 1) * sin).astype(o_ref.dtype)


def _diff_prep(z, g, cos, sin, tm):
    m = z.shape[0]
    n = 2 * DIFF_QK_COLS
    nb = n // DIFF_PREP_COLS
    return pl.pallas_call(
        _diff_prep_kernel,
        grid=(m // tm, nb),
        in_specs=[pl.BlockSpec((tm, DIFF_PREP_COLS), lambda i, j: (i, j)),
                  pl.BlockSpec((1, DIFF_PREP_COLS), lambda i, j: (0, j)),
                  pl.BlockSpec((tm, LANES), lambda i, j: (i, 0)),
                  pl.BlockSpec((tm, LANES), lambda i, j: (i, 0))],
        out_specs=pl.BlockSpec((tm, DIFF_PREP_COLS), lambda i, j: (i, j)),
        out_shape=jax.ShapeDtypeStruct((m, n), BF16),
        compiler_params=_cparams(("parallel", "arbitrary")),
        name="diff_qk_prep",
    )(z, g, cos, sin)


def _scores_t(k, q):
    return lax.dot_general(k, q, (((1,), (1,)), ((), ())), preferred_element_type=F32)


class _Buf(NamedTuple):
    s: Any
    bmax: Any

    def cols(self, sl):
        return _Buf(self.s.at[:, sl], self.bmax.at[:, sl])


def _scores_into(buf, k, q):
    s = _scores_t(k, q)
    buf.s[...] = s
    buf.bmax[...] = jnp.max(s, axis=0, keepdims=True)


def _mask_diag_block(buf, c0, tk):
    upper = lax.broadcasted_iota(jnp.int32, (CHUNK, LANES), 1) >= CHUNK
    for g in range(tk // LANES):
        cols = pl.ds(c0 + g * LANES, LANES)
        r0 = g * LANES + CHUNK
        edge = jnp.where(upper, buf.s[r0:r0 + CHUNK, cols], NEG)
        buf.s[r0:r0 + CHUNK, cols] = edge
        if r0 + CHUNK < tk:
            buf.s[r0 + CHUNK:tk, cols] = jnp.full((tk - r0 - CHUNK, LANES), NEG, F32)
        buf.bmax[:, cols] = jnp.maximum(jnp.max(buf.s[0:r0, cols], axis=0, keepdims=True),
                                        jnp.max(edge, axis=0, keepdims=True))


def _online_update_t(buf, v, m_sc, l_sc, acc_sc):
    tk, tq = buf.s.shape
    m_prev = m_sc[...]
    m_new = jnp.maximum(m_prev, buf.bmax[...])
    alpha = jnp.exp2(m_prev - m_new)
    p_t = jnp.exp2(buf.s[...] - m_new)
    l_sc[...] = alpha * l_sc[...] + jnp.sum(p_t, axis=0, keepdims=True)
    pv = lax.dot_general(v, p_t.astype(BF16), (((0,), (0,)), ((), ())),
                         preferred_element_type=F32)
    acc_sc[...] = alpha * acc_sc[...] + pv
    m_sc[...] = m_new


def _init_state(m_sc, l_sc, acc_sc):
    m_sc[...] = jnp.full(m_sc.shape, NEG, F32)
    l_sc[...] = jnp.zeros(l_sc.shape, F32)
    acc_sc[...] = jnp.zeros(acc_sc.shape, F32)


class _Chain(NamedTuple):
    q: Any
    k_blk: Any
    v_blk: Any
    m: Any
    l: Any
    acc: Any
    buf_a: Any
    buf_b: Any


def _flash_pairs(i, tq, tk, chains):
    assert tq == 2 * tk

    def k_start(j):
        return pl.multiple_of(j * tk, tk)

    def qk(use_b, j):
        k0 = k_start(j)
        for c in chains:
            _scores_into(c.buf_b if use_b else c.buf_a, c.k_blk(k0), c.q)

    def update(use_b, j):
        k0 = k_start(j)
        for c in chains:
            _online_update_t(c.buf_b if use_b else c.buf_a, c.v_blk(k0), c.m, c.l, c.acc)

    for c in chains:
        _init_state(c.m, c.l, c.acc)
    qk(False, 0)

    def body(p, carry):
        qk(True, 2 * p + 1)
        update(False, 2 * p)
        qk(False, 2 * p + 2)
        update(True, 2 * p + 1)
        return carry

    lax.fori_loop(0, i, body, 0)
    half = pl.ds(tk, tk)
    k_last = k_start(2 * i + 1)
    for c in chains:
        _mask_diag_block(c.buf_a, 0, tk)
        _scores_into(c.buf_b.cols(half), c.k_blk(k_last), c.q[tk:, :])
        _mask_diag_block(c.buf_b, tk, tk)
    update(False, 2 * i)
    for c in chains:
        _online_update_t(c.buf_b.cols(half), c.v_blk(k_last),
                         c.m.at[:, half], c.l.at[:, half], c.acc.at[:, half])


def _score_scratch(n, tq, tk):
    one = [pltpu.VMEM((n, tk, tq), F32), pltpu.VMEM((n, 1, tq), F32)]
    return one + one


def _bufs(refs, idx):
    sa, ma, sb, mb = refs
    return _Buf(sa.at[idx], ma.at[idx]), _Buf(sb.at[idx], mb.at[idx])


MLA_HEADS_PER_STEP = 2


def _mla_attn_kernel(q_ref, k_ref, v_ref, o_ref, m_sc, l_sc, acc_sc, *score_refs, tq, tk):
    nh = m_sc.shape[0]
    chains = []
    for h in range(nh):
        qk_cols = slice(h * MLA_HEAD_PAD, (h + 1) * MLA_HEAD_PAD)
        v_cols = slice(h * MLA_V, (h + 1) * MLA_V)
        buf_a, buf_b = _bufs(score_refs, h)
        chains.append(_Chain(
            q=q_ref[:, qk_cols],
            k_blk=functools.partial(lambda k0, cols: k_ref[pl.ds(k0, tk), cols], cols=qk_cols),
            v_blk=functools.partial(lambda k0, cols: v_ref[pl.ds(k0, tk), cols], cols=v_cols),
            m=m_sc.at[h], l=l_sc.at[h], acc=acc_sc.at[h], buf_a=buf_a, buf_b=buf_b))
    _flash_pairs(pl.program_id(1), tq, tk, chains)
    for h in range(nh):
        o_ref[:, h * MLA_V:(h + 1) * MLA_V] = (acc_sc[h] / l_sc[h]).T.astype(o_ref.dtype)


def _mla_attn(q, k, v, tq, tk):
    s = q.shape[0]
    nh = MLA_HEADS_PER_STEP
    kern = functools.partial(_mla_attn_kernel, tq=tq, tk=tk)
    return pl.pallas_call(
        kern,
        grid=(MLA_HEADS // nh, s // tq),
        in_specs=[pl.BlockSpec((tq, nh * MLA_HEAD_PAD), lambda h, i: (i, h)),
                  pl.BlockSpec((s, nh * MLA_HEAD_PAD), lambda h, i: (0, h)),
                  pl.BlockSpec((s, nh * MLA_V), lambda h, i: (0, h))],
        out_specs=pl.BlockSpec((tq, nh * MLA_V), lambda h, i: (i, h)),
        out_shape=jax.ShapeDtypeStruct((s, MLA_HEADS * MLA_V), BF16),
        scratch_shapes=[pltpu.VMEM((nh, 1, tq), F32), pltpu.VMEM((nh, 1, tq), F32),
                        pltpu.VMEM((nh, MLA_V, tq), F32)] + _score_scratch(nh, tq, tk),
        compiler_params=_cparams(("parallel", "arbitrary")),
        name="mla_attn",
    )(q, k, v)


def _diff_attn_kernel(q_ref, k_ref, v_ref, lq1_ref, lk1_ref, lq2_ref, lk2_ref, g_ref, o_ref,
                      m_sc, l_sc, acc_sc, *score_refs, tq, tk):
    chains = []
    for t in range(2):
        cols = slice(t * DIFF_HEAD_DIM, (t + 1) * DIFF_HEAD_DIM)
        buf_a, buf_b = _bufs(score_refs, t)
        chains.append(_Chain(
            q=q_ref[:, cols],
            k_blk=functools.partial(lambda k0, cols: k_ref[pl.ds(k0, tk), cols], cols=cols),
            v_blk=lambda k0: v_ref[pl.ds(k0, tk), :],
            m=m_sc.at[t], l=l_sc.at[t], acc=acc_sc.at[t], buf_a=buf_a, buf_b=buf_b))
    _flash_pairs(pl.program_id(1), tq, tk, chains)

    lam = (jnp.exp(jnp.sum(lq1_ref[...] * lk1_ref[...], axis=-1, keepdims=True))
           - jnp.exp(jnp.sum(lq2_ref[...] * lk2_ref[...], axis=-1, keepdims=True))
           + LAMBDA_INIT)
    o = (acc_sc[0] / l_sc[0] - lam * (acc_sc[1] / l_sc[1])).T
    ms = jnp.mean(o * o, axis=-1, keepdims=True)
    o = o * lax.rsqrt(ms + EPS) * g_ref[...]
    o_ref[...] = (o * (1.0 - LAMBDA_INIT)).astype(o_ref.dtype)


def _diff_attn(qk, z, lq1, lk1, lq2, lk2, g, tq, tk):
    s = qk.shape[0]
    kern = functools.partial(_diff_attn_kernel, tq=tq, tk=tk)
    k_off = DIFF_QK_COLS // DIFF_V
    v_off = 2 * DIFF_QK_COLS // DIFF_V
    vec = pl.BlockSpec((1, DIFF_HEAD_DIM), lambda h, i: (0, 0))
    return pl.pallas_call(
        kern,
        grid=(DIFF_HEADS, s // tq),
        in_specs=[pl.BlockSpec((tq, DIFF_V), lambda h, i: (i, h)),
                  pl.BlockSpec((s, DIFF_V), lambda h, i: (0, h + k_off)),
                  pl.BlockSpec((s, DIFF_V), lambda h, i: (0, h + v_off)),
                  vec, vec, vec, vec,
                  pl.BlockSpec((1, DIFF_V), lambda h, i: (0, 0))],
        out_specs=pl.BlockSpec((tq, DIFF_V), lambda h, i: (i, h)),
        out_shape=jax.ShapeDtypeStruct((s, DIFF_HEADS * DIFF_V), BF16),
        scratch_shapes=[pltpu.VMEM((2, 1, tq), F32), pltpu.VMEM((2, 1, tq), F32),
                        pltpu.VMEM((2, DIFF_V, tq), F32)] + _score_scratch(2, tq, tk),
        compiler_params=_cparams(("parallel", "arbitrary")),
        name="diff_attn",
    )(qk, qk, z, lq1, lk1, lq2, lk2, g)


def _wo_kernel(oa_ref, ob_ref, wa_ref, wb_ref, x_ref, h_ref):
    acc = jnp.dot(oa_ref[...], wa_ref[...].astype(BF16), preferred_element_type=F32)
    acc = acc + jnp.dot(ob_ref[...], wb_ref[...].astype(BF16), preferred_element_type=F32)
    h_ref[...] = x_ref[...] + acc


def _wo_proj(oa, ob, w, x, tm, tn):
    m, ka = oa.shape
    kb = ob.shape[1]
    assert ka == kb and w.shape[0] == ka + kb
    n = w.shape[1]
    return pl.pallas_call(
        _wo_kernel,
        grid=(m // tm, n // tn),
        in_specs=[pl.BlockSpec((tm, ka), lambda i, j: (i, 0)),
                  pl.BlockSpec((tm, kb), lambda i, j: (i, 0)),
                  pl.BlockSpec((ka, tn), lambda i, j: (0, j)),
                  pl.BlockSpec((kb, tn), lambda i, j: (1, j)),
                  pl.BlockSpec((tm, tn), lambda i, j: (i, j))],
        out_specs=pl.BlockSpec((tm, tn), lambda i, j: (i, j)),
        out_shape=jax.ShapeDtypeStruct((m, n), F32),
        compiler_params=_cparams(("parallel", "arbitrary")),
        name="wo_proj",
    )(oa, ob, w, w, x)


def _gate_up_kernel(a_ref, wg_ref, wu_ref, o_ref):
    a = a_ref[...]
    g = jnp.dot(a, wg_ref[...].astype(BF16), preferred_element_type=F32)
    u = jnp.dot(a, wu_ref[...].astype(BF16), preferred_element_type=F32)
    o_ref[...] = (g * (1.0 / (1.0 + jnp.exp(-g))) * u).astype(o_ref.dtype)


def _gate_up(a, wg, wu, tm, tn):
    m, k = a.shape
    n = wg.shape[1]
    return pl.pallas_call(
        _gate_up_kernel,
        grid=(m // tm, n // tn),
        in_specs=[pl.BlockSpec((tm, k), lambda i, j: (i, 0), pipeline_mode=pl.Buffered(1)),
                  pl.BlockSpec((k, tn), lambda i, j: (0, j)),
                  pl.BlockSpec((k, tn), lambda i, j: (0, j))],
        out_specs=pl.BlockSpec((tm, tn), lambda i, j: (i, j)),
        out_shape=jax.ShapeDtypeStruct((m, n), BF16),
        compiler_params=_cparams(("parallel", "arbitrary")),
        name="ffn_gate_up",
    )(a, wg, wu)


def _down_kernel(a_ref, w_ref, h_ref, o_ref):
    o_ref[...] = h_ref[...] + jnp.dot(a_ref[...], w_ref[...], preferred_element_type=F32)


def _down_proj(a, w, h, tm, tn):
    m, k = a.shape
    n = w.shape[1]
    return pl.pallas_call(
        _down_kernel,
        grid=(m // tm, n // tn),
        in_specs=[pl.BlockSpec((tm, k), lambda i, j: (i, 0)),
                  pl.BlockSpec((k, tn), lambda i, j: (0, j)),
                  pl.BlockSpec((tm, tn), lambda i, j: (i, j))],
        out_specs=pl.BlockSpec((tm, tn), lambda i, j: (i, j)),
        out_shape=jax.ShapeDtypeStruct((m, n), F32),
        compiler_params=_cparams(("parallel", "arbitrary")),
        name="ffn_down",
    )(a, w, h)


def _rope_tables(s):
    pos = np.arange(s, dtype=np.float64)

    def cs(half):
        inv_freq = ROPE_THETA ** (-np.arange(half, dtype=np.float64) / half)
        ang = pos[:, None] * inv_freq[None, :]
        return np.cos(ang), np.sin(ang)

    c, sn = cs(MLA_ROPE // 2)
    z32 = np.zeros_like(c)
    z64 = np.zeros((s, 64))
    mla_cos = np.concatenate([c, c, z64], axis=-1)
    mla_sa = np.concatenate([-sn, z32, z64], axis=-1)
    mla_sb = np.concatenate([z32, sn, z64], axis=-1)
    c, sn = cs(DIFF_HEAD_DIM // 2)
    diff_cos = np.concatenate([c, c], axis=-1)
    diff_sin = np.concatenate([-sn, sn], axis=-1)
    return tuple(jnp.asarray(t.astype(np.float32))
                 for t in (mla_cos, mla_sa, mla_sb, diff_cos, diff_sin))


def _pad_gain(g, lo, hi, width):
    return jnp.pad(g[lo:hi], (0, width - (hi - lo))).reshape(1, width)


def kernel(x, attn_norm_g, w_in, q_latent_norm_g, kv_latent_norm_g, w_uq, w_ukv, mla_q_norm_g, mla_k_norm_g, diff_q_norm_g, diff_k_norm_g, lambda_q1, lambda_k1, lambda_q2, lambda_k2, diff_subln_g, w_o, ffn_norm_g, w_gate, w_up, w_down):
    b, s, d = x.shape
    assert b == 1 and d == D_MODEL and s % 512 == 0
    x2 = x.reshape(s, d)
    tm = min(1024, s)
    tp = min(512, s)
    tq = min(1024, s)
    tk = tq // 2

    w_in0 = w_in[0]
    w_lat = jnp.pad(w_in0[:, :Z_LAT_USED].astype(BF16), ((0, 0), (0, Z_LAT - Z_LAT_USED)))
    w_diff = w_in0[:, Z_LAT_USED:].astype(BF16)
    w_uq_p = jnp.pad(w_uq[0].reshape(MLA_Q_LORA, MLA_HEADS, MLA_QK),
                     ((0, 0), (0, 0), (0, MLA_HEAD_PAD - MLA_QK))
                     ).reshape(MLA_Q_LORA, MLA_HEADS * MLA_HEAD_PAD).astype(BF16)
    w_ukv3 = w_ukv[0].reshape(MLA_KV_LORA, MLA_HEADS, MLA_NOPE + MLA_V)
    w_uk = w_ukv3[:, :, :MLA_NOPE].reshape(MLA_KV_LORA, MLA_HEADS * MLA_NOPE).astype(BF16)
    w_uv = w_ukv3[:, :, MLA_NOPE:].reshape(MLA_KV_LORA, MLA_HEADS * MLA_V).astype(BF16)
    w_down_b = _cast_bf16(w_down[0], D_FF // 8, d // 2)

    mla_cos, mla_sa, mla_sb, diff_cos, diff_sin = _rope_tables(s)
    gq_a = _pad_gain(mla_q_norm_g[0], 0, MLA_NOPE, LANES)
    gq_b = _pad_gain(mla_q_norm_g[0], MLA_NOPE, MLA_QK, LANES)
    gk_a = _pad_gain(mla_k_norm_g[0], 0, MLA_NOPE, LANES)
    gk_b = _pad_gain(mla_k_norm_g[0], MLA_NOPE, MLA_QK, LANES)
    g_diff = jnp.concatenate([jnp.tile(diff_q_norm_g[0] * DIFF_QSCALE, 2 * DIFF_HEADS),
                              jnp.tile(diff_k_norm_g[0], 2 * DIFF_HEADS)]).reshape(1, -1)

    n = _rmsnorm(x2, attn_norm_g[0], min(256, s))
    z_lat = _matmul(n, w_lat, tm, Z_LAT // 2, BF16, "in_proj_lat")
    z_diff = _matmul(n, w_diff, tm, 1024, BF16, "in_proj_diff")
    q_a = _q_prep(z_lat, q_latent_norm_g[0].reshape(1, -1), w_uq_p, gq_a, gq_b,
                  mla_cos, mla_sa, mla_sb, tp)
    k_a, v_a = _kv_prep(z_lat, kv_latent_norm_g[0].reshape(1, -1), w_uk, w_uv, gk_a, gk_b,
                        mla_cos, mla_sa, mla_sb, tp)
    o_a = _mla_attn(q_a, k_a, v_a, tq, tk)
    qk_b = _diff_prep(z_diff, g_diff, diff_cos, diff_sin, tm)
    o_b = _diff_attn(qk_b, z_diff, lambda_q1[0].reshape(1, -1), lambda_k1[0].reshape(1, -1),
                     lambda_q2[0].reshape(1, -1), lambda_k2[0].reshape(1, -1),
                     diff_subln_g[0].reshape(1, -1), tq, tk)
    h = _wo_proj(o_a, o_b, w_o[0], x2, tm, 512)

    m = _rmsnorm(h, ffn_norm_g[0], min(256, s))
    a = _gate_up(m, w_gate[0], w_up[0], min(2048, s), 256)
    out = _down_proj(a, w_down_b, h, min(512, s), 512)
    return out.reshape(b, s, d)
```

```python
import functools
import math
from typing import Any, NamedTuple

import jax
import jax.numpy as jnp
import numpy as np
from jax import lax
from jax.experimental import pallas as pl
from jax.experimental.pallas import tpu as pltpu

F32 = jnp.float32
BF16 = jnp.bfloat16

D_MODEL = 4096
CHUNK = 64
ROPE_THETA = 10000.0
EPS = 1e-6
MLA_HEADS = 16
MLA_Q_LORA = 768
MLA_KV_LORA = 512
MLA_NOPE = 128
MLA_ROPE = 64
MLA_QK = MLA_NOPE + MLA_ROPE
MLA_V = 128
LOG2E = math.log2(math.e)
MLA_SCALE = 1.0 / math.sqrt(MLA_QK)
MLA_QSCALE = MLA_SCALE * LOG2E
DIFF_HEADS = 8
DIFF_HEAD_DIM = 128
DIFF_V = 2 * DIFF_HEAD_DIM
DIFF_SCALE = 1.0 / math.sqrt(DIFF_HEAD_DIM)
DIFF_QSCALE = DIFF_SCALE * LOG2E
DIFF_QK_COLS = 2 * DIFF_HEADS * DIFF_HEAD_DIM
DIFF_V_COLS = DIFF_HEADS * DIFF_V
D_FF = -(-8 * D_MODEL // (3 * 256)) * 256
LAMBDA_INIT = 0.8 - 0.6 * math.exp(-0.3 * 0)

LANES = 128
MLA_HEAD_PAD = 2 * LANES
Z_LAT = 1536
Z_KPE = MLA_Q_LORA + MLA_KV_LORA
Z_LAT_USED = Z_KPE + MLA_ROPE
NEG = -1e30
VMEM_LIMIT = 56 * 1024 * 1024


def _cparams(sem):
    return pltpu.CompilerParams(dimension_semantics=sem, vmem_limit_bytes=VMEM_LIMIT)


def _cast_kernel(w_ref, o_ref):
    o_ref[...] = w_ref[...].astype(o_ref.dtype)


def _cast_bf16(w, tr, tc):
    r, c = w.shape
    return pl.pallas_call(
        _cast_kernel,
        grid=(r // tr, c // tc),
        in_specs=[pl.BlockSpec((tr, tc), lambda i, j: (i, j))],
        out_specs=pl.BlockSpec((tr, tc), lambda i, j: (i, j)),
        out_shape=jax.ShapeDtypeStruct((r, c), BF16),
        compiler_params=_cparams(("parallel", "parallel")),
        name="cast_bf16",
    )(w)


def _rmsnorm_kernel(x_ref, g_ref, o_ref):
    x = x_ref[...]
    ms = jnp.mean(x * x, axis=-1, keepdims=True)
    o_ref[...] = (x * lax.rsqrt(ms + EPS) * g_ref[...]).astype(o_ref.dtype)


def _rmsnorm(x, g, tm):
    m, d = x.shape
    return pl.pallas_call(
        _rmsnorm_kernel,
        grid=(m // tm,),
        in_specs=[pl.BlockSpec((tm, d), lambda i: (i, 0)),
                  pl.BlockSpec((1, d), lambda i: (0, 0))],
        out_specs=pl.BlockSpec((tm, d), lambda i: (i, 0)),
        out_shape=jax.ShapeDtypeStruct((m, d), BF16),
        compiler_params=_cparams(("parallel",)),
        name="rmsnorm",
    )(x, g.reshape(1, d))


def _mm_kernel(a_ref, b_ref, o_ref):
    o_ref[...] = jnp.dot(a_ref[...], b_ref[...],
                         preferred_element_type=F32).astype(o_ref.dtype)


def _matmul(a, b, tm, tn, out_dtype, name):
    m, k = a.shape
    _, n = b.shape
    return pl.pallas_call(
        _mm_kernel,
        grid=(m // tm, n // tn),
        in_specs=[pl.BlockSpec((tm, k), lambda i, j: (i, 0)),
                  pl.BlockSpec((k, tn), lambda i, j: (0, j))],
        out_specs=pl.BlockSpec((tm, tn), lambda i, j: (i, j)),
        out_shape=jax.ShapeDtypeStruct((m, n), out_dtype),
        compiler_params=_cparams(("parallel", "arbitrary")),
        name=name,
    )(a, b)


def _latent_norm(c, g):
    ms = jnp.mean(c * c, axis=-1, keepdims=True)
    return (c * lax.rsqrt(ms + EPS) * g).astype(BF16)


def _mla_rope(y, cos, sa, sb):
    return y * cos + pltpu.roll(y, 96, 1) * sa + pltpu.roll(y, 32, 1) * sb


def _q_prep_kernel(z_ref, gl_ref, w_ref, ga_ref, gb_ref, cos_ref, sa_ref, sb_ref, o_ref):
    c = z_ref[:, :MLA_Q_LORA].astype(F32)
    lat = _latent_norm(c, gl_ref[...])
    q = jnp.dot(lat, w_ref[...], preferred_element_type=F32)
    cos, sa, sb = cos_ref[...], sa_ref[...], sb_ref[...]
    ga, gb = ga_ref[...], gb_ref[...]
    for h in range(MLA_HEADS):
        lo = h * MLA_HEAD_PAD
        qa = q[:, lo:lo + LANES]
        qb = q[:, lo + LANES:lo + 2 * LANES]
        ss = jnp.sum(qa * qa + qb * qb, axis=-1, keepdims=True)
        r = lax.rsqrt(ss * (1.0 / MLA_QK) + EPS)
        ya = qa * r * ga
        yb = _mla_rope(qb * r * gb, cos, sa, sb)
        o_ref[:, lo:lo + LANES] = (ya * MLA_QSCALE).astype(o_ref.dtype)
        o_ref[:, lo + LANES:lo + 2 * LANES] = (yb * MLA_QSCALE).astype(o_ref.dtype)


def _kv_prep_kernel(z_ref, gl_ref, wk_ref, wv_ref, ga_ref, gb_ref, cos_ref, sa_ref, sb_ref,
                    k_ref, v_ref):
    c = z_ref[:, MLA_Q_LORA:Z_KPE].astype(F32)
    lat = _latent_norm(c, gl_ref[...])
    kn = jnp.dot(lat, wk_ref[...], preferred_element_type=F32)
    v_ref[...] = jnp.dot(lat, wv_ref[...], preferred_element_type=F32).astype(v_ref.dtype)
    kpe = z_ref[:, Z_KPE:Z_KPE + LANES].astype(F32)
    ss_pe = jnp.sum(kpe * kpe, axis=-1, keepdims=True)
    pe_rot = _mla_rope(kpe * gb_ref[...], cos_ref[...], sa_ref[...], sb_ref[...])
    ga = ga_ref[...]
    for h in range(MLA_HEADS):
        a = kn[:, h * LANES:(h + 1) * LANES]
        ss = jnp.sum(a * a, axis=-1, keepdims=True) + ss_pe
        r = lax.rsqrt(ss * (1.0 / MLA_QK) + EPS)
        lo = h * MLA_HEAD_PAD
        k_ref[:, lo:lo + LANES] = (a * r * ga).astype(k_ref.dtype)
        k_ref[:, lo + LANES:lo + 2 * LANES] = (pe_rot * r).astype(k_ref.dtype)


def _row_spec(tm, n, col=0):
    return pl.BlockSpec((tm, n), lambda i: (i, col))


def _const_spec(shape):
    return pl.BlockSpec(shape, lambda i: (0, 0))


def _q_prep(z, gl, w, ga, gb, cos, sa, sb, tm):
    m = z.shape[0]
    n = MLA_HEADS * MLA_HEAD_PAD
    return pl.pallas_call(
        _q_prep_kernel,
        grid=(m // tm,),
        in_specs=[_row_spec(tm, Z_LAT), _const_spec(gl.shape), _const_spec(w.shape),
                  _const_spec(ga.shape), _const_spec(gb.shape),
                  _row_spec(tm, LANES), _row_spec(tm, LANES), _row_spec(tm, LANES)],
        out_specs=_row_spec(tm, n),
        out_shape=jax.ShapeDtypeStruct((m, n), BF16),
        compiler_params=_cparams(("parallel",)),
        name="mla_q_prep",
    )(z, gl, w, ga, gb, cos, sa, sb)


def _kv_prep(z, gl, wk, wv, ga, gb, cos, sa, sb, tm):
    m = z.shape[0]
    nk = MLA_HEADS * MLA_HEAD_PAD
    nv = MLA_HEADS * MLA_V
    return pl.pallas_call(
        _kv_prep_kernel,
        grid=(m // tm,),
        in_specs=[_row_spec(tm, Z_LAT), _const_spec(gl.shape), _const_spec(wk.shape),
                  _const_spec(wv.shape), _const_spec(ga.shape), _const_spec(gb.shape),
                  _row_spec(tm, LANES), _row_spec(tm, LANES), _row_spec(tm, LANES)],
        out_specs=[_row_spec(tm, nk), _row_spec(tm, nv)],
        out_shape=[jax.ShapeDtypeStruct((m, nk), BF16), jax.ShapeDtypeStruct((m, nv), BF16)],
        compiler_params=_cparams(("parallel",)),
        name="mla_kv_prep",
    )(z, gl, wk, wv, ga, gb, cos, sa, sb)


DIFF_PREP_COLS = 1024


def _diff_prep_kernel(x_ref, g_ref, cos_ref, sin_ref, o_ref):
    cos, sin = cos_ref[...], sin_ref[...]
    for c in range(DIFF_PREP_COLS // LANES):
        sl = slice(c * LANES, (c + 1) * LANES)
        x = x_ref[:, sl].astype(F32)
        ms = jnp.mean(x * x, axis=-1, keepdims=True)
        y = x * lax.rsqrt(ms + EPS) * g_ref[:, sl]
        o_ref[:, sl] = (y * cos + pltpu.roll(y, 64, 1) * sin).astype(o_ref.dtype)


def _diff_prep(z, g, cos, sin, tm):
    m = z.shape[0]
    n = 2 * DIFF_QK_COLS
    nb = n // DIFF_PREP_COLS
    return pl.pallas_call(
        _diff_prep_kernel,
        grid=(m // tm, nb),
        in_specs=[pl.BlockSpec((tm, DIFF_PREP_COLS), lambda i, j: (i, j)),
                  pl.BlockSpec((1, DIFF_PREP_COLS), lambda i, j: (0, j)),
                  pl.BlockSpec((tm, LANES), lambda i, j: (i, 0)),
                  pl.BlockSpec((tm, LANES), lambda i, j: (i, 0))],
        out_specs=pl.BlockSpec((tm, DIFF_PREP_COLS), lambda i, j: (i, j)),
        out_shape=jax.ShapeDtypeStruct((m, n), BF16),
        compiler_params=_cparams(("parallel", "arbitrary")),
        name="diff_qk_prep",
    )(z, g, cos, sin)


def _scores_t(k, q):
    return lax.dot_general(k, q, (((1,), (1,)), ((), ())), preferred_element_type=F32)


class _Buf(NamedTuple):
    s: Any
    bmax: Any

    def cols(self, sl):
        return _Buf(self.s.at[:, sl], self.bmax.at[:, sl])


def _scores_into(buf, k, q):
    s = _scores_t(k, q)
    buf.s[...] = s
    buf.bmax[...] = jnp.max(s, axis=0, keepdims=True)


def _mask_diag_block(buf, c0, tk):
    upper = lax.broadcasted_iota(jnp.int32, (CHUNK, LANES), 1) >= CHUNK
    for g in range(tk // LANES):
        cols = pl.ds(c0 + g * LANES, LANES)
        r0 = g * LANES + CHUNK
        edge = jnp.where(upper, buf.s[r0:r0 + CHUNK, cols], NEG)
        buf.s[r0:r0 + CHUNK, cols] = edge
        if r0 + CHUNK < tk:
            buf.s[r0 + CHUNK:tk, cols] = jnp.full((tk - r0 - CHUNK, LANES), NEG, F32)
        buf.bmax[:, cols] = jnp.maximum(jnp.max(buf.s[0:r0, cols], axis=0, keepdims=True),
                                        jnp.max(edge, axis=0, keepdims=True))


def _online_update_t(buf, v, m_sc, l_sc, acc_sc):
    tk, tq = buf.s.shape
    m_prev = m_sc[...]
    m_new = jnp.maximum(m_prev, buf.bmax[...])
    alpha = jnp.exp2(m_prev - m_new)
    p_t = jnp.exp2(buf.s[...] - m_new)
    l_sc[...] = alpha * l_sc[...] + jnp.sum(p_t, axis=0, keepdims=True)
    pv = lax.dot_general(v, p_t.astype(BF16), (((0,), (0,)), ((), ())),
                         preferred_element_type=F32)
    acc_sc[...] = alpha * acc_sc[...] + pv
    m_sc[...] = m_new


def _init_state(m_sc, l_sc, acc_sc):
    m_sc[...] = jnp.full(m_sc.shape, NEG, F32)
    l_sc[...] = jnp.zeros(l_sc.shape, F32)
    acc_sc[...] = jnp.zeros(acc_sc.shape, F32)


class _Chain(NamedTuple):
    q: Any
    k_blk: Any
    v_blk: Any
    m: Any
    l: Any
    acc: Any
    buf_a: Any
    buf_b: Any


def _flash_pairs(i, tq, tk, chains):
    assert tq == 2 * tk

    def k_start(j):
        return pl.multiple_of(j * tk, tk)

    def qk(use_b, j):
        k0 = k_start(j)
        for c in chains:
            _scores_into(c.buf_b if use_b else c.buf_a, c.k_blk(k0), c.q)

    def update(use_b, j):
        k0 = k_start(j)
        for c in chains:
            _online_update_t(c.buf_b if use_b else c.buf_a, c.v_blk(k0), c.m, c.l, c.acc)

    for c in chains:
        _init_state(c.m, c.l, c.acc)
    qk(False, 0)

    def body(p, carry):
        qk(True, 2 * p + 1)
        update(False, 2 * p)
        qk(False, 2 * p + 2)
        update(True, 2 * p + 1)
        return carry

    lax.fori_loop(0, i, body, 0)
    half = pl.ds(tk, tk)
    k_last = k_start(2 * i + 1)
    for c in chains:
        _mask_diag_block(c.buf_a, 0, tk)
        _scores_into(c.buf_b.cols(half), c.k_blk(k_last), c.q[tk:, :])
        _mask_diag_block(c.buf_b, tk, tk)
    update(False, 2 * i)
    for c in chains:
        _online_update_t(c.buf_b.cols(half), c.v_blk(k_last),
                         c.m.at[:, half], c.l.at[:, half], c.acc.at[:, half])


def _score_scratch(n, tq, tk):
    one = [pltpu.VMEM((n, tk, tq), F32), pltpu.VMEM((n, 1, tq), F32)]
    return one + one


def _bufs(refs, idx):
    sa, ma, sb, mb = refs
    return _Buf(sa.at[idx], ma.at[idx]), _Buf(sb.at[idx], mb.at[idx])


MLA_HEADS_PER_STEP = 2


def _mla_attn_kernel(q_ref, k_ref, v_ref, o_ref, m_sc, l_sc, acc_sc, *score_refs, tq, tk):
    nh = m_sc.shape[0]
    chains = []
    for h in range(nh):
        qk_cols = slice(h * MLA_HEAD_PAD, (h + 1) * MLA_HEAD_PAD)
        v_cols = slice(h * MLA_V, (h + 1) * MLA_V)
        buf_a, buf_b = _bufs(score_refs, h)
        chains.append(_Chain(
            q=q_ref[:, qk_cols],
            k_blk=functools.partial(lambda k0, cols: k_ref[pl.ds(k0, tk), cols], cols=qk_cols),
            v_blk=functools.partial(lambda k0, cols: v_ref[pl.ds(k0, tk), cols], cols=v_cols),
            m=m_sc.at[h], l=l_sc.at[h], acc=acc_sc.at[h], buf_a=buf_a, buf_b=buf_b))
    _flash_pairs(pl.program_id(1), tq, tk, chains)
    for h in range(nh):
        o_ref[:, h * MLA_V:(h + 1) * MLA_V] = (acc_sc[h] / l_sc[h]).T.astype(o_ref.dtype)


def _mla_attn(q, k, v, tq, tk):
    s = q.shape[0]
    nh = MLA_HEADS_PER_STEP
    kern = functools.partial(_mla_attn_kernel, tq=tq, tk=tk)
    return pl.pallas_call(
        kern,
        grid=(MLA_HEADS // nh, s // tq),
        in_specs=[pl.BlockSpec((tq, nh * MLA_HEAD_PAD), lambda h, i: (i, h)),
                  pl.BlockSpec((s, nh * MLA_HEAD_PAD), lambda h, i: (0, h)),
                  pl.BlockSpec((s, nh * MLA_V), lambda h, i: (0, h))],
        out_specs=pl.BlockSpec((tq, nh * MLA_V), lambda h, i: (i, h)),
        out_shape=jax.ShapeDtypeStruct((s, MLA_HEADS * MLA_V), BF16),
        scratch_shapes=[pltpu.VMEM((nh, 1, tq), F32), pltpu.VMEM((nh, 1, tq), F32),
                        pltpu.VMEM((nh, MLA_V, tq), F32)] + _score_scratch(nh, tq, tk),
        compiler_params=_cparams(("parallel", "arbitrary")),
        name="mla_attn",
    )(q, k, v)


def _diff_attn_kernel(q_ref, k_ref, v_ref, lq1_ref, lk1_ref, lq2_ref, lk2_ref, g_ref, o_ref,
                      m_sc, l_sc, acc_sc, *score_refs, tq, tk):
    chains = []
    for t in range(2):
        cols = slice(t * DIFF_HEAD_DIM, (t + 1) * DIFF_HEAD_DIM)
        buf_a, buf_b = _bufs(score_refs, t)
        chains.append(_Chain(
            q=q_ref[:, cols],
            k_blk=functools.partial(lambda k0, cols: k_ref[pl.ds(k0, tk), cols], cols=cols),
            v_blk=lambda k0: v_ref[pl.ds(k0, tk), :],
            m=m_sc.at[t], l=l_sc.at[t], acc=acc_sc.at[t], buf_a=buf_a, buf_b=buf_b))
    _flash_pairs(pl.program_id(1), tq, tk, chains)

    lam = (jnp.exp(jnp.sum(lq1_ref[...] * lk1_ref[...], axis=-1, keepdims=True))
           - jnp.exp(jnp.sum(lq2_ref[...] * lk2_ref[...], axis=-1, keepdims=True))
           + LAMBDA_INIT)
    o = (acc_sc[0] / l_sc[0] - lam * (acc_sc[1] / l_sc[1])).T
    ms = jnp.mean(o * o, axis=-1, keepdims=True)
    o = o * lax.rsqrt(ms + EPS) * g_ref[...]
    o_ref[...] = (o * (1.0 - LAMBDA_INIT)).astype(o_ref.dtype)


def _diff_attn(qk, z, lq1, lk1, lq2, lk2, g, tq, tk):
    s = qk.shape[0]
    kern = functools.partial(_diff_attn_kernel, tq=tq, tk=tk)
    k_off = DIFF_QK_COLS // DIFF_V
    v_off = 2 * DIFF_QK_COLS // DIFF_V
    vec = pl.BlockSpec((1, DIFF_HEAD_DIM), lambda h, i: (0, 0))
    return pl.pallas_call(
        kern,
        grid=(DIFF_HEADS, s // tq),
        in_specs=[pl.BlockSpec((tq, DIFF_V), lambda h, i: (i, h)),
                  pl.BlockSpec((s, DIFF_V), lambda h, i: (0, h + k_off)),
                  pl.BlockSpec((s, DIFF_V), lambda h, i: (0, h + v_off)),
                  vec, vec, vec, vec,
                  pl.BlockSpec((1, DIFF_V), lambda h, i: (0, 0))],
        out_specs=pl.BlockSpec((tq, DIFF_V), lambda h, i: (i, h)),
        out_shape=jax.ShapeDtypeStruct((s, DIFF_HEADS * DIFF_V), BF16),
        scratch_shapes=[pltpu.VMEM((2, 1, tq), F32), pltpu.VMEM((2, 1, tq), F32),
                        pltpu.VMEM((2, DIFF_V, tq), F32)] + _score_scratch(2, tq, tk),
        compiler_params=_cparams(("parallel", "arbitrary")),
        name="diff_attn",
    )(qk, qk, z, lq1, lk1, lq2, lk2, g)


def _wo_kernel(oa_ref, ob_ref, wa_ref, wb_ref, x_ref, h_ref):
    acc = jnp.dot(oa_ref[...], wa_ref[...].astype(BF16), preferred_element_type=F32)
    acc = acc + jnp.dot(ob_ref[...], wb_ref[...].astype(BF16), preferred_element_type=F32)
    h_ref[...] = x_ref[...] + acc


def _wo_proj(oa, ob, w, x, tm, tn):
    m, ka = oa.shape
    kb = ob.shape[1]
    assert ka == kb and w.shape[0] == ka + kb
    n = w.shape[1]
    return pl.pallas_call(
        _wo_kernel,
        grid=(m // tm, n // tn),
        in_specs=[pl.BlockSpec((tm, ka), lambda i, j: (i, 0)),
                  pl.BlockSpec((tm, kb), lambda i, j: (i, 0)),
                  pl.BlockSpec((ka, tn), lambda i, j: (0, j)),
                  pl.BlockSpec((kb, tn), lambda i, j: (1, j)),
                  pl.BlockSpec((tm, tn), lambda i, j: (i, j))],
        out_specs=pl.BlockSpec((tm, tn), lambda i, j: (i, j)),
        out_shape=jax.ShapeDtypeStruct((m, n), F32),
        compiler_params=_cparams(("parallel", "arbitrary")),
        name="wo_proj",
    )(oa, ob, w, w, x)


def _gate_up_kernel(a_ref, wg_ref, wu_ref, wd_ref, o_ref, wd_o_ref):
    a = a_ref[...]
    g = jnp.dot(a, wg_ref[...].astype(BF16), preferred_element_type=F32)
    u = jnp.dot(a, wu_ref[...].astype(BF16), preferred_element_type=F32)
    o_ref[...] = (g * (1.0 / (1.0 + jnp.exp(-g))) * u).astype(o_ref.dtype)
    wd_o_ref[...] = wd_ref[...].astype(wd_o_ref.dtype)


def _gate_up(a, wg, wu, wd, tm, tn):
    m, k = a.shape
    n = wg.shape[1]
    ni, nj = m // tm, n // tn
    kd, nd = wd.shape
    assert kd % (ni * nj) == 0
    slab = kd // (ni * nj)
    return pl.pallas_call(
        _gate_up_kernel,
        grid=(ni, nj),
        in_specs=[pl.BlockSpec((tm, k), lambda i, j: (i, 0), pipeline_mode=pl.Buffered(1)),
                  pl.BlockSpec((k, tn), lambda i, j: (0, j)),
                  pl.BlockSpec((k, tn), lambda i, j: (0, j)),
                  pl.BlockSpec((slab, nd), lambda i, j: (i * nj + j, 0))],
        out_specs=[pl.BlockSpec((tm, tn), lambda i, j: (i, j)),
                   pl.BlockSpec((slab, nd), lambda i, j: (i * nj + j, 0))],
        out_shape=[jax.ShapeDtypeStruct((m, n), BF16), jax.ShapeDtypeStruct((kd, nd), BF16)],
        compiler_params=_cparams(("arbitrary", "arbitrary")),
        name="ffn_gate_up",
    )(a, wg, wu, wd)


def _down_kernel(a_ref, w_ref, h_ref, o_ref):
    o_ref[...] = h_ref[...] + jnp.dot(a_ref[...], w_ref[...], preferred_element_type=F32)


def _down_proj(a, w, h, tm, tn):
    m, k = a.shape
    n = w.shape[1]
    return pl.pallas_call(
        _down_kernel,
        grid=(m // tm, n // tn),
        in_specs=[pl.BlockSpec((tm, k), lambda i, j: (i, 0)),
                  pl.BlockSpec((k, tn), lambda i, j: (0, j)),
                  pl.BlockSpec((tm, tn), lambda i, j: (i, j))],
        out_specs=pl.BlockSpec((tm, tn), lambda i, j: (i, j)),
        out_shape=jax.ShapeDtypeStruct((m, n), F32),
        compiler_params=_cparams(("parallel", "arbitrary")),
        name="ffn_down",
    )(a, w, h)


def _rope_tables(s):
    pos = np.arange(s, dtype=np.float64)

    def cs(half):
        inv_freq = ROPE_THETA ** (-np.arange(half, dtype=np.float64) / half)
        ang = pos[:, None] * inv_freq[None, :]
        return np.cos(ang), np.sin(ang)

    c, sn = cs(MLA_ROPE // 2)
    z32 = np.zeros_like(c)
    z64 = np.zeros((s, 64))
    mla_cos = np.concatenate([c, c, z64], axis=-1)
    mla_sa = np.concatenate([-sn, z32, z64], axis=-1)
    mla_sb = np.concatenate([z32, sn, z64], axis=-1)
    c, sn = cs(DIFF_HEAD_DIM // 2)
    diff_cos = np.concatenate([c, c], axis=-1)
    diff_sin = np.concatenate([-sn, sn], axis=-1)
    return tuple(jnp.asarray(t.astype(np.float32))
                 for t in (mla_cos, mla_sa, mla_sb, diff_cos, diff_sin))


def _pad_gain(g, lo, hi, width):
    return jnp.pad(g[lo:hi], (0, width - (hi - lo))).reshape(1, width)


def kernel(x, attn_norm_g, w_in, q_latent_norm_g, kv_latent_norm_g, w_uq, w_ukv, mla_q_norm_g, mla_k_norm_g, diff_q_norm_g, diff_k_norm_g, lambda_q1, lambda_k1, lambda_q2, lambda_k2, diff_subln_g, w_o, ffn_norm_g, w_gate, w_up, w_down):
    b, s, d = x.shape
    assert b == 1 and d == D_MODEL and s % 512 == 0
    x2 = x.reshape(s, d)
    tm = min(1024, s)
    tp = min(512, s)
    tq = min(1024, s)
    tk = tq // 2

    w_in_b = _cast_bf16(w_in[0], 256, w_in.shape[2])
    w_lat = jnp.pad(w_in_b[:, :Z_LAT_USED], ((0, 0), (0, Z_LAT - Z_LAT_USED)))
    w_diff = w_in_b[:, Z_LAT_USED:]
    w_uq_p = jnp.pad(w_uq[0].reshape(MLA_Q_LORA, MLA_HEADS, MLA_QK),
                     ((0, 0), (0, 0), (0, MLA_HEAD_PAD - MLA_QK))
                     ).reshape(MLA_Q_LORA, MLA_HEADS * MLA_HEAD_PAD).astype(BF16)
    w_ukv3 = w_ukv[0].reshape(MLA_KV_LORA, MLA_HEADS, MLA_NOPE + MLA_V)
    w_uk = w_ukv3[:, :, :MLA_NOPE].reshape(MLA_KV_LORA, MLA_HEADS * MLA_NOPE).astype(BF16)
    w_uv = w_ukv3[:, :, MLA_NOPE:].reshape(MLA_KV_LORA, MLA_HEADS * MLA_V).astype(BF16)

    mla_cos, mla_sa, mla_sb, diff_cos, diff_sin = _rope_tables(s)
    gq_a = _pad_gain(mla_q_norm_g[0], 0, MLA_NOPE, LANES)
    gq_b = _pad_gain(mla_q_norm_g[0], MLA_NOPE, MLA_QK, LANES)
    gk_a = _pad_gain(mla_k_norm_g[0], 0, MLA_NOPE, LANES)
    gk_b = _pad_gain(mla_k_norm_g[0], MLA_NOPE, MLA_QK, LANES)
    g_diff = jnp.concatenate([jnp.tile(diff_q_norm_g[0] * DIFF_QSCALE, 2 * DIFF_HEADS),
                              jnp.tile(diff_k_norm_g[0], 2 * DIFF_HEADS)]).reshape(1, -1)

    n = _rmsnorm(x2, attn_norm_g[0], min(512, s))
    z_lat = _matmul(n, w_lat, tm, Z_LAT // 2, BF16, "in_proj_lat")
    z_diff = _matmul(n, w_diff, tm, 1024, BF16, "in_proj_diff")
    q_a = _q_prep(z_lat, q_latent_norm_g[0].reshape(1, -1), w_uq_p, gq_a, gq_b,
                  mla_cos, mla_sa, mla_sb, tp)
    k_a, v_a = _kv_prep(z_lat, kv_latent_norm_g[0].reshape(1, -1), w_uk, w_uv, gk_a, gk_b,
                        mla_cos, mla_sa, mla_sb, tp)
    o_a = _mla_attn(q_a, k_a, v_a, tq, tk)
    qk_b = _diff_prep(z_diff, g_diff, diff_cos, diff_sin, tm)
    o_b = _diff_attn(qk_b, z_diff, lambda_q1[0].reshape(1, -1), lambda_k1[0].reshape(1, -1),
                     lambda_q2[0].reshape(1, -1), lambda_k2[0].reshape(1, -1),
                     diff_subln_g[0].reshape(1, -1), tq, tk)
    h = _wo_proj(o_a, o_b, w_o[0], x2, tm, 512)

    m = _rmsnorm(h, ffn_norm_g[0], min(512, s))
    a, w_down_b = _gate_up(m, w_gate[0], w_up[0], w_down[0], min(2048, s), 256)
    out = _down_proj(a, w_down_b, h, min(512, s), 512)
    return out.reshape(b, s, d)
```

```python
import functools
import math
from typing import Any, NamedTuple

import jax
import jax.numpy as jnp
import numpy as np
from jax import lax
from jax.experimental import pallas as pl
from jax.experimental.pallas import tpu as pltpu

F32 = jnp.float32
BF16 = jnp.bfloat16

D_MODEL = 4096
CHUNK = 64
ROPE_THETA = 10000.0
EPS = 1e-6
MLA_HEADS = 16
MLA_Q_LORA = 768
MLA_KV_LORA = 512
MLA_NOPE = 128
MLA_ROPE = 64
MLA_QK = MLA_NOPE + MLA_ROPE
MLA_V = 128
LOG2E = math.log2(math.e)
MLA_SCALE = 1.0 / math.sqrt(MLA_QK)
MLA_QSCALE = MLA_SCALE * LOG2E
DIFF_HEADS = 8
DIFF_HEAD_DIM = 128
DIFF_V = 2 * DIFF_HEAD_DIM
DIFF_SCALE = 1.0 / math.sqrt(DIFF_HEAD_DIM)
DIFF_QSCALE = DIFF_SCALE * LOG2E
DIFF_QK_COLS = 2 * DIFF_HEADS * DIFF_HEAD_DIM
DIFF_V_COLS = DIFF_HEADS * DIFF_V
D_FF = -(-8 * D_MODEL // (3 * 256)) * 256
LAMBDA_INIT = 0.8 - 0.6 * math.exp(-0.3 * 0)

LANES = 128
MLA_HEAD_PAD = 2 * LANES
Z_LAT = 1536
Z_KPE = MLA_Q_LORA + MLA_KV_LORA
Z_LAT_USED = Z_KPE + MLA_ROPE
NEG = -1e30
VMEM_LIMIT = 56 * 1024 * 1024


def _cparams(sem):
    return pltpu.CompilerParams(dimension_semantics=sem, vmem_limit_bytes=VMEM_LIMIT)


def _cast_kernel(w_ref, o_ref):
    o_ref[...] = w_ref[...].astype(o_ref.dtype)


def _cast_rows_bf16(w, row0, n_rows, tr):
    c = w.shape[1]
    assert row0 % tr == 0 and n_rows % tr == 0 and row0 + n_rows <= w.shape[0]
    blk0 = row0 // tr
    return pl.pallas_call(
        _cast_kernel,
        grid=(n_rows // tr,),
        in_specs=[pl.BlockSpec((tr, c), lambda i: (i + blk0, 0))],
        out_specs=pl.BlockSpec((tr, c), lambda i: (i, 0)),
        out_shape=jax.ShapeDtypeStruct((n_rows, c), BF16),
        compiler_params=_cparams(("parallel",)),
        name="cast_bf16",
    )(w)


def _rmsnorm_kernel(x_ref, g_ref, o_ref):
    x = x_ref[...]
    ms = jnp.mean(x * x, axis=-1, keepdims=True)
    o_ref[...] = (x * lax.rsqrt(ms + EPS) * g_ref[...]).astype(o_ref.dtype)


def _rmsnorm(x, g, tm):
    m, d = x.shape
    return pl.pallas_call(
        _rmsnorm_kernel,
        grid=(m // tm,),
        in_specs=[pl.BlockSpec((tm, d), lambda i: (i, 0)),
                  pl.BlockSpec((1, d), lambda i: (0, 0))],
        out_specs=pl.BlockSpec((tm, d), lambda i: (i, 0)),
        out_shape=jax.ShapeDtypeStruct((m, d), BF16),
        compiler_params=_cparams(("parallel",)),
        name="rmsnorm",
    )(x, g.reshape(1, d))


def _mm_nt_kernel(a_ref, bt_ref, o_ref):
    o_ref[...] = lax.dot_general(a_ref[...], bt_ref[...], (((1,), (1,)), ((), ())),
                                 preferred_element_type=F32).astype(o_ref.dtype)


def _matmul_nt(a, bt, tm, tn, out_dtype, name):
    m, k = a.shape
    n = bt.shape[0]
    assert bt.shape[1] == k and n % tn == 0
    return pl.pallas_call(
        _mm_nt_kernel,
        grid=(m // tm, n // tn),
        in_specs=[pl.BlockSpec((tm, k), lambda i, j: (i, 0)),
                  pl.BlockSpec((tn, k), lambda i, j: (j, 0))],
        out_specs=pl.BlockSpec((tm, tn), lambda i, j: (i, j)),
        out_shape=jax.ShapeDtypeStruct((m, n), out_dtype),
        compiler_params=_cparams(("parallel", "arbitrary")),
        name=name,
    )(a, bt)


def _latent_norm(c, g):
    ms = jnp.mean(c * c, axis=-1, keepdims=True)
    return (c * lax.rsqrt(ms + EPS) * g).astype(BF16)


def _mla_rope(y, cos, sa, sb):
    return y * cos + pltpu.roll(y, 96, 1) * sa + pltpu.roll(y, 32, 1) * sb


def _q_prep_kernel(z_ref, gl_ref, w_ref, ga_ref, gb_ref, cos_ref, sa_ref, sb_ref, o_ref):
    c = z_ref[:, :MLA_Q_LORA].astype(F32)
    lat = _latent_norm(c, gl_ref[...])
    q = jnp.dot(lat, w_ref[...], preferred_element_type=F32)
    cos, sa, sb = cos_ref[...], sa_ref[...], sb_ref[...]
    ga, gb = ga_ref[...], gb_ref[...]
    for h in range(MLA_HEADS):
        lo = h * MLA_HEAD_PAD
        qa = q[:, lo:lo + LANES]
        qb = q[:, lo + LANES:lo + 2 * LANES]
        ss = jnp.sum(qa * qa + qb * qb, axis=-1, keepdims=True)
        r = lax.rsqrt(ss * (1.0 / MLA_QK) + EPS)
        ya = qa * r * ga
        yb = _mla_rope(qb * r * gb, cos, sa, sb)
        o_ref[:, lo:lo + LANES] = (ya * MLA_QSCALE).astype(o_ref.dtype)
        o_ref[:, lo + LANES:lo + 2 * LANES] = (yb * MLA_QSCALE).astype(o_ref.dtype)


def _kv_prep_kernel(z_ref, gl_ref, wk_ref, wv_ref, ga_ref, gb_ref, cos_ref, sa_ref, sb_ref,
                    k_ref, v_ref):
    c = z_ref[:, MLA_Q_LORA:Z_KPE].astype(F32)
    lat = _latent_norm(c, gl_ref[...])
    kn = jnp.dot(lat, wk_ref[...], preferred_element_type=F32)
    v_ref[...] = jnp.dot(lat, wv_ref[...], preferred_element_type=F32).astype(v_ref.dtype)
    lane = lax.broadcasted_iota(jnp.int32, (1, LANES), 1)
    kpe = jnp.where(lane < MLA_ROPE, z_ref[:, Z_KPE:Z_KPE + LANES].astype(F32), 0.0)
    ss_pe = jnp.sum(kpe * kpe, axis=-1, keepdims=True)
    pe_rot = _mla_rope(kpe * gb_ref[...], cos_ref[...], sa_ref[...], sb_ref[...])
    ga = ga_ref[...]
    for h in range(MLA_HEADS):
        a = kn[:, h * LANES:(h + 1) * LANES]
        ss = jnp.sum(a * a, axis=-1, keepdims=True) + ss_pe
        r = lax.rsqrt(ss * (1.0 / MLA_QK) + EPS)
        lo = h * MLA_HEAD_PAD
        k_ref[:, lo:lo + LANES] = (a * r * ga).astype(k_ref.dtype)
        k_ref[:, lo + LANES:lo + 2 * LANES] = (pe_rot * r).astype(k_ref.dtype)


def _row_spec(tm, n, col=0):
    return pl.BlockSpec((tm, n), lambda i: (i, col))


def _const_spec(shape):
    return pl.BlockSpec(shape, lambda i: (0, 0))


def _q_prep(z, gl, w, ga, gb, cos, sa, sb, tm):
    m = z.shape[0]
    n = MLA_HEADS * MLA_HEAD_PAD
    return pl.pallas_call(
        _q_prep_kernel,
        grid=(m // tm,),
        in_specs=[_row_spec(tm, Z_LAT), _const_spec(gl.shape), _const_spec(w.shape),
                  _const_spec(ga.shape), _const_spec(gb.shape),
                  _row_spec(tm, LANES), _row_spec(tm, LANES), _row_spec(tm, LANES)],
        out_specs=_row_spec(tm, n),
        out_shape=jax.ShapeDtypeStruct((m, n), BF16),
        compiler_params=_cparams(("parallel",)),
        name="mla_q_prep",
    )(z, gl, w, ga, gb, cos, sa, sb)


def _kv_prep(z, gl, wk, wv, ga, gb, cos, sa, sb, tm):
    m = z.shape[0]
    nk = MLA_HEADS * MLA_HEAD_PAD
    nv = MLA_HEADS * MLA_V
    return pl.pallas_call(
        _kv_prep_kernel,
        grid=(m // tm,),
        in_specs=[_row_spec(tm, Z_LAT), _const_spec(gl.shape), _const_spec(wk.shape),
                  _const_spec(wv.shape), _const_spec(ga.shape), _const_spec(gb.shape),
                  _row_spec(tm, LANES), _row_spec(tm, LANES), _row_spec(tm, LANES)],
        out_specs=[_row_spec(tm, nk), _row_spec(tm, nv)],
        out_shape=[jax.ShapeDtypeStruct((m, nk), BF16), jax.ShapeDtypeStruct((m, nv), BF16)],
        compiler_params=_cparams(("parallel",)),
        name="mla_kv_prep",
    )(z, gl, wk, wv, ga, gb, cos, sa, sb)


DIFF_PREP_COLS = 1024


def _diff_prep_kernel(x_ref, g_ref, cos_ref, sin_ref, o_ref):
    cos, sin = cos_ref[...], sin_ref[...]
    for c in range(DIFF_PREP_COLS // LANES):
        sl = slice(c * LANES, (c + 1) * LANES)
        x = x_ref[:, sl].astype(F32)
        ms = jnp.mean(x * x, axis=-1, keepdims=True)
        y = x * lax.rsqrt(ms + EPS) * g_ref[:, sl]
        o_ref[:, sl] = (y * cos + pltpu.roll(y, 64, 1) * sin).astype(o_ref.dtype)


def _diff_prep(z, g, cos, sin, tm):
    m = z.shape[0]
    n = 2 * DIFF_QK_COLS
    nb = n // DIFF_PREP_COLS
    return pl.pallas_call(
        _diff_prep_kernel,
        grid=(m // tm, nb),
        in_specs=[pl.BlockSpec((tm, DIFF_PREP_COLS), lambda i, j: (i, j)),
                  pl.BlockSpec((1, DIFF_PREP_COLS), lambda i, j: (0, j)),
                  pl.BlockSpec((tm, LANES), lambda i, j: (i, 0)),
                  pl.BlockSpec((tm, LANES), lambda i, j: (i, 0))],
        out_specs=pl.BlockSpec((tm, DIFF_PREP_COLS), lambda i, j: (i, j)),
        out_shape=jax.ShapeDtypeStruct((m, n), BF16),
        compiler_params=_cparams(("parallel", "arbitrary")),
        name="diff_qk_prep",
    )(z, g, cos, sin)


def _scores_t(k, q):
    return lax.dot_general(k, q, (((1,), (1,)), ((), ())), preferred_element_type=F32)


class _Buf(NamedTuple):
    s: Any
    bmax: Any

    def cols(self, sl):
        return _Buf(self.s.at[:, sl], self.bmax.at[:, sl])


def _scores_into(buf, k, q):
    s = _scores_t(k, q)
    buf.s[...] = s
    buf.bmax[...] = jnp.max(s, axis=0, keepdims=True)


def _mask_diag_block(buf, c0, tk):
    upper = lax.broadcasted_iota(jnp.int32, (CHUNK, LANES), 1) >= CHUNK
    for g in range(tk // LANES):
        cols = pl.ds(c0 + g * LANES, LANES)
        r0 = g * LANES + CHUNK
        edge = jnp.where(upper, buf.s[r0:r0 + CHUNK, cols], NEG)
        buf.s[r0:r0 + CHUNK, cols] = edge
        if r0 + CHUNK < tk:
            buf.s[r0 + CHUNK:tk, cols] = jnp.full((tk - r0 - CHUNK, LANES), NEG, F32)
        buf.bmax[:, cols] = jnp.maximum(jnp.max(buf.s[0:r0, cols], axis=0, keepdims=True),
                                        jnp.max(edge, axis=0, keepdims=True))


def _online_update_t(buf, v, m_sc, l_sc, acc_sc):
    tk, tq = buf.s.shape
    m_prev = m_sc[...]
    m_new = jnp.maximum(m_prev, buf.bmax[...])
    alpha = jnp.exp2(m_prev - m_new)
    p_t = jnp.exp2(buf.s[...] - m_new)
    l_sc[...] = alpha * l_sc[...] + jnp.sum(p_t, axis=0, keepdims=True)
    pv = lax.dot_general(v, p_t.astype(BF16), (((0,), (0,)), ((), ())),
                         preferred_element_type=F32)
    acc_sc[...] = alpha * acc_sc[...] + pv
    m_sc[...] = m_new


def _init_state(m_sc, l_sc, acc_sc):
    m_sc[...] = jnp.full(m_sc.shape, NEG, F32)
    l_sc[...] = jnp.zeros(l_sc.shape, F32)
    acc_sc[...] = jnp.zeros(acc_sc.shape, F32)


class _Chain(NamedTuple):
    q: Any
    k_blk: Any
    v_blk: Any
    m: Any
    l: Any
    acc: Any
    buf_a: Any
    buf_b: Any


def _flash_pairs(i, tq, tk, chains):
    assert tq == 2 * tk

    def k_start(j):
        return pl.multiple_of(j * tk, tk)

    def qk(use_b, j):
        k0 = k_start(j)
        for c in chains:
            _scores_into(c.buf_b if use_b else c.buf_a, c.k_blk(k0), c.q)

    def update(use_b, j):
        k0 = k_start(j)
        for c in chains:
            _online_update_t(c.buf_b if use_b else c.buf_a, c.v_blk(k0), c.m, c.l, c.acc)

    for c in chains:
        _init_state(c.m, c.l, c.acc)
    qk(False, 0)

    def body(p, carry):
        qk(True, 2 * p + 1)
        update(False, 2 * p)
        qk(False, 2 * p + 2)
        update(True, 2 * p + 1)
        return carry

    lax.fori_loop(0, i, body, 0)
    half = pl.ds(tk, tk)
    k_last = k_start(2 * i + 1)
    for c in chains:
        _mask_diag_block(c.buf_a, 0, tk)
        _scores_into(c.buf_b.cols(half), c.k_blk(k_last), c.q[tk:, :])
        _mask_diag_block(c.buf_b, tk, tk)
    update(False, 2 * i)
    for c in chains:
        _online_update_t(c.buf_b.cols(half), c.v_blk(k_last),
                         c.m.at[:, half], c.l.at[:, half], c.acc.at[:, half])


def _score_scratch(n, tq, tk):
    one = [pltpu.VMEM((n, tk, tq), F32), pltpu.VMEM((n, 1, tq), F32)]
    return one + one


def _bufs(refs, idx):
    sa, ma, sb, mb = refs
    return _Buf(sa.at[idx], ma.at[idx]), _Buf(sb.at[idx], mb.at[idx])


MLA_HEADS_PER_STEP = 2


def _mla_attn_kernel(q_ref, k_ref, v_ref, o_ref, m_sc, l_sc, acc_sc, *score_refs, tq, tk):
    nh = m_sc.shape[0]
    chains = []
    for h in range(nh):
        qk_cols = slice(h * MLA_HEAD_PAD, (h + 1) * MLA_HEAD_PAD)
        v_cols = slice(h * MLA_V, (h + 1) * MLA_V)
        buf_a, buf_b = _bufs(score_refs, h)
        chains.append(_Chain(
            q=q_ref[:, qk_cols],
            k_blk=functools.partial(lambda k0, cols: k_ref[pl.ds(k0, tk), cols], cols=qk_cols),
            v_blk=functools.partial(lambda k0, cols: v_ref[pl.ds(k0, tk), cols], cols=v_cols),
            m=m_sc.at[h], l=l_sc.at[h], acc=acc_sc.at[h], buf_a=buf_a, buf_b=buf_b))
    _flash_pairs(pl.program_id(1), tq, tk, chains)
    for h in range(nh):
        o_ref[:, h * MLA_V:(h + 1) * MLA_V] = (acc_sc[h] / l_sc[h]).T.astype(o_ref.dtype)


def _mla_attn(q, k, v, tq, tk):
    s = q.shape[0]
    nh = MLA_HEADS_PER_STEP
    kern = functools.partial(_mla_attn_kernel, tq=tq, tk=tk)
    return pl.pallas_call(
        kern,
        grid=(MLA_HEADS // nh, s // tq),
        in_specs=[pl.BlockSpec((tq, nh * MLA_HEAD_PAD), lambda h, i: (i, h)),
                  pl.BlockSpec((s, nh * MLA_HEAD_PAD), lambda h, i: (0, h)),
                  pl.BlockSpec((s, nh * MLA_V), lambda h, i: (0, h))],
        out_specs=pl.BlockSpec((tq, nh * MLA_V), lambda h, i: (i, h)),
        out_shape=jax.ShapeDtypeStruct((s, MLA_HEADS * MLA_V), BF16),
        scratch_shapes=[pltpu.VMEM((nh, 1, tq), F32), pltpu.VMEM((nh, 1, tq), F32),
                        pltpu.VMEM((nh, MLA_V, tq), F32)] + _score_scratch(nh, tq, tk),
        compiler_params=_cparams(("parallel", "arbitrary")),
        name="mla_attn",
    )(q, k, v)


def _diff_attn_kernel(q_ref, k_ref, v_ref, lq1_ref, lk1_ref, lq2_ref, lk2_ref, g_ref, o_ref,
                      m_sc, l_sc, acc_sc, *score_refs, tq, tk):
    chains = []
    for t in range(2):
        cols = slice(t * DIFF_HEAD_DIM, (t + 1) * DIFF_HEAD_DIM)
        buf_a, buf_b = _bufs(score_refs, t)
        chains.append(_Chain(
            q=q_ref[:, cols],
            k_blk=functools.partial(lambda k0, cols: k_ref[pl.ds(k0, tk), cols], cols=cols),
            v_blk=lambda k0: v_ref[pl.ds(k0, tk), :],
            m=m_sc.at[t], l=l_sc.at[t], acc=acc_sc.at[t], buf_a=buf_a, buf_b=buf_b))
    _flash_pairs(pl.program_id(1), tq, tk, chains)

    lam = (jnp.exp(jnp.sum(lq1_ref[...] * lk1_ref[...], axis=-1, keepdims=True))
           - jnp.exp(jnp.sum(lq2_ref[...] * lk2_ref[...], axis=-1, keepdims=True))
           + LAMBDA_INIT)
    o = (acc_sc[0] / l_sc[0] - lam * (acc_sc[1] / l_sc[1])).T
    ms = jnp.mean(o * o, axis=-1, keepdims=True)
    o = o * lax.rsqrt(ms + EPS) * g_ref[...]
    o_ref[...] = (o * (1.0 - LAMBDA_INIT)).astype(o_ref.dtype)


def _diff_attn(qk, z, lq1, lk1, lq2, lk2, g, tq, tk):
    s = qk.shape[0]
    kern = functools.partial(_diff_attn_kernel, tq=tq, tk=tk)
    k_off = DIFF_QK_COLS // DIFF_V
    v_off = 2 * DIFF_QK_COLS // DIFF_V
    vec = pl.BlockSpec((1, DIFF_HEAD_DIM), lambda h, i: (0, 0))
    return pl.pallas_call(
        kern,
        grid=(DIFF_HEADS, s // tq),
        in_specs=[pl.BlockSpec((tq, DIFF_V), lambda h, i: (i, h)),
                  pl.BlockSpec((s, DIFF_V), lambda h, i: (0, h + k_off)),
                  pl.BlockSpec((s, DIFF_V), lambda h, i: (0, h + v_off)),
                  vec, vec, vec, vec,
                  pl.BlockSpec((1, DIFF_V), lambda h, i: (0, 0))],
        out_specs=pl.BlockSpec((tq, DIFF_V), lambda h, i: (i, h)),
        out_shape=jax.ShapeDtypeStruct((s, DIFF_HEADS * DIFF_V), BF16),
        scratch_shapes=[pltpu.VMEM((2, 1, tq), F32), pltpu.VMEM((2, 1, tq), F32),
                        pltpu.VMEM((2, DIFF_V, tq), F32)] + _score_scratch(2, tq, tk),
        compiler_params=_cparams(("parallel", "arbitrary")),
        name="diff_attn",
    )(qk, qk, z, lq1, lk1, lq2, lk2, g)


def _wo_kernel(oa_ref, ob_ref, wa_ref, wb_ref, x_ref, h_ref):
    acc = jnp.dot(oa_ref[...], wa_ref[...].astype(BF16), preferred_element_type=F32)
    acc = acc + jnp.dot(ob_ref[...], wb_ref[...].astype(BF16), preferred_element_type=F32)
    h_ref[...] = x_ref[...] + acc


def _wo_proj(oa, ob, w, x, tm, tn):
    m, ka = oa.shape
    kb = ob.shape[1]
    assert ka == kb and w.shape[0] == ka + kb
    n = w.shape[1]
    return pl.pallas_call(
        _wo_kernel,
        grid=(m // tm, n // tn),
        in_specs=[pl.BlockSpec((tm, ka), lambda i, j: (i, 0)),
                  pl.BlockSpec((tm, kb), lambda i, j: (i, 0)),
                  pl.BlockSpec((ka, tn), lambda i, j: (0, j)),
                  pl.BlockSpec((kb, tn), lambda i, j: (1, j)),
                  pl.BlockSpec((tm, tn), lambda i, j: (i, j))],
        out_specs=pl.BlockSpec((tm, tn), lambda i, j: (i, j)),
        out_shape=jax.ShapeDtypeStruct((m, n), F32),
        compiler_params=_cparams(("parallel", "arbitrary")),
        name="wo_proj",
    )(oa, ob, w, w, x)


def _gate_up_kernel(a_ref, wg_ref, wu_ref, wd_ref, o_ref, wd_o_ref):
    a = a_ref[...]
    g = jnp.dot(a, wg_ref[...].astype(BF16), preferred_element_type=F32)
    u = jnp.dot(a, wu_ref[...].astype(BF16), preferred_element_type=F32)
    o_ref[...] = (g * (1.0 / (1.0 + jnp.exp(-g))) * u).astype(o_ref.dtype)
    wd_o_ref[...] = wd_ref[...].astype(wd_o_ref.dtype)


def _gate_up(a, wg, wu, wd, tm, tn):
    m, k = a.shape
    n = wg.shape[1]
    ni, nj = m // tm, n // tn
    kd, nd = wd.shape
    assert kd % (ni * nj) == 0
    slab = kd // (ni * nj)
    return pl.pallas_call(
        _gate_up_kernel,
        grid=(ni, nj),
        in_specs=[pl.BlockSpec((tm, k), lambda i, j: (i, 0), pipeline_mode=pl.Buffered(1)),
                  pl.BlockSpec((k, tn), lambda i, j: (0, j)),
                  pl.BlockSpec((k, tn), lambda i, j: (0, j)),
                  pl.BlockSpec((slab, nd), lambda i, j: (i * nj + j, 0))],
        out_specs=[pl.BlockSpec((tm, tn), lambda i, j: (i, j)),
                   pl.BlockSpec((slab, nd), lambda i, j: (i * nj + j, 0))],
        out_shape=[jax.ShapeDtypeStruct((m, n), BF16), jax.ShapeDtypeStruct((kd, nd), BF16)],
        compiler_params=_cparams(("arbitrary", "arbitrary")),
        name="ffn_gate_up",
    )(a, wg, wu, wd)


def _down_kernel(a_ref, w_ref, h_ref, o_ref):
    o_ref[...] = h_ref[...] + jnp.dot(a_ref[...], w_ref[...], preferred_element_type=F32)


def _down_proj(a, w, h, tm, tn):
    m, k = a.shape
    n = w.shape[1]
    return pl.pallas_call(
        _down_kernel,
        grid=(m // tm, n // tn),
        in_specs=[pl.BlockSpec((tm, k), lambda i, j: (i, 0)),
                  pl.BlockSpec((k, tn), lambda i, j: (0, j)),
                  pl.BlockSpec((tm, tn), lambda i, j: (i, j))],
        out_specs=pl.BlockSpec((tm, tn), lambda i, j: (i, j)),
        out_shape=jax.ShapeDtypeStruct((m, n), F32),
        compiler_params=_cparams(("parallel", "arbitrary")),
        name="ffn_down",
    )(a, w, h)


def _rope_tables(s):
    pos = np.arange(s, dtype=np.float64)

    def cs(half):
        inv_freq = ROPE_THETA ** (-np.arange(half, dtype=np.float64) / half)
        ang = pos[:, None] * inv_freq[None, :]
        return np.cos(ang), np.sin(ang)

    c, sn = cs(MLA_ROPE // 2)
    z32 = np.zeros_like(c)
    z64 = np.zeros((s, 64))
    mla_cos = np.concatenate([c, c, z64], axis=-1)
    mla_sa = np.concatenate([-sn, z32, z64], axis=-1)
    mla_sb = np.concatenate([z32, sn, z64], axis=-1)
    c, sn = cs(DIFF_HEAD_DIM // 2)
    diff_cos = np.concatenate([c, c], axis=-1)
    diff_sin = np.concatenate([-sn, sn], axis=-1)
    return tuple(jnp.asarray(t.astype(np.float32))
                 for t in (mla_cos, mla_sa, mla_sb, diff_cos, diff_sin))


def _pad_gain(g, lo, hi, width):
    return jnp.pad(g[lo:hi], (0, width - (hi - lo))).reshape(1, width)


def kernel(x, attn_norm_g, w_in, q_latent_norm_g, kv_latent_norm_g, w_uq, w_ukv, mla_q_norm_g, mla_k_norm_g, diff_q_norm_g, diff_k_norm_g, lambda_q1, lambda_k1, lambda_q2, lambda_k2, diff_subln_g, w_o, ffn_norm_g, w_gate, w_up, w_down):
    b, s, d = x.shape
    assert b == 1 and d == D_MODEL and s % 512 == 0
    x2 = x.reshape(s, d)
    tm = min(1024, s)
    tp = min(512, s)
    tq = min(1024, s)
    tk = tq // 2

    w_in_t = jnp.swapaxes(w_in[0], 0, 1)
    n_diff = 2 * DIFF_QK_COLS + DIFF_V_COLS
    w_lat_t = _cast_rows_bf16(w_in_t, 0, Z_LAT, Z_LAT // 2)
    w_diff_t = _cast_rows_bf16(w_in_t, Z_LAT_USED, n_diff, math.gcd(Z_LAT_USED, n_diff))
    w_uq_p = jnp.pad(w_uq[0].reshape(MLA_Q_LORA, MLA_HEADS, MLA_QK),
                     ((0, 0), (0, 0), (0, MLA_HEAD_PAD - MLA_QK))
                     ).reshape(MLA_Q_LORA, MLA_HEADS * MLA_HEAD_PAD).astype(BF16)
    w_ukv3 = w_ukv[0].reshape(MLA_KV_LORA, MLA_HEADS, MLA_NOPE + MLA_V)
    w_uk = w_ukv3[:, :, :MLA_NOPE].reshape(MLA_KV_LORA, MLA_HEADS * MLA_NOPE).astype(BF16)
    w_uv = w_ukv3[:, :, MLA_NOPE:].reshape(MLA_KV_LORA, MLA_HEADS * MLA_V).astype(BF16)

    mla_cos, mla_sa, mla_sb, diff_cos, diff_sin = _rope_tables(s)
    gq_a = _pad_gain(mla_q_norm_g[0], 0, MLA_NOPE, LANES)
    gq_b = _pad_gain(mla_q_norm_g[0], MLA_NOPE, MLA_QK, LANES)
    gk_a = _pad_gain(mla_k_norm_g[0], 0, MLA_NOPE, LANES)
    gk_b = _pad_gain(mla_k_norm_g[0], MLA_NOPE, MLA_QK, LANES)
    g_diff = jnp.concatenate([jnp.tile(diff_q_norm_g[0] * DIFF_QSCALE, 2 * DIFF_HEADS),
                              jnp.tile(diff_k_norm_g[0], 2 * DIFF_HEADS)]).reshape(1, -1)

    n = _rmsnorm(x2, attn_norm_g[0], min(512, s))
    z_lat = _matmul_nt(n, w_lat_t, tm, Z_LAT // 2, BF16, "in_proj_lat")
    z_diff = _matmul_nt(n, w_diff_t, tm, 1024, BF16, "in_proj_diff")
    q_a = _q_prep(z_lat, q_latent_norm_g[0].reshape(1, -1), w_uq_p, gq_a, gq_b,
                  mla_cos, mla_sa, mla_sb, tp)
    k_a, v_a = _kv_prep(z_lat, kv_latent_norm_g[0].reshape(1, -1), w_uk, w_uv, gk_a, gk_b,
                        mla_cos, mla_sa, mla_sb, tp)
    o_a = _mla_attn(q_a, k_a, v_a, tq, tk)
    qk_b = _diff_prep(z_diff, g_diff, diff_cos, diff_sin, tm)
    o_b = _diff_attn(qk_b, z_diff, lambda_q1[0].reshape(1, -1), lambda_k1[0].reshape(1, -1),
                     lambda_q2[0].reshape(1, -1), lambda_k2[0].reshape(1, -1),
                     diff_subln_g[0].reshape(1, -1), tq, tk)
    h = _wo_proj(o_a, o_b, w_o[0], x2, tm, 512)

    m = _rmsnorm(h, ffn_norm_g[0], min(512, s))
    a, w_down_b = _gate_up(m, w_gate[0], w_up[0], w_down[0], min(2048, s), 256)
    out = _down_proj(a, w_down_b, h, min(512, s), 512)
    return out.reshape(b, s, d)
```

```python
import functools
import math
from typing import Any, NamedTuple

import jax
import jax.numpy as jnp
import numpy as np
from jax import lax
from jax.experimental import pallas as pl
from jax.experimental.pallas import tpu as pltpu

F32 = jnp.float32
BF16 = jnp.bfloat16

D_MODEL = 4096
CHUNK = 64
ROPE_THETA = 10000.0
EPS = 1e-6
MLA_HEADS = 16
MLA_Q_LORA = 768
MLA_KV_LORA = 512
MLA_NOPE = 128
MLA_ROPE = 64
MLA_QK = MLA_NOPE + MLA_ROPE
MLA_V = 128
LOG2E = math.log2(math.e)
MLA_SCALE = 1.0 / math.sqrt(MLA_QK)
MLA_QSCALE = MLA_SCALE * LOG2E
DIFF_HEADS = 8
DIFF_HEAD_DIM = 128
DIFF_V = 2 * DIFF_HEAD_DIM
DIFF_SCALE = 1.0 / math.sqrt(DIFF_HEAD_DIM)
DIFF_QSCALE = DIFF_SCALE * LOG2E
DIFF_QK_COLS = 2 * DIFF_HEADS * DIFF_HEAD_DIM
DIFF_V_COLS = DIFF_HEADS * DIFF_V
D_FF = -(-8 * D_MODEL // (3 * 256)) * 256
LAMBDA_INIT = 0.8 - 0.6 * math.exp(-0.3 * 0)

LANES = 128
MLA_HEAD_PAD = 2 * LANES
Z_LAT = 1536
Z_KPE = MLA_Q_LORA + MLA_KV_LORA
Z_LAT_USED = Z_KPE + MLA_ROPE
NEG = -1e30
VMEM_LIMIT = 56 * 1024 * 1024


def _cparams(sem):
    return pltpu.CompilerParams(dimension_semantics=sem, vmem_limit_bytes=VMEM_LIMIT)


def _cast_kernel(w_ref, o_ref):
    o_ref[...] = w_ref[...].astype(o_ref.dtype)


def _cast_rows_bf16(w, row0, n_rows, tr):
    c = w.shape[1]
    assert row0 % tr == 0 and n_rows % tr == 0 and row0 + n_rows <= w.shape[0]
    blk0 = row0 // tr
    return pl.pallas_call(
        _cast_kernel,
        grid=(n_rows // tr,),
        in_specs=[pl.BlockSpec((tr, c), lambda i: (i + blk0, 0))],
        out_specs=pl.BlockSpec((tr, c), lambda i: (i, 0)),
        out_shape=jax.ShapeDtypeStruct((n_rows, c), BF16),
        compiler_params=_cparams(("parallel",)),
        name="cast_bf16",
    )(w)


def _rmsnorm_kernel(x_ref, g_ref, o_ref):
    x = x_ref[...]
    ms = jnp.mean(x * x, axis=-1, keepdims=True)
    o_ref[...] = (x * lax.rsqrt(ms + EPS) * g_ref[...]).astype(o_ref.dtype)


def _rmsnorm(x, g, tm):
    m, d = x.shape
    return pl.pallas_call(
        _rmsnorm_kernel,
        grid=(m // tm,),
        in_specs=[pl.BlockSpec((tm, d), lambda i: (i, 0)),
                  pl.BlockSpec((1, d), lambda i: (0, 0))],
        out_specs=pl.BlockSpec((tm, d), lambda i: (i, 0)),
        out_shape=jax.ShapeDtypeStruct((m, d), BF16),
        compiler_params=_cparams(("parallel",)),
        name="rmsnorm",
    )(x, g.reshape(1, d))


def _mm_nt_kernel(a_ref, bt_ref, o_ref):
    o_ref[...] = lax.dot_general(a_ref[...], bt_ref[...], (((1,), (1,)), ((), ())),
                                 preferred_element_type=F32).astype(o_ref.dtype)


def _mm_nt_ride_kernel(a_ref, bt_ref, w_ref, o_ref, w_o_ref):
    _mm_nt_kernel(a_ref, bt_ref, o_ref)
    w_o_ref[...] = w_ref[...].astype(w_o_ref.dtype)


def _matmul_nt(a, bt, tm, tn, out_dtype, name, ride=None):
    m, k = a.shape
    n = bt.shape[0]
    assert bt.shape[1] == k and n % tn == 0
    ni, nj = m // tm, n // tn
    in_specs = [pl.BlockSpec((tm, k), lambda i, j: (i, 0)),
                pl.BlockSpec((tn, k), lambda i, j: (j, 0))]
    out_spec = pl.BlockSpec((tm, tn), lambda i, j: (i, j))
    out_shape = jax.ShapeDtypeStruct((m, n), out_dtype)
    if ride is None:
        return pl.pallas_call(
            _mm_nt_kernel, grid=(ni, nj), in_specs=in_specs, out_specs=out_spec,
            out_shape=out_shape, compiler_params=_cparams(("parallel", "arbitrary")),
            name=name)(a, bt)
    rr, rc = ride.shape
    assert rr % (ni * nj) == 0
    slab = pl.BlockSpec((rr // (ni * nj), rc), lambda i, j: (i * nj + j, 0))
    return pl.pallas_call(
        _mm_nt_ride_kernel, grid=(ni, nj), in_specs=in_specs + [slab],
        out_specs=[out_spec, slab],
        out_shape=[out_shape, jax.ShapeDtypeStruct((rr, rc), BF16)],
        compiler_params=_cparams(("arbitrary", "arbitrary")),
        name=name)(a, bt, ride)


def _latent_norm(c, g):
    ms = jnp.mean(c * c, axis=-1, keepdims=True)
    return (c * lax.rsqrt(ms + EPS) * g).astype(BF16)


def _mla_rope(y, cos, sa, sb):
    return y * cos + pltpu.roll(y, 96, 1) * sa + pltpu.roll(y, 32, 1) * sb


def _q_prep_kernel(z_ref, gl_ref, w_ref, ga_ref, gb_ref, cos_ref, sa_ref, sb_ref, o_ref):
    c = z_ref[:, :MLA_Q_LORA].astype(F32)
    lat = _latent_norm(c, gl_ref[...])
    q = jnp.dot(lat, w_ref[...], preferred_element_type=F32)
    cos, sa, sb = cos_ref[...], sa_ref[...], sb_ref[...]
    ga, gb = ga_ref[...], gb_ref[...]
    for h in range(MLA_HEADS):
        lo = h * MLA_HEAD_PAD
        qa = q[:, lo:lo + LANES]
        qb = q[:, lo + LANES:lo + 2 * LANES]
        ss = jnp.sum(qa * qa + qb * qb, axis=-1, keepdims=True)
        r = lax.rsqrt(ss * (1.0 / MLA_QK) + EPS)
        ya = qa * r * ga
        yb = _mla_rope(qb * r * gb, cos, sa, sb)
        o_ref[:, lo:lo + LANES] = (ya * MLA_QSCALE).astype(o_ref.dtype)
        o_ref[:, lo + LANES:lo + 2 * LANES] = (yb * MLA_QSCALE).astype(o_ref.dtype)


def _kv_prep_kernel(z_ref, gl_ref, wk_ref, wvt_ref, ga_ref, gb_ref, cos_ref, sa_ref, sb_ref,
                    k_ref, vt_ref):
    c = z_ref[:, MLA_Q_LORA:Z_KPE].astype(F32)
    lat = _latent_norm(c, gl_ref[...])
    kn = jnp.dot(lat, wk_ref[...], preferred_element_type=F32)
    vt_ref[...] = lax.dot_general(wvt_ref[...], lat, (((1,), (1,)), ((), ())),
                                  preferred_element_type=F32).astype(vt_ref.dtype)
    lane = lax.broadcasted_iota(jnp.int32, (1, LANES), 1)
    kpe = jnp.where(lane < MLA_ROPE, z_ref[:, Z_KPE:Z_KPE + LANES].astype(F32), 0.0)
    ss_pe = jnp.sum(kpe * kpe, axis=-1, keepdims=True)
    pe_rot = _mla_rope(kpe * gb_ref[...], cos_ref[...], sa_ref[...], sb_ref[...])
    ga = ga_ref[...]
    for h in range(MLA_HEADS):
        a = kn[:, h * LANES:(h + 1) * LANES]
        ss = jnp.sum(a * a, axis=-1, keepdims=True) + ss_pe
        r = lax.rsqrt(ss * (1.0 / MLA_QK) + EPS)
        lo = h * MLA_HEAD_PAD
        k_ref[:, lo:lo + LANES] = (a * r * ga).astype(k_ref.dtype)
        k_ref[:, lo + LANES:lo + 2 * LANES] = (pe_rot * r).astype(k_ref.dtype)


def _row_spec(tm, n, col=0):
    return pl.BlockSpec((tm, n), lambda i: (i, col))


def _const_spec(shape):
    return pl.BlockSpec(shape, lambda i: (0, 0))


def _q_prep(z, gl, w, ga, gb, cos, sa, sb, tm):
    m = z.shape[0]
    n = MLA_HEADS * MLA_HEAD_PAD
    return pl.pallas_call(
        _q_prep_kernel,
        grid=(m // tm,),
        in_specs=[_row_spec(tm, Z_LAT), _const_spec(gl.shape), _const_spec(w.shape),
                  _const_spec(ga.shape), _const_spec(gb.shape),
                  _row_spec(tm, LANES), _row_spec(tm, LANES), _row_spec(tm, LANES)],
        out_specs=_row_spec(tm, n),
        out_shape=jax.ShapeDtypeStruct((m, n), BF16),
        compiler_params=_cparams(("parallel",)),
        name="mla_q_prep",
    )(z, gl, w, ga, gb, cos, sa, sb)


def _kv_prep(z, gl, wk, wvt, ga, gb, cos, sa, sb, tm):
    m = z.shape[0]
    nk = MLA_HEADS * MLA_HEAD_PAD
    nv = MLA_HEADS * MLA_V
    return pl.pallas_call(
        _kv_prep_kernel,
        grid=(m // tm,),
        in_specs=[_row_spec(tm, Z_LAT), _const_spec(gl.shape), _const_spec(wk.shape),
                  _const_spec(wvt.shape), _const_spec(ga.shape), _const_spec(gb.shape),
                  _row_spec(tm, LANES), _row_spec(tm, LANES), _row_spec(tm, LANES)],
        out_specs=[_row_spec(tm, nk), pl.BlockSpec((nv, tm), lambda i: (0, i))],
        out_shape=[jax.ShapeDtypeStruct((m, nk), BF16), jax.ShapeDtypeStruct((nv, m), BF16)],
        compiler_params=_cparams(("parallel",)),
        name="mla_kv_prep",
    )(z, gl, wk, wvt, ga, gb, cos, sa, sb)


DIFF_PREP_COLS = 1024


def _diff_prep_kernel(x_ref, g_ref, cos_ref, sin_ref, o_ref):
    cos, sin = cos_ref[...], sin_ref[...]
    for c in range(DIFF_PREP_COLS // LANES):
        sl = slice(c * LANES, (c + 1) * LANES)
        x = x_ref[:, sl].astype(F32)
        ms = jnp.mean(x * x, axis=-1, keepdims=True)
        y = x * lax.rsqrt(ms + EPS) * g_ref[:, sl]
        o_ref[:, sl] = (y * cos + pltpu.roll(y, 64, 1) * sin).astype(o_ref.dtype)


def _diff_prep(z, g, cos, sin, tm):
    m = z.shape[0]
    n = 2 * DIFF_QK_COLS
    nb = n // DIFF_PREP_COLS
    return pl.pallas_call(
        _diff_prep_kernel,
        grid=(m // tm, nb),
        in_specs=[pl.BlockSpec((tm, DIFF_PREP_COLS), lambda i, j: (i, j)),
                  pl.BlockSpec((1, DIFF_PREP_COLS), lambda i, j: (0, j)),
                  pl.BlockSpec((tm, LANES), lambda i, j: (i, 0)),
                  pl.BlockSpec((tm, LANES), lambda i, j: (i, 0))],
        out_specs=pl.BlockSpec((tm, DIFF_PREP_COLS), lambda i, j: (i, j)),
        out_shape=jax.ShapeDtypeStruct((m, n), BF16),
        compiler_params=_cparams(("parallel", "arbitrary")),
        name="diff_qk_prep",
    )(z, g, cos, sin)


def _scores_t(k, q):
    return lax.dot_general(k, q, (((1,), (1,)), ((), ())), preferred_element_type=F32)


class _Buf(NamedTuple):
    s: Any
    bmax: Any

    def cols(self, sl):
        return _Buf(self.s.at[:, sl], self.bmax.at[:, sl])


def _scores_into(buf, k, q):
    s = _scores_t(k, q)
    buf.s[...] = s
    buf.bmax[...] = jnp.max(s, axis=0, keepdims=True)


def _mask_diag_block(buf, c0, tk):
    upper = lax.broadcasted_iota(jnp.int32, (CHUNK, LANES), 1) >= CHUNK
    for g in range(tk // LANES):
        cols = pl.ds(c0 + g * LANES, LANES)
        r0 = g * LANES + CHUNK
        edge = jnp.where(upper, buf.s[r0:r0 + CHUNK, cols], NEG)
        buf.s[r0:r0 + CHUNK, cols] = edge
        if r0 + CHUNK < tk:
            buf.s[r0 + CHUNK:tk, cols] = jnp.full((tk - r0 - CHUNK, LANES), NEG, F32)
        buf.bmax[:, cols] = jnp.maximum(jnp.max(buf.s[0:r0, cols], axis=0, keepdims=True),
                                        jnp.max(edge, axis=0, keepdims=True))


ONES_ROWS = 16


def _online_update_t(buf, v, m_sc, l_sc, acc_sc):
    tk, tq = buf.s.shape
    m_prev = m_sc[...]
    m_new = jnp.maximum(m_prev, buf.bmax[...])
    alpha = jnp.exp2(m_prev - m_new)
    p_t = jnp.exp2(buf.s[...] - m_new)
    if l_sc is None:
        v_aug = jnp.concatenate([v, jnp.ones((ONES_ROWS, tk), BF16)], axis=0)
        pv = jnp.dot(v_aug, p_t.astype(BF16), preferred_element_type=F32)
    else:
        l_sc[...] = alpha * l_sc[...] + jnp.sum(p_t, axis=0, keepdims=True)
        pv = lax.dot_general(v, p_t.astype(BF16), (((0,), (0,)), ((), ())),
                             preferred_element_type=F32)
    acc_sc[...] = alpha * acc_sc[...] + pv
    m_sc[...] = m_new


def _init_state(m_sc, l_sc, acc_sc):
    m_sc[...] = jnp.full(m_sc.shape, NEG, F32)
    if l_sc is not None:
        l_sc[...] = jnp.zeros(l_sc.shape, F32)
    acc_sc[...] = jnp.zeros(acc_sc.shape, F32)


class _Chain(NamedTuple):
    q: Any
    k_blk: Any
    v_blk: Any
    m: Any
    l: Any
    acc: Any
    buf_a: Any
    buf_b: Any


def _flash_pairs(i, tq, tk, chains):
    assert tq == 2 * tk

    def k_start(j):
        return pl.multiple_of(j * tk, tk)

    def qk(use_b, j):
        k0 = k_start(j)
        for c in chains:
            _scores_into(c.buf_b if use_b else c.buf_a, c.k_blk(k0), c.q)

    def update(use_b, j):
        k0 = k_start(j)
        for c in chains:
            _online_update_t(c.buf_b if use_b else c.buf_a, c.v_blk(k0), c.m, c.l, c.acc)

    for c in chains:
        _init_state(c.m, c.l, c.acc)
    qk(False, 0)

    def body(p, carry):
        qk(True, 2 * p + 1)
        update(False, 2 * p)
        qk(False, 2 * p + 2)
        update(True, 2 * p + 1)
        return carry

    lax.fori_loop(0, i, body, 0)
    half = pl.ds(tk, tk)
    k_last = k_start(2 * i + 1)
    for c in chains:
        _mask_diag_block(c.buf_a, 0, tk)
        _scores_into(c.buf_b.cols(half), c.k_blk(k_last), c.q[tk:, :])
        _mask_diag_block(c.buf_b, tk, tk)
    update(False, 2 * i)
    for c in chains:
        _online_update_t(c.buf_b.cols(half), c.v_blk(k_last), c.m.at[:, half],
                         None if c.l is None else c.l.at[:, half], c.acc.at[:, half])


def _score_scratch(n, tq, tk):
    one = [pltpu.VMEM((n, tk, tq), F32), pltpu.VMEM((n, 1, tq), F32)]
    return one + one


def _bufs(refs, idx):
    sa, ma, sb, mb = refs
    return _Buf(sa.at[idx], ma.at[idx]), _Buf(sb.at[idx], mb.at[idx])


MLA_HEADS_PER_STEP = 2


def _mla_attn_kernel(q_ref, k_ref, vt_ref, o_ref, m_sc, acc_sc, *score_refs, tq, tk):
    nh = m_sc.shape[0]
    chains = []
    for h in range(nh):
        qk_cols = slice(h * MLA_HEAD_PAD, (h + 1) * MLA_HEAD_PAD)
        v_rows = slice(h * MLA_V, (h + 1) * MLA_V)
        buf_a, buf_b = _bufs(score_refs, h)
        chains.append(_Chain(
            q=q_ref[:, qk_cols],
            k_blk=functools.partial(lambda k0, cols: k_ref[pl.ds(k0, tk), cols], cols=qk_cols),
            v_blk=functools.partial(lambda k0, rows: vt_ref[rows, pl.ds(k0, tk)], rows=v_rows),
            m=m_sc.at[h], l=None, acc=acc_sc.at[h], buf_a=buf_a, buf_b=buf_b))
    _flash_pairs(pl.program_id(1), tq, tk, chains)
    for h in range(nh):
        acc = acc_sc[h]
        o = acc[:MLA_V] / acc[MLA_V:MLA_V + 1]
        o_ref[:, h * MLA_V:(h + 1) * MLA_V] = o.T.astype(o_ref.dtype)


def _mla_attn(q, k, vt, tq, tk):
    s = q.shape[0]
    nh = MLA_HEADS_PER_STEP
    kern = functools.partial(_mla_attn_kernel, tq=tq, tk=tk)
    return pl.pallas_call(
        kern,
        grid=(MLA_HEADS // nh, s // tq),
        in_specs=[pl.BlockSpec((tq, nh * MLA_HEAD_PAD), lambda h, i: (i, h)),
                  pl.BlockSpec((s, nh * MLA_HEAD_PAD), lambda h, i: (0, h)),
                  pl.BlockSpec((nh * MLA_V, s), lambda h, i: (h, 0))],
        out_specs=pl.BlockSpec((tq, nh * MLA_V), lambda h, i: (i, h)),
        out_shape=jax.ShapeDtypeStruct((s, MLA_HEADS * MLA_V), BF16),
        scratch_shapes=[pltpu.VMEM((nh, 1, tq), F32),
                        pltpu.VMEM((nh, MLA_V + ONES_ROWS, tq), F32)]
        + _score_scratch(nh, tq, tk),
        compiler_params=_cparams(("parallel", "arbitrary")),
        name="mla_attn",
    )(q, k, vt)


def _diff_attn_kernel(q_ref, k_ref, v_ref, lq1_ref, lk1_ref, lq2_ref, lk2_ref, g_ref, o_ref,
                      m_sc, l_sc, acc_sc, *score_refs, tq, tk):
    chains = []
    for t in range(2):
        cols = slice(t * DIFF_HEAD_DIM, (t + 1) * DIFF_HEAD_DIM)
        buf_a, buf_b = _bufs(score_refs, t)
        chains.append(_Chain(
            q=q_ref[:, cols],
            k_blk=functools.partial(lambda k0, cols: k_ref[pl.ds(k0, tk), cols], cols=cols),
            v_blk=lambda k0: v_ref[pl.ds(k0, tk), :],
            m=m_sc.at[t], l=l_sc.at[t], acc=acc_sc.at[t], buf_a=buf_a, buf_b=buf_b))
    _flash_pairs(pl.program_id(1), tq, tk, chains)

    lam = (jnp.exp(jnp.sum(lq1_ref[...] * lk1_ref[...], axis=-1, keepdims=True))
           - jnp.exp(jnp.sum(lq2_ref[...] * lk2_ref[...], axis=-1, keepdims=True))
           + LAMBDA_INIT)
    o = (acc_sc[0] / l_sc[0] - lam * (acc_sc[1] / l_sc[1])).T
    ms = jnp.mean(o * o, axis=-1, keepdims=True)
    o = o * lax.rsqrt(ms + EPS) * g_ref[...]
    o_ref[...] = (o * (1.0 - LAMBDA_INIT)).astype(o_ref.dtype)


def _diff_attn(qk, z, lq1, lk1, lq2, lk2, g, tq, tk):
    s = qk.shape[0]
    kern = functools.partial(_diff_attn_kernel, tq=tq, tk=tk)
    k_off = DIFF_QK_COLS // DIFF_V
    v_off = 2 * DIFF_QK_COLS // DIFF_V
    vec = pl.BlockSpec((1, DIFF_HEAD_DIM), lambda h, i: (0, 0))
    return pl.pallas_call(
        kern,
        grid=(DIFF_HEADS, s // tq),
        in_specs=[pl.BlockSpec((tq, DIFF_V), lambda h, i: (i, h)),
                  pl.BlockSpec((s, DIFF_V), lambda h, i: (0, h + k_off)),
                  pl.BlockSpec((s, DIFF_V), lambda h, i: (0, h + v_off)),
                  vec, vec, vec, vec,
                  pl.BlockSpec((1, DIFF_V), lambda h, i: (0, 0))],
        out_specs=pl.BlockSpec((tq, DIFF_V), lambda h, i: (i, h)),
        out_shape=jax.ShapeDtypeStruct((s, DIFF_HEADS * DIFF_V), BF16),
        scratch_shapes=[pltpu.VMEM((2, 1, tq), F32), pltpu.VMEM((2, 1, tq), F32),
                        pltpu.VMEM((2, DIFF_V, tq), F32)] + _score_scratch(2, tq, tk),
        compiler_params=_cparams(("parallel", "arbitrary")),
        name="diff_attn",
    )(qk, qk, z, lq1, lk1, lq2, lk2, g)


def _wo_kernel(oa_ref, ob_ref, wa_ref, wb_ref, x_ref, h_ref):
    acc = jnp.dot(oa_ref[...], wa_ref[...], preferred_element_type=F32)
    acc = acc + jnp.dot(ob_ref[...], wb_ref[...], preferred_element_type=F32)
    h_ref[...] = x_ref[...] + acc


def _wo_proj(oa, ob, w, x, tm, tn):
    m, ka = oa.shape
    kb = ob.shape[1]
    assert ka == kb and w.shape[0] == ka + kb
    n = w.shape[1]
    return pl.pallas_call(
        _wo_kernel,
        grid=(m // tm, n // tn),
        in_specs=[pl.BlockSpec((tm, ka), lambda i, j: (i, 0)),
                  pl.BlockSpec((tm, kb), lambda i, j: (i, 0)),
                  pl.BlockSpec((ka, tn), lambda i, j: (0, j)),
                  pl.BlockSpec((kb, tn), lambda i, j: (1, j)),
                  pl.BlockSpec((tm, tn), lambda i, j: (i, j))],
        out_specs=pl.BlockSpec((tm, tn), lambda i, j: (i, j)),
        out_shape=jax.ShapeDtypeStruct((m, n), F32),
        compiler_params=_cparams(("parallel", "arbitrary")),
        name="wo_proj",
    )(oa, ob, w, w, x)


def _gate_up_kernel(a_ref, wg_ref, wu_ref, wd_ref, o_ref, wd_o_ref):
    a = a_ref[...]
    g = jnp.dot(a, wg_ref[...].astype(BF16), preferred_element_type=F32)
    u = jnp.dot(a, wu_ref[...].astype(BF16), preferred_element_type=F32)
    o_ref[...] = (g * (1.0 / (1.0 + jnp.exp(-g))) * u).astype(o_ref.dtype)
    wd_o_ref[...] = wd_ref[...].astype(wd_o_ref.dtype)


def _gate_up(a, wg, wu, wd, tm, tn):
    m, k = a.shape
    n = wg.shape[1]
    ni, nj = m // tm, n // tn
    kd, nd = wd.shape
    assert kd % (ni * nj) == 0
    slab = kd // (ni * nj)
    return pl.pallas_call(
        _gate_up_kernel,
        grid=(ni, nj),
        in_specs=[pl.BlockSpec((tm, k), lambda i, j: (i, 0), pipeline_mode=pl.Buffered(1)),
                  pl.BlockSpec((k, tn), lambda i, j: (0, j)),
                  pl.BlockSpec((k, tn), lambda i, j: (0, j)),
                  pl.BlockSpec((slab, nd), lambda i, j: (i * nj + j, 0))],
        out_specs=[pl.BlockSpec((tm, tn), lambda i, j: (i, j)),
                   pl.BlockSpec((slab, nd), lambda i, j: (i * nj + j, 0))],
        out_shape=[jax.ShapeDtypeStruct((m, n), BF16), jax.ShapeDtypeStruct((kd, nd), BF16)],
        compiler_params=_cparams(("arbitrary", "arbitrary")),
        name="ffn_gate_up",
    )(a, wg, wu, wd)


def _down_kernel(a_ref, w_ref, h_ref, o_ref):
    o_ref[...] = h_ref[...] + jnp.dot(a_ref[...], w_ref[...], preferred_element_type=F32)


def _down_proj(a, w, h, tm, tn):
    m, k = a.shape
    n = w.shape[1]
    return pl.pallas_call(
        _down_kernel,
        grid=(m // tm, n // tn),
        in_specs=[pl.BlockSpec((tm, k), lambda i, j: (i, 0)),
                  pl.BlockSpec((k, tn), lambda i, j: (0, j)),
                  pl.BlockSpec((tm, tn), lambda i, j: (i, j))],
        out_specs=pl.BlockSpec((tm, tn), lambda i, j: (i, j)),
        out_shape=jax.ShapeDtypeStruct((m, n), F32),
        compiler_params=_cparams(("parallel", "arbitrary")),
        name="ffn_down",
    )(a, w, h)


def _rope_tables(s):
    pos = np.arange(s, dtype=np.float64)

    def cs(half):
        inv_freq = ROPE_THETA ** (-np.arange(half, dtype=np.float64) / half)
        ang = pos[:, None] * inv_freq[None, :]
        return np.cos(ang), np.sin(ang)

    c, sn = cs(MLA_ROPE // 2)
    z32 = np.zeros_like(c)
    z64 = np.zeros((s, 64))
    mla_cos = np.concatenate([c, c, z64], axis=-1)
    mla_sa = np.concatenate([-sn, z32, z64], axis=-1)
    mla_sb = np.concatenate([z32, sn, z64], axis=-1)
    c, sn = cs(DIFF_HEAD_DIM // 2)
    diff_cos = np.concatenate([c, c], axis=-1)
    diff_sin = np.concatenate([-sn, sn], axis=-1)
    return tuple(jnp.asarray(t.astype(np.float32))
                 for t in (mla_cos, mla_sa, mla_sb, diff_cos, diff_sin))


def _pad_gain(g, lo, hi, width):
    return jnp.pad(g[lo:hi], (0, width - (hi - lo))).reshape(1, width)


def kernel(x, attn_norm_g, w_in, q_latent_norm_g, kv_latent_norm_g, w_uq, w_ukv, mla_q_norm_g, mla_k_norm_g, diff_q_norm_g, diff_k_norm_g, lambda_q1, lambda_k1, lambda_q2, lambda_k2, diff_subln_g, w_o, ffn_norm_g, w_gate, w_up, w_down):
    b, s, d = x.shape
    assert b == 1 and d == D_MODEL and s % 512 == 0
    x2 = x.reshape(s, d)
    tm = min(1024, s)
    tp = min(512, s)
    tq = min(1024, s)
    tk = tq // 2

    w_in_t = jnp.swapaxes(w_in[0], 0, 1)
    n_diff = 2 * DIFF_QK_COLS + DIFF_V_COLS
    w_lat_t = _cast_rows_bf16(w_in_t, 0, Z_LAT, Z_LAT // 2)
    w_diff_t = _cast_rows_bf16(w_in_t, Z_LAT_USED, n_diff, math.gcd(Z_LAT_USED, n_diff))
    w_uq_p = jnp.pad(w_uq[0].reshape(MLA_Q_LORA, MLA_HEADS, MLA_QK),
                     ((0, 0), (0, 0), (0, MLA_HEAD_PAD - MLA_QK))
                     ).reshape(MLA_Q_LORA, MLA_HEADS * MLA_HEAD_PAD).astype(BF16)
    w_ukv3 = w_ukv[0].reshape(MLA_KV_LORA, MLA_HEADS, MLA_NOPE + MLA_V)
    w_uk = w_ukv3[:, :, :MLA_NOPE].reshape(MLA_KV_LORA, MLA_HEADS * MLA_NOPE).astype(BF16)
    w_uv_t = w_ukv3[:, :, MLA_NOPE:].reshape(MLA_KV_LORA, MLA_HEADS * MLA_V).T.astype(BF16)

    mla_cos, mla_sa, mla_sb, diff_cos, diff_sin = _rope_tables(s)
    gq_a = _pad_gain(mla_q_norm_g[0], 0, MLA_NOPE, LANES)
    gq_b = _pad_gain(mla_q_norm_g[0], MLA_NOPE, MLA_QK, LANES)
    gk_a = _pad_gain(mla_k_norm_g[0], 0, MLA_NOPE, LANES)
    gk_b = _pad_gain(mla_k_norm_g[0], MLA_NOPE, MLA_QK, LANES)
    g_diff = jnp.concatenate([jnp.tile(diff_q_norm_g[0] * DIFF_QSCALE, 2 * DIFF_HEADS),
                              jnp.tile(diff_k_norm_g[0], 2 * DIFF_HEADS)]).reshape(1, -1)

    n = _rmsnorm(x2, attn_norm_g[0], min(512, s))
    z_lat, w_o_b = _matmul_nt(n, w_lat_t, tm, Z_LAT // 2, BF16, "in_proj_lat",
                              ride=w_o[0])
    z_diff = _matmul_nt(n, w_diff_t, tm, 1024, BF16, "in_proj_diff")
    q_a = _q_prep(z_lat, q_latent_norm_g[0].reshape(1, -1), w_uq_p, gq_a, gq_b,
                  mla_cos, mla_sa, mla_sb, tp)
    k_a, vt_a = _kv_prep(z_lat, kv_latent_norm_g[0].reshape(1, -1), w_uk, w_uv_t, gk_a, gk_b,
                        mla_cos, mla_sa, mla_sb, tp)
    o_a = _mla_attn(q_a, k_a, vt_a, tq, tk)
    qk_b = _diff_prep(z_diff, g_diff, diff_cos, diff_sin, tm)
    o_b = _diff_attn(qk_b, z_diff, lambda_q1[0].reshape(1, -1), lambda_k1[0].reshape(1, -1),
                     lambda_q2[0].reshape(1, -1), lambda_k2[0].reshape(1, -1),
                     diff_subln_g[0].reshape(1, -1), tq, tk)
    h = _wo_proj(o_a, o_b, w_o_b, x2, tm, 1024)

    m = _rmsnorm(h, ffn_norm_g[0], min(512, s))
    a, w_down_b = _gate_up(m, w_gate[0], w_up[0], w_down[0], min(2048, s), 256)
    out = _down_proj(a, w_down_b, h, min(512, s), 512)
    return out.reshape(b, s, d)
```

```python
import functools
import math
from typing import Any, NamedTuple

import jax
import jax.numpy as jnp
import numpy as np
from jax import lax
from jax.experimental import pallas as pl
from jax.experimental.pallas import tpu as pltpu

F32 = jnp.float32
BF16 = jnp.bfloat16

D_MODEL = 4096
CHUNK = 64
ROPE_THETA = 10000.0
EPS = 1e-6
MLA_HEADS = 16
MLA_Q_LORA = 768
MLA_KV_LORA = 512
MLA_NOPE = 128
MLA_ROPE = 64
MLA_QK = MLA_NOPE + MLA_ROPE
MLA_V = 128
LOG2E = math.log2(math.e)
MLA_SCALE = 1.0 / math.sqrt(MLA_QK)
MLA_QSCALE = MLA_SCALE * LOG2E
DIFF_HEADS = 8
DIFF_HEAD_DIM = 128
DIFF_V = 2 * DIFF_HEAD_DIM
DIFF_SCALE = 1.0 / math.sqrt(DIFF_HEAD_DIM)
DIFF_QSCALE = DIFF_SCALE * LOG2E
DIFF_QK_COLS = 2 * DIFF_HEADS * DIFF_HEAD_DIM
DIFF_V_COLS = DIFF_HEADS * DIFF_V
D_FF = -(-8 * D_MODEL // (3 * 256)) * 256
LAMBDA_INIT = 0.8 - 0.6 * math.exp(-0.3 * 0)

LANES = 128
MLA_HEAD_PAD = 2 * LANES
Z_LAT = 1536
Z_KPE = MLA_Q_LORA + MLA_KV_LORA
Z_LAT_USED = Z_KPE + MLA_ROPE
NEG = -1e30
VMEM_LIMIT = 56 * 1024 * 1024


def _cparams(sem):
    return pltpu.CompilerParams(dimension_semantics=sem, vmem_limit_bytes=VMEM_LIMIT)


def _cast_kernel(w_ref, o_ref):
    o_ref[...] = w_ref[...].astype(o_ref.dtype)


def _cast_rows_bf16(w, row0, n_rows, tr):
    c = w.shape[1]
    assert row0 % tr == 0 and n_rows % tr == 0 and row0 + n_rows <= w.shape[0]
    blk0 = row0 // tr
    return pl.pallas_call(
        _cast_kernel,
        grid=(n_rows // tr,),
        in_specs=[pl.BlockSpec((tr, c), lambda i: (i + blk0, 0))],
        out_specs=pl.BlockSpec((tr, c), lambda i: (i, 0)),
        out_shape=jax.ShapeDtypeStruct((n_rows, c), BF16),
        compiler_params=_cparams(("parallel",)),
        name="cast_bf16",
    )(w)


def _rmsnorm_kernel(x_ref, g_ref, o_ref):
    x = x_ref[...]
    ms = jnp.mean(x * x, axis=-1, keepdims=True)
    o_ref[...] = (x * lax.rsqrt(ms + EPS) * g_ref[...]).astype(o_ref.dtype)


def _rmsnorm(x, g, tm):
    m, d = x.shape
    return pl.pallas_call(
        _rmsnorm_kernel,
        grid=(m // tm,),
        in_specs=[pl.BlockSpec((tm, d), lambda i: (i, 0)),
                  pl.BlockSpec((1, d), lambda i: (0, 0))],
        out_specs=pl.BlockSpec((tm, d), lambda i: (i, 0)),
        out_shape=jax.ShapeDtypeStruct((m, d), BF16),
        compiler_params=_cparams(("parallel",)),
        name="rmsnorm",
    )(x, g.reshape(1, d))


def _mm_nt_kernel(a_ref, bt_ref, o_ref):
    o_ref[...] = lax.dot_general(a_ref[...], bt_ref[...], (((1,), (1,)), ((), ())),
                                 preferred_element_type=F32).astype(o_ref.dtype)


def _mm_nt_ride_kernel(a_ref, bt_ref, w_ref, o_ref, w_o_ref):
    _mm_nt_kernel(a_ref, bt_ref, o_ref)
    w_o_ref[...] = w_ref[...].astype(w_o_ref.dtype)


def _matmul_nt(a, bt, tm, tn, out_dtype, name, ride=None, n=None):
    m, k = a.shape
    n = bt.shape[0] if n is None else n
    assert bt.shape[1] == k and n % tn == 0 and n <= bt.shape[0]
    ni, nj = m // tm, n // tn
    in_specs = [pl.BlockSpec((tm, k), lambda i, j: (i, 0)),
                pl.BlockSpec((tn, k), lambda i, j: (j, 0))]
    out_spec = pl.BlockSpec((tm, tn), lambda i, j: (i, j))
    out_shape = jax.ShapeDtypeStruct((m, n), out_dtype)
    if ride is None:
        return pl.pallas_call(
            _mm_nt_kernel, grid=(ni, nj), in_specs=in_specs, out_specs=out_spec,
            out_shape=out_shape, compiler_params=_cparams(("parallel", "arbitrary")),
            name=name)(a, bt)
    rr, rc = ride.shape
    assert rr % (ni * nj) == 0
    slab = pl.BlockSpec((rr // (ni * nj), rc), lambda i, j: (i * nj + j, 0))
    return pl.pallas_call(
        _mm_nt_ride_kernel, grid=(ni, nj), in_specs=in_specs + [slab],
        out_specs=[out_spec, slab],
        out_shape=[out_shape, jax.ShapeDtypeStruct((rr, rc), BF16)],
        compiler_params=_cparams(("arbitrary", "arbitrary")),
        name=name)(a, bt, ride)


def _proj_t(wt, blk, rows, a, tm, name):
    m, k = a.shape
    return pl.pallas_call(
        _mm_nt_kernel,
        grid=(m // tm,),
        in_specs=[pl.BlockSpec((rows, k), lambda i: (blk, 0), pipeline_mode=pl.Buffered(1)),
                  pl.BlockSpec((tm, k), lambda i: (i, 0))],
        out_specs=pl.BlockSpec((rows, tm), lambda i: (0, i)),
        out_shape=jax.ShapeDtypeStruct((rows, m), BF16),
        compiler_params=_cparams(("parallel",)),
        name=name,
    )(wt, a)


def _latent_norm(c, g):
    ms = jnp.mean(c * c, axis=-1, keepdims=True)
    return (c * lax.rsqrt(ms + EPS) * g).astype(BF16)


def _mla_rope(y, cos, sa, sb):
    return y * cos + pltpu.roll(y, 96, 1) * sa + pltpu.roll(y, 32, 1) * sb


def _q_prep_kernel(z_ref, gl_ref, w_ref, ga_ref, gb_ref, cos_ref, sa_ref, sb_ref, o_ref):
    ga, gb = ga_ref[...], gb_ref[...]
    half = z_ref.shape[0] // 2
    for rows in (slice(0, half), slice(half, 2 * half)):
        c = z_ref[rows, :MLA_Q_LORA].astype(F32)
        lat = _latent_norm(c, gl_ref[...])
        q = jnp.dot(lat, w_ref[...], preferred_element_type=F32)
        cos, sa, sb = cos_ref[rows, :], sa_ref[rows, :], sb_ref[rows, :]
        for h in range(MLA_HEADS):
            lo = h * MLA_HEAD_PAD
            qa = q[:, lo:lo + LANES]
            qb = q[:, lo + LANES:lo + 2 * LANES]
            ss = jnp.sum(qa * qa + qb * qb, axis=-1, keepdims=True)
            r = lax.rsqrt(ss * (1.0 / MLA_QK) + EPS)
            ya = qa * r * ga
            yb = _mla_rope(qb * r * gb, cos, sa, sb)
            o_ref[rows, lo:lo + LANES] = (ya * MLA_QSCALE).astype(o_ref.dtype)
            o_ref[rows, lo + LANES:lo + 2 * LANES] = (yb * MLA_QSCALE).astype(o_ref.dtype)


def _kv_prep_kernel(z_ref, gl_ref, wk_ref, wvt_ref, ga_ref, gb_ref, cos_ref, sa_ref, sb_ref,
                    k_ref, vt_ref):
    c = z_ref[:, MLA_Q_LORA:Z_KPE].astype(F32)
    lat = _latent_norm(c, gl_ref[...])
    kn = jnp.dot(lat, wk_ref[...], preferred_element_type=F32)
    vt_ref[...] = lax.dot_general(wvt_ref[...], lat, (((1,), (1,)), ((), ())),
                                  preferred_element_type=F32).astype(vt_ref.dtype)
    lane = lax.broadcasted_iota(jnp.int32, (1, LANES), 1)
    kpe = jnp.where(lane < MLA_ROPE, z_ref[:, Z_KPE:Z_KPE + LANES].astype(F32), 0.0)
    ss_pe = jnp.sum(kpe * kpe, axis=-1, keepdims=True)
    pe_rot = _mla_rope(kpe * gb_ref[...], cos_ref[...], sa_ref[...], sb_ref[...])
    ga = ga_ref[...]
    for h in range(MLA_HEADS):
        a = kn[:, h * LANES:(h + 1) * LANES]
        ss = jnp.sum(a * a, axis=-1, keepdims=True) + ss_pe
        r = lax.rsqrt(ss * (1.0 / MLA_QK) + EPS)
        lo = h * MLA_HEAD_PAD
        k_ref[:, lo:lo + LANES] = (a * r * ga).astype(k_ref.dtype)
        k_ref[:, lo + LANES:lo + 2 * LANES] = (pe_rot * r).astype(k_ref.dtype)


def _row_spec(tm, n, col=0):
    return pl.BlockSpec((tm, n), lambda i: (i, col))


def _const_spec(shape):
    return pl.BlockSpec(shape, lambda i: (0, 0))


def _q_prep(z, gl, w, ga, gb, cos, sa, sb, tm):
    m = z.shape[0]
    n = MLA_HEADS * MLA_HEAD_PAD
    return pl.pallas_call(
        _q_prep_kernel,
        grid=(m // tm,),
        in_specs=[_row_spec(tm, Z_LAT), _const_spec(gl.shape), _const_spec(w.shape),
                  _const_spec(ga.shape), _const_spec(gb.shape),
                  _row_spec(tm, LANES), _row_spec(tm, LANES), _row_spec(tm, LANES)],
        out_specs=_row_spec(tm, n),
        out_shape=jax.ShapeDtypeStruct((m, n), BF16),
        compiler_params=_cparams(("parallel",)),
        name="mla_q_prep",
    )(z, gl, w, ga, gb, cos, sa, sb)


def _kv_prep(z, gl, wk, wvt, ga, gb, cos, sa, sb, tm):
    m = z.shape[0]
    nk = MLA_HEADS * MLA_HEAD_PAD
    nv = MLA_HEADS * MLA_V
    return pl.pallas_call(
        _kv_prep_kernel,
        grid=(m // tm,),
        in_specs=[_row_spec(tm, Z_LAT), _const_spec(gl.shape), _const_spec(wk.shape),
                  _const_spec(wvt.shape), _const_spec(ga.shape), _const_spec(gb.shape),
                  _row_spec(tm, LANES), _row_spec(tm, LANES), _row_spec(tm, LANES)],
        out_specs=[_row_spec(tm, nk), pl.BlockSpec((nv, tm), lambda i: (0, i))],
        out_shape=[jax.ShapeDtypeStruct((m, nk), BF16), jax.ShapeDtypeStruct((nv, m), BF16)],
        compiler_params=_cparams(("parallel",)),
        name="mla_kv_prep",
    )(z, gl, wk, wvt, ga, gb, cos, sa, sb)


DIFF_PREP_COLS = 1024


def _diff_prep_kernel(x_ref, g_ref, cos_ref, sin_ref, o_ref):
    cos, sin = cos_ref[...], sin_ref[...]
    for c in range(DIFF_PREP_COLS // LANES):
        sl = slice(c * LANES, (c + 1) * LANES)
        x = x_ref[:, sl].astype(F32)
        ms = jnp.mean(x * x, axis=-1, keepdims=True)
        y = x * lax.rsqrt(ms + EPS) * g_ref[:, sl]
        o_ref[:, sl] = (y * cos + pltpu.roll(y, 64, 1) * sin).astype(o_ref.dtype)


def _diff_prep(z, g, cos, sin, tm):
    m = z.shape[0]
    n = 2 * DIFF_QK_COLS
    nb = n // DIFF_PREP_COLS
    return pl.pallas_call(
        _diff_prep_kernel,
        grid=(m // tm, nb),
        in_specs=[pl.BlockSpec((tm, DIFF_PREP_COLS), lambda i, j: (i, j)),
                  pl.BlockSpec((1, DIFF_PREP_COLS), lambda i, j: (0, j)),
                  pl.BlockSpec((tm, LANES), lambda i, j: (i, 0)),
                  pl.BlockSpec((tm, LANES), lambda i, j: (i, 0))],
        out_specs=pl.BlockSpec((tm, DIFF_PREP_COLS), lambda i, j: (i, j)),
        out_shape=jax.ShapeDtypeStruct((m, n), BF16),
        compiler_params=_cparams(("parallel", "arbitrary")),
        name="diff_qk_prep",
    )(z, g, cos, sin)


def _scores_t(k, q):
    return lax.dot_general(k, q, (((1,), (1,)), ((), ())), preferred_element_type=F32)


class _Buf(NamedTuple):
    s: Any
    bmax: Any

    def cols(self, sl):
        return _Buf(self.s.at[:, sl], self.bmax.at[:, sl])


def _scores_into(buf, k, q):
    s = _scores_t(k, q)
    buf.s[...] = s
    buf.bmax[...] = jnp.max(s, axis=0, keepdims=True)


def _mask_diag_block(buf, c0, tk):
    upper = lax.broadcasted_iota(jnp.int32, (CHUNK, LANES), 1) >= CHUNK
    for g in range(tk // LANES):
        cols = pl.ds(c0 + g * LANES, LANES)
        r0 = g * LANES + CHUNK
        edge = jnp.where(upper, buf.s[r0:r0 + CHUNK, cols], NEG)
        buf.s[r0:r0 + CHUNK, cols] = edge
        if r0 + CHUNK < tk:
            buf.s[r0 + CHUNK:tk, cols] = jnp.full((tk - r0 - CHUNK, LANES), NEG, F32)
        buf.bmax[:, cols] = jnp.maximum(jnp.max(buf.s[0:r0, cols], axis=0, keepdims=True),
                                        jnp.max(edge, axis=0, keepdims=True))


ONES_ROWS = 16


def _online_update_t(buf, v, m_sc, l_sc, acc_sc):
    tk, tq = buf.s.shape
    m_prev = m_sc[...]
    m_new = jnp.maximum(m_prev, buf.bmax[...])
    alpha = jnp.exp2(m_prev - m_new)
    p_t = jnp.exp2(buf.s[...] - m_new)
    if l_sc is None:
        v_aug = jnp.concatenate([v, jnp.ones((ONES_ROWS, tk), BF16)], axis=0)
        pv = jnp.dot(v_aug, p_t.astype(BF16), preferred_element_type=F32)
    else:
        l_sc[...] = alpha * l_sc[...] + jnp.sum(p_t, axis=0, keepdims=True)
        pv = lax.dot_general(v, p_t.astype(BF16), (((0,), (0,)), ((), ())),
                             preferred_element_type=F32)
    acc_sc[...] = alpha * acc_sc[...] + pv
    m_sc[...] = m_new


def _init_state(m_sc, l_sc, acc_sc):
    m_sc[...] = jnp.full(m_sc.shape, NEG, F32)
    if l_sc is not None:
        l_sc[...] = jnp.zeros(l_sc.shape, F32)
    acc_sc[...] = jnp.zeros(acc_sc.shape, F32)


class _Chain(NamedTuple):
    q: Any
    k_blk: Any
    v_blk: Any
    m: Any
    l: Any
    acc: Any
    buf_a: Any
    buf_b: Any


def _flash_pairs(i, tq, tk, chains):
    assert tq == 2 * tk

    def k_start(j):
        return pl.multiple_of(j * tk, tk)

    def qk(use_b, j):
        k0 = k_start(j)
        for c in chains:
            _scores_into(c.buf_b if use_b else c.buf_a, c.k_blk(k0), c.q)

    def update(use_b, j):
        k0 = k_start(j)
        for c in chains:
            _online_update_t(c.buf_b if use_b else c.buf_a, c.v_blk(k0), c.m, c.l, c.acc)

    for c in chains:
        _init_state(c.m, c.l, c.acc)
    qk(False, 0)

    def body(p, carry):
        qk(True, 2 * p + 1)
        update(False, 2 * p)
        qk(False, 2 * p + 2)
        update(True, 2 * p + 1)
        return carry

    lax.fori_loop(0, i, body, 0)
    half = pl.ds(tk, tk)
    k_last = k_start(2 * i + 1)
    for c in chains:
        _mask_diag_block(c.buf_a, 0, tk)
        _scores_into(c.buf_b.cols(half), c.k_blk(k_last), c.q[tk:, :])
        _mask_diag_block(c.buf_b, tk, tk)
    update(False, 2 * i)
    for c in chains:
        _online_update_t(c.buf_b.cols(half), c.v_blk(k_last), c.m.at[:, half],
                         None if c.l is None else c.l.at[:, half], c.acc.at[:, half])


def _score_scratch(n, tq, tk):
    one = [pltpu.VMEM((n, tk, tq), F32), pltpu.VMEM((n, 1, tq), F32)]
    return one + one


def _bufs(refs, idx):
    sa, ma, sb, mb = refs
    return _Buf(sa.at[idx], ma.at[idx]), _Buf(sb.at[idx], mb.at[idx])


MLA_HEADS_PER_STEP = 2


def _mla_attn_kernel(q_ref, k_ref, vt_ref, o_ref, m_sc, acc_sc, *score_refs, tq, tk):
    nh = m_sc.shape[0]
    chains = []
    for h in range(nh):
        qk_cols = slice(h * MLA_HEAD_PAD, (h + 1) * MLA_HEAD_PAD)
        v_rows = slice(h * MLA_V, (h + 1) * MLA_V)
        buf_a, buf_b = _bufs(score_refs, h)
        chains.append(_Chain(
            q=q_ref[:, qk_cols],
            k_blk=functools.partial(lambda k0, cols: k_ref[pl.ds(k0, tk), cols], cols=qk_cols),
            v_blk=functools.partial(lambda k0, rows: vt_ref[rows, pl.ds(k0, tk)], rows=v_rows),
            m=m_sc.at[h], l=None, acc=acc_sc.at[h], buf_a=buf_a, buf_b=buf_b))
    _flash_pairs(pl.program_id(1), tq, tk, chains)
    for h in range(nh):
        acc = acc_sc[h]
        o = acc[:MLA_V] / acc[MLA_V:MLA_V + 1]
        o_ref[:, h * MLA_V:(h + 1) * MLA_V] = o.T.astype(o_ref.dtype)


def _mla_attn(q, k, vt, tq, tk):
    s = q.shape[0]
    nh = MLA_HEADS_PER_STEP
    kern = functools.partial(_mla_attn_kernel, tq=tq, tk=tk)
    return pl.pallas_call(
        kern,
        grid=(MLA_HEADS // nh, s // tq),
        in_specs=[pl.BlockSpec((tq, nh * MLA_HEAD_PAD), lambda h, i: (i, h)),
                  pl.BlockSpec((s, nh * MLA_HEAD_PAD), lambda h, i: (0, h)),
                  pl.BlockSpec((nh * MLA_V, s), lambda h, i: (h, 0))],
        out_specs=pl.BlockSpec((tq, nh * MLA_V), lambda h, i: (i, h)),
        out_shape=jax.ShapeDtypeStruct((s, MLA_HEADS * MLA_V), BF16),
        scratch_shapes=[pltpu.VMEM((nh, 1, tq), F32),
                        pltpu.VMEM((nh, MLA_V + ONES_ROWS, tq), F32)]
        + _score_scratch(nh, tq, tk),
        compiler_params=_cparams(("parallel", "arbitrary")),
        name="mla_attn",
    )(q, k, vt)


def _diff_attn_kernel(q_ref, k_ref, vt_ref, lq1_ref, lk1_ref, lq2_ref, lk2_ref, g_ref, o_ref,
                      m_sc, acc_sc, *score_refs, tq, tk):
    chains = []
    for t in range(2):
        cols = slice(t * DIFF_HEAD_DIM, (t + 1) * DIFF_HEAD_DIM)
        buf_a, buf_b = _bufs(score_refs, t)
        chains.append(_Chain(
            q=q_ref[:, cols],
            k_blk=functools.partial(lambda k0, cols: k_ref[pl.ds(k0, tk), cols], cols=cols),
            v_blk=lambda k0: vt_ref[:, pl.ds(k0, tk)],
            m=m_sc.at[t], l=None, acc=acc_sc.at[t], buf_a=buf_a, buf_b=buf_b))
    _flash_pairs(pl.program_id(1), tq, tk, chains)

    lam = (jnp.exp(jnp.sum(lq1_ref[...] * lk1_ref[...], axis=-1, keepdims=True))
           - jnp.exp(jnp.sum(lq2_ref[...] * lk2_ref[...], axis=-1, keepdims=True))
           + LAMBDA_INIT)
    a1, a2 = acc_sc[0], acc_sc[1]
    o = (a1[:DIFF_V] / a1[DIFF_V:DIFF_V + 1]
         - lam * (a2[:DIFF_V] / a2[DIFF_V:DIFF_V + 1])).T
    ms = jnp.mean(o * o, axis=-1, keepdims=True)
    o = o * lax.rsqrt(ms + EPS) * g_ref[...]
    o_ref[...] = (o * (1.0 - LAMBDA_INIT)).astype(o_ref.dtype)


def _diff_attn(qk, vt, lq1, lk1, lq2, lk2, g, tq, tk):
    s = qk.shape[0]
    kern = functools.partial(_diff_attn_kernel, tq=tq, tk=tk)
    k_off = DIFF_QK_COLS // DIFF_V
    vec = pl.BlockSpec((1, DIFF_HEAD_DIM), lambda h, i: (0, 0))
    return pl.pallas_call(
        kern,
        grid=(DIFF_HEADS, s // tq),
        in_specs=[pl.BlockSpec((tq, DIFF_V), lambda h, i: (i, h)),
                  pl.BlockSpec((s, DIFF_V), lambda h, i: (0, h + k_off)),
                  pl.BlockSpec((DIFF_V, s), lambda h, i: (h, 0)),
                  vec, vec, vec, vec,
                  pl.BlockSpec((1, DIFF_V), lambda h, i: (0, 0))],
        out_specs=pl.BlockSpec((tq, DIFF_V), lambda h, i: (i, h)),
        out_shape=jax.ShapeDtypeStruct((s, DIFF_HEADS * DIFF_V), BF16),
        scratch_shapes=[pltpu.VMEM((2, 1, tq), F32),
                        pltpu.VMEM((2, DIFF_V + ONES_ROWS, tq), F32)]
        + _score_scratch(2, tq, tk),
        compiler_params=_cparams(("parallel", "arbitrary")),
        name="diff_attn",
    )(qk, qk, vt, lq1, lk1, lq2, lk2, g)


def _wo_kernel(oa_ref, ob_ref, wa_ref, wb_ref, x_ref, h_ref):
    acc = jnp.dot(oa_ref[...], wa_ref[...], preferred_element_type=F32)
    acc = acc + jnp.dot(ob_ref[...], wb_ref[...], preferred_element_type=F32)
    h_ref[...] = x_ref[...] + acc


def _wo_proj(oa, ob, w, x, tm, tn):
    m, ka = oa.shape
    kb = ob.shape[1]
    assert ka == kb and w.shape[0] == ka + kb
    n = w.shape[1]
    return pl.pallas_call(
        _wo_kernel,
        grid=(m // tm, n // tn),
        in_specs=[pl.BlockSpec((tm, ka), lambda i, j: (i, 0)),
                  pl.BlockSpec((tm, kb), lambda i, j: (i, 0)),
                  pl.BlockSpec((ka, tn), lambda i, j: (0, j)),
                  pl.BlockSpec((kb, tn), lambda i, j: (1, j)),
                  pl.BlockSpec((tm, tn), lambda i, j: (i, j))],
        out_specs=pl.BlockSpec((tm, tn), lambda i, j: (i, j)),
        out_shape=jax.ShapeDtypeStruct((m, n), F32),
        compiler_params=_cparams(("parallel", "arbitrary")),
        name="wo_proj",
    )(oa, ob, w, w, x)


def _gate_up_kernel(a_ref, wg_ref, wu_ref, wd_ref, o_ref, wd_o_ref):
    a = a_ref[...]
    g = jnp.dot(a, wg_ref[...].astype(BF16), preferred_element_type=F32)
    u = jnp.dot(a, wu_ref[...].astype(BF16), preferred_element_type=F32)
    o_ref[...] = (g * (1.0 / (1.0 + jnp.exp(-g))) * u).astype(o_ref.dtype)
    wd_o_ref[...] = wd_ref[...].astype(wd_o_ref.dtype)


def _gate_up(a, wg, wu, wd, tm, tn):
    m, k = a.shape
    n = wg.shape[1]
    ni, nj = m // tm, n // tn
    kd, nd = wd.shape
    assert kd % (ni * nj) == 0
    slab = kd // (ni * nj)
    return pl.pallas_call(
        _gate_up_kernel,
        grid=(ni, nj),
        in_specs=[pl.BlockSpec((tm, k), lambda i, j: (i, 0), pipeline_mode=pl.Buffered(1)),
                  pl.BlockSpec((k, tn), lambda i, j: (0, j)),
                  pl.BlockSpec((k, tn), lambda i, j: (0, j)),
                  pl.BlockSpec((slab, nd), lambda i, j: (i * nj + j, 0))],
        out_specs=[pl.BlockSpec((tm, tn), lambda i, j: (i, j)),
                   pl.BlockSpec((slab, nd), lambda i, j: (i * nj + j, 0))],
        out_shape=[jax.ShapeDtypeStruct((m, n), BF16), jax.ShapeDtypeStruct((kd, nd), BF16)],
        compiler_params=_cparams(("arbitrary", "arbitrary")),
        name="ffn_gate_up",
    )(a, wg, wu, wd)


def _down_kernel(a_ref, w_ref, h_ref, o_ref):
    o_ref[...] = h_ref[...] + jnp.dot(a_ref[...], w_ref[...], preferred_element_type=F32)


def _down_proj(a, w, h, tm, tn):
    m, k = a.shape
    n = w.shape[1]
    return pl.pallas_call(
        _down_kernel,
        grid=(m // tm, n // tn),
        in_specs=[pl.BlockSpec((tm, k), lambda i, j: (i, 0)),
                  pl.BlockSpec((k, tn), lambda i, j: (0, j)),
                  pl.BlockSpec((tm, tn), lambda i, j: (i, j))],
        out_specs=pl.BlockSpec((tm, tn), lambda i, j: (i, j)),
        out_shape=jax.ShapeDtypeStruct((m, n), F32),
        compiler_params=_cparams(("parallel", "arbitrary")),
        name="ffn_down",
    )(a, w, h)


def _rope_tables(s):
    pos = np.arange(s, dtype=np.float64)

    def cs(half):
        inv_freq = ROPE_THETA ** (-np.arange(half, dtype=np.float64) / half)
        ang = pos[:, None] * inv_freq[None, :]
        return np.cos(ang), np.sin(ang)

    c, sn = cs(MLA_ROPE // 2)
    z32 = np.zeros_like(c)
    z64 = np.zeros((s, 64))
    mla_cos = np.concatenate([c, c, z64], axis=-1)
    mla_sa = np.concatenate([-sn, z32, z64], axis=-1)
    mla_sb = np.concatenate([z32, sn, z64], axis=-1)
    c, sn = cs(DIFF_HEAD_DIM // 2)
    diff_cos = np.concatenate([c, c], axis=-1)
    diff_sin = np.concatenate([-sn, sn], axis=-1)
    return tuple(jnp.asarray(t.astype(np.float32))
                 for t in (mla_cos, mla_sa, mla_sb, diff_cos, diff_sin))


def _pad_gain(g, lo, hi, width):
    return jnp.pad(g[lo:hi], (0, width - (hi - lo))).reshape(1, width)


def kernel(x, attn_norm_g, w_in, q_latent_norm_g, kv_latent_norm_g, w_uq, w_ukv, mla_q_norm_g, mla_k_norm_g, diff_q_norm_g, diff_k_norm_g, lambda_q1, lambda_k1, lambda_q2, lambda_k2, diff_subln_g, w_o, ffn_norm_g, w_gate, w_up, w_down):
    b, s, d = x.shape
    assert b == 1 and d == D_MODEL and s % 512 == 0
    x2 = x.reshape(s, d)
    tm = min(1024, s)
    tp = min(512, s)
    tq = min(1024, s)
    tk = tq // 2

    w_in_t = jnp.swapaxes(w_in[0], 0, 1)
    n_diff = 2 * DIFF_QK_COLS + DIFF_V_COLS
    w_lat_t = _cast_rows_bf16(w_in_t, 0, Z_LAT, Z_LAT // 2)
    w_diff_t = _cast_rows_bf16(w_in_t, Z_LAT_USED, n_diff, math.gcd(Z_LAT_USED, n_diff))
    w_uq_p = jnp.pad(w_uq[0].reshape(MLA_Q_LORA, MLA_HEADS, MLA_QK),
                     ((0, 0), (0, 0), (0, MLA_HEAD_PAD - MLA_QK))
                     ).reshape(MLA_Q_LORA, MLA_HEADS * MLA_HEAD_PAD).astype(BF16)
    w_ukv3 = w_ukv[0].reshape(MLA_KV_LORA, MLA_HEADS, MLA_NOPE + MLA_V)
    w_uk = w_ukv3[:, :, :MLA_NOPE].reshape(MLA_KV_LORA, MLA_HEADS * MLA_NOPE).astype(BF16)
    w_uv_t = w_ukv3[:, :, MLA_NOPE:].reshape(MLA_KV_LORA, MLA_HEADS * MLA_V).T.astype(BF16)

    mla_cos, mla_sa, mla_sb, diff_cos, diff_sin = _rope_tables(s)
    gq_a = _pad_gain(mla_q_norm_g[0], 0, MLA_NOPE, LANES)
    gq_b = _pad_gain(mla_q_norm_g[0], MLA_NOPE, MLA_QK, LANES)
    gk_a = _pad_gain(mla_k_norm_g[0], 0, MLA_NOPE, LANES)
    gk_b = _pad_gain(mla_k_norm_g[0], MLA_NOPE, MLA_QK, LANES)
    g_diff = jnp.concatenate([jnp.tile(diff_q_norm_g[0] * DIFF_QSCALE, 2 * DIFF_HEADS),
                              jnp.tile(diff_k_norm_g[0], 2 * DIFF_HEADS)]).reshape(1, -1)

    n = _rmsnorm(x2, attn_norm_g[0], min(512, s))
    z_lat, w_o_b = _matmul_nt(n, w_lat_t, tm, Z_LAT // 2, BF16, "in_proj_lat",
                              ride=w_o[0])
    z_qk = _matmul_nt(n, w_diff_t, tm, 1024, BF16, "in_proj_diff",
                      n=2 * DIFF_QK_COLS)
    dv_t = _proj_t(w_diff_t, 2 * DIFF_QK_COLS // DIFF_V_COLS, DIFF_V_COLS, n, tp,
                   "in_proj_dv_t")
    q_a = _q_prep(z_lat, q_latent_norm_g[0].reshape(1, -1), w_uq_p, gq_a, gq_b,
                  mla_cos, mla_sa, mla_sb, tm)
    k_a, vt_a = _kv_prep(z_lat, kv_latent_norm_g[0].reshape(1, -1), w_uk, w_uv_t, gk_a, gk_b,
                        mla_cos, mla_sa, mla_sb, tp)
    o_a = _mla_attn(q_a, k_a, vt_a, tq, tk)
    qk_b = _diff_prep(z_qk, g_diff, diff_cos, diff_sin, tm)
    o_b = _diff_attn(qk_b, dv_t, lambda_q1[0].reshape(1, -1), lambda_k1[0].reshape(1, -1),
                     lambda_q2[0].reshape(1, -1), lambda_k2[0].reshape(1, -1),
                     diff_subln_g[0].reshape(1, -1), tq, tk)
    h = _wo_proj(o_a, o_b, w_o_b, x2, tm, 1024)

    m = _rmsnorm(h, ffn_norm_g[0], min(512, s))
    a, w_down_b = _gate_up(m, w_gate[0], w_up[0], w_down[0], min(2048, s), 256)
    out = _down_proj(a, w_down_b, h, min(512, s), 512)
    return out.reshape(b, s, d)
```

```python
import functools
import math
from typing import Any, NamedTuple

import jax
import jax.numpy as jnp
import numpy as np
from jax import lax
from jax.experimental import pallas as pl
from jax.experimental.pallas import tpu as pltpu

F32 = jnp.float32
BF16 = jnp.bfloat16

D_MODEL = 4096
CHUNK = 64
ROPE_THETA = 10000.0
EPS = 1e-6
MLA_HEADS = 16
MLA_Q_LORA = 768
MLA_KV_LORA = 512
MLA_NOPE = 128
MLA_ROPE = 64
MLA_QK = MLA_NOPE + MLA_ROPE
MLA_V = 128
LOG2E = math.log2(math.e)
MLA_SCALE = 1.0 / math.sqrt(MLA_QK)
MLA_QSCALE = MLA_SCALE * LOG2E
DIFF_HEADS = 8
DIFF_HEAD_DIM = 128
DIFF_V = 2 * DIFF_HEAD_DIM
DIFF_SCALE = 1.0 / math.sqrt(DIFF_HEAD_DIM)
DIFF_QSCALE = DIFF_SCALE * LOG2E
DIFF_QK_COLS = 2 * DIFF_HEADS * DIFF_HEAD_DIM
DIFF_V_COLS = DIFF_HEADS * DIFF_V
D_FF = -(-8 * D_MODEL // (3 * 256)) * 256
LAMBDA_INIT = 0.8 - 0.6 * math.exp(-0.3 * 0)

LANES = 128
MLA_HEAD_PAD = 2 * LANES
Z_LAT = 1536
Z_KPE = MLA_Q_LORA + MLA_KV_LORA
Z_LAT_USED = Z_KPE + MLA_ROPE
NEG = -1e30
VMEM_LIMIT = 56 * 1024 * 1024


def _cparams(sem):
    return pltpu.CompilerParams(dimension_semantics=sem, vmem_limit_bytes=VMEM_LIMIT)


def _cast_kernel(w_ref, o_ref):
    o_ref[...] = w_ref[...].astype(o_ref.dtype)


def _cast_rows_bf16(w, row0, n_rows, tr):
    c = w.shape[1]
    assert row0 % tr == 0 and n_rows % tr == 0 and row0 + n_rows <= w.shape[0]
    blk0 = row0 // tr
    return pl.pallas_call(
        _cast_kernel,
        grid=(n_rows // tr,),
        in_specs=[pl.BlockSpec((tr, c), lambda i: (i + blk0, 0))],
        out_specs=pl.BlockSpec((tr, c), lambda i: (i, 0)),
        out_shape=jax.ShapeDtypeStruct((n_rows, c), BF16),
        compiler_params=_cparams(("parallel",)),
        name="cast_bf16",
    )(w)


def _rmsnorm_kernel(x_ref, g_ref, o_ref):
    x = x_ref[...]
    ms = jnp.mean(x * x, axis=-1, keepdims=True)
    o_ref[...] = (x * lax.rsqrt(ms + EPS) * g_ref[...]).astype(o_ref.dtype)


def _rmsnorm(x, g, tm):
    m, d = x.shape
    return pl.pallas_call(
        _rmsnorm_kernel,
        grid=(m // tm,),
        in_specs=[pl.BlockSpec((tm, d), lambda i: (i, 0)),
                  pl.BlockSpec((1, d), lambda i: (0, 0))],
        out_specs=pl.BlockSpec((tm, d), lambda i: (i, 0)),
        out_shape=jax.ShapeDtypeStruct((m, d), BF16),
        compiler_params=_cparams(("parallel",)),
        name="rmsnorm",
    )(x, g.reshape(1, d))


def _mm_nt_kernel(a_ref, bt_ref, o_ref):
    o_ref[...] = lax.dot_general(a_ref[...], bt_ref[...], (((1,), (1,)), ((), ())),
                                 preferred_element_type=F32).astype(o_ref.dtype)


def _mm_nt_ride_kernel(a_ref, bt_ref, w_ref, o_ref, w_o_ref):
    _mm_nt_kernel(a_ref, bt_ref, o_ref)
    w_o_ref[...] = w_ref[...].astype(w_o_ref.dtype)


def _matmul_nt(a, bt, tm, tn, out_dtype, name, ride=None, n=None):
    m, k = a.shape
    n = bt.shape[0] if n is None else n
    assert bt.shape[1] == k and n % tn == 0 and n <= bt.shape[0]
    ni, nj = m // tm, n // tn
    in_specs = [pl.BlockSpec((tm, k), lambda i, j: (i, 0)),
                pl.BlockSpec((tn, k), lambda i, j: (j, 0))]
    out_spec = pl.BlockSpec((tm, tn), lambda i, j: (i, j))
    out_shape = jax.ShapeDtypeStruct((m, n), out_dtype)
    if ride is None:
        return pl.pallas_call(
            _mm_nt_kernel, grid=(ni, nj), in_specs=in_specs, out_specs=out_spec,
            out_shape=out_shape, compiler_params=_cparams(("parallel", "arbitrary")),
            name=name)(a, bt)
    rr, rc = ride.shape
    assert rr % (ni * nj) == 0
    slab = pl.BlockSpec((rr // (ni * nj), rc), lambda i, j: (i * nj + j, 0))
    return pl.pallas_call(
        _mm_nt_ride_kernel, grid=(ni, nj), in_specs=in_specs + [slab],
        out_specs=[out_spec, slab],
        out_shape=[out_shape, jax.ShapeDtypeStruct((rr, rc), BF16)],
        compiler_params=_cparams(("arbitrary", "arbitrary")),
        name=name)(a, bt, ride)


def _proj_t(wt, blk, rows, a, tm, name):
    m, k = a.shape
    return pl.pallas_call(
        _mm_nt_kernel,
        grid=(m // tm,),
        in_specs=[pl.BlockSpec((rows, k), lambda i: (blk, 0), pipeline_mode=pl.Buffered(1)),
                  pl.BlockSpec((tm, k), lambda i: (i, 0))],
        out_specs=pl.BlockSpec((rows, tm), lambda i: (0, i)),
        out_shape=jax.ShapeDtypeStruct((rows, m), BF16),
        compiler_params=_cparams(("parallel",)),
        name=name,
    )(wt, a)


def _latent_norm(c, g):
    ms = jnp.mean(c * c, axis=-1, keepdims=True)
    return (c * lax.rsqrt(ms + EPS) * g).astype(BF16)


ROPE_HALF = MLA_ROPE // 2


def _mla_rope(y, cos, sin):
    return y * cos + pltpu.roll(y, 64, 1) * sin


def _q_prep_kernel(z_ref, gl_ref, w_ref, ga_ref, gb_ref, cos_ref, sin_ref, o_ref):
    ga, gb = ga_ref[...], gb_ref[...]
    half = z_ref.shape[0] // 2
    for rows in (slice(0, half), slice(half, 2 * half)):
        c = z_ref[rows, :MLA_Q_LORA].astype(F32)
        lat = _latent_norm(c, gl_ref[...])
        q = jnp.dot(lat, w_ref[...], preferred_element_type=F32)
        cos, sin = cos_ref[rows, :], sin_ref[rows, :]
        for h in range(MLA_HEADS):
            lo = h * MLA_HEAD_PAD
            qa = q[:, lo:lo + LANES]
            qb = q[:, lo + LANES:lo + 2 * LANES]
            ss = jnp.sum(qa * qa + qb * qb, axis=-1, keepdims=True)
            r = lax.rsqrt(ss * (1.0 / MLA_QK) + EPS)
            ya = qa * r * ga
            yb = _mla_rope(qb * r * gb, cos, sin)
            o_ref[rows, lo:lo + LANES] = (ya * MLA_QSCALE).astype(o_ref.dtype)
            o_ref[rows, lo + LANES:lo + 2 * LANES] = (yb * MLA_QSCALE).astype(o_ref.dtype)


def _kv_prep_kernel(z_ref, gl_ref, wk_ref, wvt_ref, ga_ref, gb_ref, cos_ref, sin_ref,
                    k_ref, vt_ref):
    c = z_ref[:, MLA_Q_LORA:Z_KPE].astype(F32)
    lat = _latent_norm(c, gl_ref[...])
    kn = jnp.dot(lat, wk_ref[...], preferred_element_type=F32)
    vt_ref[...] = lax.dot_general(wvt_ref[...], lat, (((1,), (1,)), ((), ())),
                                  preferred_element_type=F32).astype(vt_ref.dtype)
    lane = lax.broadcasted_iota(jnp.int32, (1, LANES), 1)
    raw = z_ref[:, Z_KPE:Z_KPE + LANES].astype(F32)
    t1 = jnp.where(lane < ROPE_HALF, raw, 0.0)
    t2 = jnp.where((lane >= ROPE_HALF) & (lane < MLA_ROPE), raw, 0.0)
    kpe = t1 + pltpu.roll(t2, ROPE_HALF, 1)
    ss_pe = jnp.sum(kpe * kpe, axis=-1, keepdims=True)
    pe_rot = _mla_rope(kpe * gb_ref[...], cos_ref[...], sin_ref[...])
    ga = ga_ref[...]
    for h in range(MLA_HEADS):
        a = kn[:, h * LANES:(h + 1) * LANES]
        ss = jnp.sum(a * a, axis=-1, keepdims=True) + ss_pe
        r = lax.rsqrt(ss * (1.0 / MLA_QK) + EPS)
        lo = h * MLA_HEAD_PAD
        k_ref[:, lo:lo + LANES] = (a * r * ga).astype(k_ref.dtype)
        k_ref[:, lo + LANES:lo + 2 * LANES] = (pe_rot * r).astype(k_ref.dtype)


def _row_spec(tm, n, col=0):
    return pl.BlockSpec((tm, n), lambda i: (i, col))


def _const_spec(shape):
    return pl.BlockSpec(shape, lambda i: (0, 0))


def _q_prep(z, gl, w, ga, gb, cos, sin, tm):
    m = z.shape[0]
    n = MLA_HEADS * MLA_HEAD_PAD
    return pl.pallas_call(
        _q_prep_kernel,
        grid=(m // tm,),
        in_specs=[_row_spec(tm, Z_LAT), _const_spec(gl.shape), _const_spec(w.shape),
                  _const_spec(ga.shape), _const_spec(gb.shape),
                  _row_spec(tm, LANES), _row_spec(tm, LANES)],
        out_specs=_row_spec(tm, n),
        out_shape=jax.ShapeDtypeStruct((m, n), BF16),
        compiler_params=_cparams(("parallel",)),
        name="mla_q_prep",
    )(z, gl, w, ga, gb, cos, sin)


def _kv_prep(z, gl, wk, wvt, ga, gb, cos, sin, tm):
    m = z.shape[0]
    nk = MLA_HEADS * MLA_HEAD_PAD
    nv = MLA_HEADS * MLA_V
    return pl.pallas_call(
        _kv_prep_kernel,
        grid=(m // tm,),
        in_specs=[_row_spec(tm, Z_LAT), _const_spec(gl.shape), _const_spec(wk.shape),
                  _const_spec(wvt.shape), _const_spec(ga.shape), _const_spec(gb.shape),
                  _row_spec(tm, LANES), _row_spec(tm, LANES)],
        out_specs=[_row_spec(tm, nk), pl.BlockSpec((nv, tm), lambda i: (0, i))],
        out_shape=[jax.ShapeDtypeStruct((m, nk), BF16), jax.ShapeDtypeStruct((nv, m), BF16)],
        compiler_params=_cparams(("parallel",)),
        name="mla_kv_prep",
    )(z, gl, wk, wvt, ga, gb, cos, sin)


DIFF_PREP_COLS = 1024


def _diff_prep_kernel(x_ref, g_ref, cos_ref, sin_ref, o_ref):
    cos, sin = cos_ref[...], sin_ref[...]
    for c in range(DIFF_PREP_COLS // LANES):
        sl = slice(c * LANES, (c + 1) * LANES)
        x = x_ref[:, sl].astype(F32)
        ms = jnp.mean(x * x, axis=-1, keepdims=True)
        y = x * lax.rsqrt(ms + EPS) * g_ref[:, sl]
        o_ref[:, sl] = (y * cos + pltpu.roll(y, 64, 1) * sin).astype(o_ref.dtype)


def _diff_prep(z, g, cos, sin, tm):
    m = z.shape[0]
    n = 2 * DIFF_QK_COLS
    nb = n // DIFF_PREP_COLS
    return pl.pallas_call(
        _diff_prep_kernel,
        grid=(m // tm, nb),
        in_specs=[pl.BlockSpec((tm, DIFF_PREP_COLS), lambda i, j: (i, j)),
                  pl.BlockSpec((1, DIFF_PREP_COLS), lambda i, j: (0, j)),
                  pl.BlockSpec((tm, LANES), lambda i, j: (i, 0)),
                  pl.BlockSpec((tm, LANES), lambda i, j: (i, 0))],
        out_specs=pl.BlockSpec((tm, DIFF_PREP_COLS), lambda i, j: (i, j)),
        out_shape=jax.ShapeDtypeStruct((m, n), BF16),
        compiler_params=_cparams(("parallel", "arbitrary")),
        name="diff_qk_prep",
    )(z, g, cos, sin)


def _scores_t(k, q):
    return lax.dot_general(k, q, (((1,), (1,)), ((), ())), preferred_element_type=F32)


class _Buf(NamedTuple):
    s: Any
    bmax: Any

    def cols(self, sl):
        return _Buf(self.s.at[:, sl], self.bmax.at[:, sl])


def _scores_into(buf, k, q):
    s = _scores_t(k, q)
    buf.s[...] = s
    buf.bmax[...] = jnp.max(s, axis=0, keepdims=True)


def _mask_diag_block(buf, c0, tk):
    upper = lax.broadcasted_iota(jnp.int32, (CHUNK, LANES), 1) >= CHUNK
    for g in range(tk // LANES):
        cols = pl.ds(c0 + g * LANES, LANES)
        r0 = g * LANES + CHUNK
        edge = jnp.where(upper, buf.s[r0:r0 + CHUNK, cols], NEG)
        buf.s[r0:r0 + CHUNK, cols] = edge
        if r0 + CHUNK < tk:
            buf.s[r0 + CHUNK:tk, cols] = jnp.full((tk - r0 - CHUNK, LANES), NEG, F32)
        buf.bmax[:, cols] = jnp.maximum(jnp.max(buf.s[0:r0, cols], axis=0, keepdims=True),
                                        jnp.max(edge, axis=0, keepdims=True))


ONES_ROWS = 16


def _online_update_t(buf, vt, m_sc, acc_sc):
    tk, tq = buf.s.shape
    m_prev = m_sc[...]
    m_new = jnp.maximum(m_prev, buf.bmax[...])
    alpha = jnp.exp2(m_prev - m_new)
    p_t = jnp.exp2(buf.s[...] - m_new)
    v_aug = jnp.concatenate([vt, jnp.ones((ONES_ROWS, tk), BF16)], axis=0)
    pv = jnp.dot(v_aug, p_t.astype(BF16), preferred_element_type=F32)
    acc_sc[...] = alpha * acc_sc[...] + pv
    m_sc[...] = m_new


class _Chain(NamedTuple):
    q: Any
    k_blk: Any
    vt_blk: Any
    m: Any
    acc: Any
    buf_a: Any
    buf_b: Any


def _flash_pairs(i, tq, tk, chains):
    assert tq == 2 * tk

    def k_start(j):
        return pl.multiple_of(j * tk, tk)

    def qk(use_b, j):
        k0 = k_start(j)
        for c in chains:
            _scores_into(c.buf_b if use_b else c.buf_a, c.k_blk(k0), c.q)

    def update(use_b, j):
        k0 = k_start(j)
        for c in chains:
            _online_update_t(c.buf_b if use_b else c.buf_a, c.vt_blk(k0), c.m, c.acc)

    for c in chains:
        c.m[...] = jnp.full(c.m.shape, NEG, F32)
        c.acc[...] = jnp.zeros(c.acc.shape, F32)

    qk(False, 0)

    def body(p, carry):
        qk(True, 2 * p + 1)
        update(False, 2 * p)
        qk(False, 2 * p + 2)
        update(True, 2 * p + 1)
        return carry

    lax.fori_loop(0, i, body, 0)
    half = pl.ds(tk, tk)
    k_last = k_start(2 * i + 1)
    for c in chains:
        _mask_diag_block(c.buf_a, 0, tk)
        _scores_into(c.buf_b.cols(half), c.k_blk(k_last), c.q[tk:, :])
        _mask_diag_block(c.buf_b, tk, tk)
    update(False, 2 * i)
    for c in chains:
        _online_update_t(c.buf_b.cols(half), c.vt_blk(k_last), c.m.at[:, half],
                         c.acc.at[:, half])


def _score_scratch(n, tq, tk):
    one = [pltpu.VMEM((n, tk, tq), F32), pltpu.VMEM((n, 1, tq), F32)]
    return one + one


def _bufs(refs, idx):
    sa, ma, sb, mb = refs
    return _Buf(sa.at[idx], ma.at[idx]), _Buf(sb.at[idx], mb.at[idx])


MLA_HEADS_PER_STEP = 2


def _mla_attn_kernel(q_ref, k_ref, vt_ref, o_ref, m_sc, acc_sc, *score_refs, tq, tk):
    nh = m_sc.shape[0]
    chains = []
    for h in range(nh):
        qk_cols = slice(h * MLA_HEAD_PAD, (h + 1) * MLA_HEAD_PAD)
        v_rows = slice(h * MLA_V, (h + 1) * MLA_V)
        buf_a, buf_b = _bufs(score_refs, h)
        chains.append(_Chain(
            q=q_ref[:, qk_cols],
            k_blk=functools.partial(lambda k0, cols: k_ref[pl.ds(k0, tk), cols], cols=qk_cols),
            vt_blk=functools.partial(lambda k0, rows: vt_ref[rows, pl.ds(k0, tk)], rows=v_rows),
            m=m_sc.at[h], acc=acc_sc.at[h], buf_a=buf_a, buf_b=buf_b))
    _flash_pairs(pl.program_id(1), tq, tk, chains)
    for h in range(nh):
        acc = acc_sc[h]
        o = acc[:MLA_V] / acc[MLA_V:MLA_V + 1]
        o_ref[:, h * MLA_V:(h + 1) * MLA_V] = o.T.astype(o_ref.dtype)


def _mla_attn(q, k, vt, tq, tk):
    s = q.shape[0]
    nh = MLA_HEADS_PER_STEP
    kern = functools.partial(_mla_attn_kernel, tq=tq, tk=tk)
    return pl.pallas_call(
        kern,
        grid=(MLA_HEADS // nh, s // tq),
        in_specs=[pl.BlockSpec((tq, nh * MLA_HEAD_PAD), lambda h, i: (i, h)),
                  pl.BlockSpec((s, nh * MLA_HEAD_PAD), lambda h, i: (0, h)),
                  pl.BlockSpec((nh * MLA_V, s), lambda h, i: (h, 0))],
        out_specs=pl.BlockSpec((tq, nh * MLA_V), lambda h, i: (i, h)),
        out_shape=jax.ShapeDtypeStruct((s, MLA_HEADS * MLA_V), BF16),
        scratch_shapes=[pltpu.VMEM((nh, 1, tq), F32),
                        pltpu.VMEM((nh, MLA_V + ONES_ROWS, tq), F32)]
        + _score_scratch(nh, tq, tk),
        compiler_params=_cparams(("parallel", "arbitrary")),
        name="mla_attn",
    )(q, k, vt)


def _diff_attn_kernel(q_ref, k_ref, vt_ref, lq1_ref, lk1_ref, lq2_ref, lk2_ref, g_ref, o_ref,
                      m_sc, acc_sc, *score_refs, tq, tk):
    chains = []
    for t in range(2):
        cols = slice(t * DIFF_HEAD_DIM, (t + 1) * DIFF_HEAD_DIM)
        buf_a, buf_b = _bufs(score_refs, t)
        chains.append(_Chain(
            q=q_ref[:, cols],
            k_blk=functools.partial(lambda k0, cols: k_ref[pl.ds(k0, tk), cols], cols=cols),
            vt_blk=lambda k0: vt_ref[:, pl.ds(k0, tk)],
            m=m_sc.at[t], acc=acc_sc.at[t], buf_a=buf_a, buf_b=buf_b))
    _flash_pairs(pl.program_id(1), tq, tk, chains)

    lam = (jnp.exp(jnp.sum(lq1_ref[...] * lk1_ref[...], axis=-1, keepdims=True))
           - jnp.exp(jnp.sum(lq2_ref[...] * lk2_ref[...], axis=-1, keepdims=True))
           + LAMBDA_INIT)
    a1, a2 = acc_sc[0], acc_sc[1]
    o = (a1[:DIFF_V] / a1[DIFF_V:DIFF_V + 1]
         - lam * (a2[:DIFF_V] / a2[DIFF_V:DIFF_V + 1])).T
    ms = jnp.mean(o * o, axis=-1, keepdims=True)
    o = o * lax.rsqrt(ms + EPS) * g_ref[...]
    o_ref[...] = (o * (1.0 - LAMBDA_INIT)).astype(o_ref.dtype)


def _diff_attn(qk, vt, lq1, lk1, lq2, lk2, g, tq, tk):
    s = qk.shape[0]
    kern = functools.partial(_diff_attn_kernel, tq=tq, tk=tk)
    k_off = DIFF_QK_COLS // DIFF_V
    vec = pl.BlockSpec((1, DIFF_HEAD_DIM), lambda h, i: (0, 0))
    return pl.pallas_call(
        kern,
        grid=(DIFF_HEADS, s // tq),
        in_specs=[pl.BlockSpec((tq, DIFF_V), lambda h, i: (i, h)),
                  pl.BlockSpec((s, DIFF_V), lambda h, i: (0, h + k_off)),
                  pl.BlockSpec((DIFF_V, s), lambda h, i: (h, 0)),
                  vec, vec, vec, vec,
                  pl.BlockSpec((1, DIFF_V), lambda h, i: (0, 0))],
        out_specs=pl.BlockSpec((tq, DIFF_V), lambda h, i: (i, h)),
        out_shape=jax.ShapeDtypeStruct((s, DIFF_HEADS * DIFF_V), BF16),
        scratch_shapes=[pltpu.VMEM((2, 1, tq), F32),
                        pltpu.VMEM((2, DIFF_V + ONES_ROWS, tq), F32)]
        + _score_scratch(2, tq, tk),
        compiler_params=_cparams(("parallel", "arbitrary")),
        name="diff_attn",
    )(qk, qk, vt, lq1, lk1, lq2, lk2, g)


def _wo_kernel(oa_ref, ob_ref, wa_ref, wb_ref, x_ref, h_ref):
    acc = jnp.dot(oa_ref[...], wa_ref[...], preferred_element_type=F32)
    acc = acc + jnp.dot(ob_ref[...], wb_ref[...], preferred_element_type=F32)
    h_ref[...] = x_ref[...] + acc


def _wo_proj(oa, ob, w, x, tm, tn):
    m, ka = oa.shape
    kb = ob.shape[1]
    assert ka == kb and w.shape[0] == ka + kb
    n = w.shape[1]
    return pl.pallas_call(
        _wo_kernel,
        grid=(m // tm, n // tn),
        in_specs=[pl.BlockSpec((tm, ka), lambda i, j: (i, 0)),
                  pl.BlockSpec((tm, kb), lambda i, j: (i, 0)),
                  pl.BlockSpec((ka, tn), lambda i, j: (0, j)),
                  pl.BlockSpec((kb, tn), lambda i, j: (1, j)),
                  pl.BlockSpec((tm, tn), lambda i, j: (i, j))],
        out_specs=pl.BlockSpec((tm, tn), lambda i, j: (i, j)),
        out_shape=jax.ShapeDtypeStruct((m, n), F32),
        compiler_params=_cparams(("parallel", "arbitrary")),
        name="wo_proj",
    )(oa, ob, w, w, x)


def _gate_up_kernel(a_ref, wg_ref, wu_ref, wd_ref, o_ref, wd_o_ref):
    a = a_ref[...]
    g = jnp.dot(a, wg_ref[...].astype(BF16), preferred_element_type=F32)
    u = jnp.dot(a, wu_ref[...].astype(BF16), preferred_element_type=F32)
    o_ref[...] = (g * (1.0 / (1.0 + jnp.exp(-g))) * u).astype(o_ref.dtype)
    wd_o_ref[...] = wd_ref[...].astype(wd_o_ref.dtype)


def _gate_up(a, wg, wu, wd, tm, tn):
    m, k = a.shape
    n = wg.shape[1]
    ni, nj = m // tm, n // tn
    kd, nd = wd.shape
    assert kd % (ni * nj) == 0
    slab = kd // (ni * nj)
    return pl.pallas_call(
        _gate_up_kernel,
        grid=(ni, nj),
        in_specs=[pl.BlockSpec((tm, k), lambda i, j: (i, 0), pipeline_mode=pl.Buffered(1)),
                  pl.BlockSpec((k, tn), lambda i, j: (0, j)),
                  pl.BlockSpec((k, tn), lambda i, j: (0, j)),
                  pl.BlockSpec((slab, nd), lambda i, j: (i * nj + j, 0))],
        out_specs=[pl.BlockSpec((tm, tn), lambda i, j: (i, j)),
                   pl.BlockSpec((slab, nd), lambda i, j: (i * nj + j, 0))],
        out_shape=[jax.ShapeDtypeStruct((m, n), BF16), jax.ShapeDtypeStruct((kd, nd), BF16)],
        compiler_params=_cparams(("arbitrary", "arbitrary")),
        name="ffn_gate_up",
    )(a, wg, wu, wd)


def _down_kernel(a_ref, w_ref, h_ref, o_ref):
    o_ref[...] = h_ref[...] + jnp.dot(a_ref[...], w_ref[...], preferred_element_type=F32)


def _down_proj(a, w, h, tm, tn):
    m, k = a.shape
    n = w.shape[1]
    return pl.pallas_call(
        _down_kernel,
        grid=(m // tm, n // tn),
        in_specs=[pl.BlockSpec((tm, k), lambda i, j: (i, 0)),
                  pl.BlockSpec((k, tn), lambda i, j: (0, j)),
                  pl.BlockSpec((tm, tn), lambda i, j: (i, j))],
        out_specs=pl.BlockSpec((tm, tn), lambda i, j: (i, j)),
        out_shape=jax.ShapeDtypeStruct((m, n), F32),
        compiler_params=_cparams(("parallel", "arbitrary")),
        name="ffn_down",
    )(a, w, h)


def _rope_tables(s):
    pos = np.arange(s, dtype=np.float64)

    def cs(half):
        inv_freq = ROPE_THETA ** (-np.arange(half, dtype=np.float64) / half)
        ang = pos[:, None] * inv_freq[None, :]
        return np.cos(ang), np.sin(ang)

    c, sn = cs(ROPE_HALF)
    gap = np.zeros_like(c)
    mla_cos = np.concatenate([c, gap, c, gap], axis=-1)
    mla_sin = np.concatenate([-sn, gap, sn, gap], axis=-1)
    c, sn = cs(DIFF_HEAD_DIM // 2)
    diff_cos = np.concatenate([c, c], axis=-1)
    diff_sin = np.concatenate([-sn, sn], axis=-1)
    return tuple(jnp.asarray(t.astype(np.float32))
                 for t in (mla_cos, mla_sin, diff_cos, diff_sin))


def _pad_gain(g, lo, hi, width):
    return jnp.pad(g[lo:hi], (0, width - (hi - lo))).reshape(1, width)


def _spread_rope(t, axis):
    t1, t2 = jnp.split(t, 2, axis=axis)
    gap = jnp.zeros_like(t1)
    return jnp.concatenate([t1, gap, t2, gap], axis=axis)


def kernel(x, attn_norm_g, w_in, q_latent_norm_g, kv_latent_norm_g, w_uq, w_ukv, mla_q_norm_g, mla_k_norm_g, diff_q_norm_g, diff_k_norm_g, lambda_q1, lambda_k1, lambda_q2, lambda_k2, diff_subln_g, w_o, ffn_norm_g, w_gate, w_up, w_down):
    b, s, d = x.shape
    assert b == 1 and d == D_MODEL and s % 512 == 0
    x2 = x.reshape(s, d)
    tm = min(1024, s)
    tp = min(512, s)
    tq = min(1024, s)
    tk = tq // 2

    w_in_t = jnp.swapaxes(w_in[0], 0, 1)
    n_diff = 2 * DIFF_QK_COLS + DIFF_V_COLS
    w_lat_t = _cast_rows_bf16(w_in_t, 0, Z_LAT, Z_LAT // 2)
    w_diff_t = _cast_rows_bf16(w_in_t, Z_LAT_USED, n_diff, math.gcd(Z_LAT_USED, n_diff))
    w_uq3 = w_uq[0].reshape(MLA_Q_LORA, MLA_HEADS, MLA_QK)
    w_uq_p = jnp.concatenate([w_uq3[:, :, :MLA_NOPE], _spread_rope(w_uq3[:, :, MLA_NOPE:], 2)],
                             axis=2).reshape(MLA_Q_LORA, MLA_HEADS * MLA_HEAD_PAD).astype(BF16)
    w_ukv3 = w_ukv[0].reshape(MLA_KV_LORA, MLA_HEADS, MLA_NOPE + MLA_V)
    w_uk = w_ukv3[:, :, :MLA_NOPE].reshape(MLA_KV_LORA, MLA_HEADS * MLA_NOPE).astype(BF16)
    w_uv_t = w_ukv3[:, :, MLA_NOPE:].reshape(MLA_KV_LORA, MLA_HEADS * MLA_V).T.astype(BF16)

    mla_cos, mla_sin, diff_cos, diff_sin = _rope_tables(s)
    gq_a = _pad_gain(mla_q_norm_g[0], 0, MLA_NOPE, LANES)
    gq_b = _spread_rope(mla_q_norm_g[0][MLA_NOPE:], 0).reshape(1, LANES)
    gk_a = _pad_gain(mla_k_norm_g[0], 0, MLA_NOPE, LANES)
    gk_b = _spread_rope(mla_k_norm_g[0][MLA_NOPE:], 0).reshape(1, LANES)
    g_diff = jnp.concatenate([jnp.tile(diff_q_norm_g[0] * DIFF_QSCALE, 2 * DIFF_HEADS),
                              jnp.tile(diff_k_norm_g[0], 2 * DIFF_HEADS)]).reshape(1, -1)

    n = _rmsnorm(x2, attn_norm_g[0], min(512, s))
    z_lat = _matmul_nt(n, w_lat_t, tm, Z_LAT // 2, BF16, "in_proj_lat")
    z_qk, w_o_b = _matmul_nt(n, w_diff_t, tm, 1024, BF16, "in_proj_diff",
                             ride=w_o[0], n=2 * DIFF_QK_COLS)
    dv_t = _proj_t(w_diff_t, 2 * DIFF_QK_COLS // DIFF_V_COLS, DIFF_V_COLS, n, tp,
                   "in_proj_dv_t")
    q_a = _q_prep(z_lat, q_latent_norm_g[0].reshape(1, -1), w_uq_p, gq_a, gq_b,
                  mla_cos, mla_sin, tm)
    k_a, vt_a = _kv_prep(z_lat, kv_latent_norm_g[0].reshape(1, -1), w_uk, w_uv_t, gk_a, gk_b,
                        mla_cos, mla_sin, tp)
    o_a = _mla_attn(q_a, k_a, vt_a, tq, tk)
    qk_b = _diff_prep(z_qk, g_diff, diff_cos, diff_sin, tm)
    o_b = _diff_attn(qk_b, dv_t, lambda_q1[0].reshape(1, -1), lambda_k1[0].reshape(1, -1),
                     lambda_q2[0].reshape(1, -1), lambda_k2[0].reshape(1, -1),
                     diff_subln_g[0].reshape(1, -1), tq, tk)
    h = _wo_proj(o_a, o_b, w_o_b, x2, tm, 1024)

    m = _rmsnorm(h, ffn_norm_g[0], min(512, s))
    a, w_down_b = _gate_up(m, w_gate[0], w_up[0], w_down[0], min(2048, s), 256)
    out = _down_proj(a, w_down_b, h, min(512, s), 512)
    return out.reshape(b, s, d)
```

```python
import functools
import math
from typing import Any, NamedTuple

import jax
import jax.numpy as jnp
import numpy as np
from jax import lax
from jax.experimental import pallas as pl
from jax.experimental.pallas import tpu as pltpu

F32 = jnp.float32
BF16 = jnp.bfloat16

D_MODEL = 4096
CHUNK = 64
ROPE_THETA = 10000.0
EPS = 1e-6
MLA_HEADS = 16
MLA_Q_LORA = 768
MLA_KV_LORA = 512
MLA_NOPE = 128
MLA_ROPE = 64
MLA_QK = MLA_NOPE + MLA_ROPE
MLA_V = 128
LOG2E = math.log2(math.e)
MLA_SCALE = 1.0 / math.sqrt(MLA_QK)
MLA_QSCALE = MLA_SCALE * LOG2E
DIFF_HEADS = 8
DIFF_HEAD_DIM = 128
DIFF_V = 2 * DIFF_HEAD_DIM
DIFF_SCALE = 1.0 / math.sqrt(DIFF_HEAD_DIM)
DIFF_QSCALE = DIFF_SCALE * LOG2E
DIFF_QK_COLS = 2 * DIFF_HEADS * DIFF_HEAD_DIM
DIFF_V_COLS = DIFF_HEADS * DIFF_V
D_FF = -(-8 * D_MODEL // (3 * 256)) * 256
LAMBDA_INIT = 0.8 - 0.6 * math.exp(-0.3 * 0)

LANES = 128
MLA_HEAD_PAD = 2 * LANES
Z_LAT = 1536
Z_KPE = MLA_Q_LORA + MLA_KV_LORA
Z_LAT_USED = Z_KPE + MLA_ROPE
NEG = -1e30
VMEM_LIMIT = 56 * 1024 * 1024


def _cparams(sem):
    return pltpu.CompilerParams(dimension_semantics=sem, vmem_limit_bytes=VMEM_LIMIT)


def _cast_kernel(w_ref, o_ref):
    o_ref[...] = w_ref[...].astype(o_ref.dtype)


def _cast_rows_bf16(w, row0, n_rows, tr):
    c = w.shape[1]
    assert row0 % tr == 0 and n_rows % tr == 0 and row0 + n_rows <= w.shape[0]
    blk0 = row0 // tr
    return pl.pallas_call(
        _cast_kernel,
        grid=(n_rows // tr,),
        in_specs=[pl.BlockSpec((tr, c), lambda i: (i + blk0, 0))],
        out_specs=pl.BlockSpec((tr, c), lambda i: (i, 0)),
        out_shape=jax.ShapeDtypeStruct((n_rows, c), BF16),
        compiler_params=_cparams(("parallel",)),
        name="cast_bf16",
    )(w)


def _rmsnorm_kernel(x_ref, g_ref, o_ref):
    x = x_ref[...]
    ms = jnp.mean(x * x, axis=-1, keepdims=True)
    o_ref[...] = (x * lax.rsqrt(ms + EPS) * g_ref[...]).astype(o_ref.dtype)


def _rmsnorm(x, g, tm):
    m, d = x.shape
    return pl.pallas_call(
        _rmsnorm_kernel,
        grid=(m // tm,),
        in_specs=[pl.BlockSpec((tm, d), lambda i: (i, 0)),
                  pl.BlockSpec((1, d), lambda i: (0, 0))],
        out_specs=pl.BlockSpec((tm, d), lambda i: (i, 0)),
        out_shape=jax.ShapeDtypeStruct((m, d), BF16),
        compiler_params=_cparams(("parallel",)),
        name="rmsnorm",
    )(x, g.reshape(1, d))


def _mm_nt_kernel(a_ref, bt_ref, o_ref):
    o_ref[...] = lax.dot_general(a_ref[...], bt_ref[...], (((1,), (1,)), ((), ())),
                                 preferred_element_type=F32).astype(o_ref.dtype)


def _mm_nt_ride_kernel(a_ref, bt_ref, w_ref, o_ref, w_o_ref):
    _mm_nt_kernel(a_ref, bt_ref, o_ref)
    w_o_ref[...] = w_ref[...].astype(w_o_ref.dtype)


def _matmul_nt(a, bt, tm, tn, out_dtype, name, ride=None, n=None):
    m, k = a.shape
    n = bt.shape[0] if n is None else n
    assert bt.shape[1] == k and n % tn == 0 and n <= bt.shape[0]
    ni, nj = m // tm, n // tn
    in_specs = [pl.BlockSpec((tm, k), lambda i, j: (i, 0)),
                pl.BlockSpec((tn, k), lambda i, j: (j, 0))]
    out_spec = pl.BlockSpec((tm, tn), lambda i, j: (i, j))
    out_shape = jax.ShapeDtypeStruct((m, n), out_dtype)
    if ride is None:
        return pl.pallas_call(
            _mm_nt_kernel, grid=(ni, nj), in_specs=in_specs, out_specs=out_spec,
            out_shape=out_shape, compiler_params=_cparams(("parallel", "arbitrary")),
            name=name)(a, bt)
    rr, rc = ride.shape
    assert rr % (ni * nj) == 0
    slab = pl.BlockSpec((rr // (ni * nj), rc), lambda i, j: (i * nj + j, 0))
    return pl.pallas_call(
        _mm_nt_ride_kernel, grid=(ni, nj), in_specs=in_specs + [slab],
        out_specs=[out_spec, slab],
        out_shape=[out_shape, jax.ShapeDtypeStruct((rr, rc), BF16)],
        compiler_params=_cparams(("arbitrary", "arbitrary")),
        name=name)(a, bt, ride)


def _proj_t(wt, blk, rows, a, tm, name):
    m, k = a.shape
    return pl.pallas_call(
        _mm_nt_kernel,
        grid=(m // tm,),
        in_specs=[pl.BlockSpec((rows, k), lambda i: (blk, 0), pipeline_mode=pl.Buffered(1)),
                  pl.BlockSpec((tm, k), lambda i: (i, 0))],
        out_specs=pl.BlockSpec((rows, tm), lambda i: (0, i)),
        out_shape=jax.ShapeDtypeStruct((rows, m), BF16),
        compiler_params=_cparams(("parallel",)),
        name=name,
    )(wt, a)


def _latent_norm(c, g):
    ms = jnp.mean(c * c, axis=-1, keepdims=True)
    return (c * lax.rsqrt(ms + EPS) * g).astype(BF16)


ROPE_HALF = MLA_ROPE // 2


def _mla_rope(y, cos, sin):
    return y * cos + pltpu.roll(y, 64, 1) * sin


def _q_prep_kernel(z_ref, gl_ref, w_ref, ga_ref, gb_ref, cos_ref, sin_ref, o_ref):
    ga, gb = ga_ref[...], gb_ref[...]
    half = z_ref.shape[0] // 2
    for rows in (slice(0, half), slice(half, 2 * half)):
        c = z_ref[rows, :MLA_Q_LORA].astype(F32)
        lat = _latent_norm(c, gl_ref[...])
        q = jnp.dot(lat, w_ref[...], preferred_element_type=F32)
        cos, sin = cos_ref[rows, :], sin_ref[rows, :]
        for h in range(MLA_HEADS):
            lo = h * MLA_HEAD_PAD
            qa = q[:, lo:lo + LANES]
            qb = q[:, lo + LANES:lo + 2 * LANES]
            ss = jnp.sum(qa * qa + qb * qb, axis=-1, keepdims=True)
            r = lax.rsqrt(ss * (1.0 / MLA_QK) + EPS)
            ya = qa * r * ga
            yb = _mla_rope(qb * r * gb, cos, sin)
            o_ref[rows, lo:lo + LANES] = (ya * MLA_QSCALE).astype(o_ref.dtype)
            o_ref[rows, lo + LANES:lo + 2 * LANES] = (yb * MLA_QSCALE).astype(o_ref.dtype)


def _kv_prep_kernel(z_ref, gl_ref, wk_ref, wvt_ref, ga_ref, gb_ref, cos_ref, sin_ref,
                    k_ref, vt_ref):
    c = z_ref[:, MLA_Q_LORA:Z_KPE].astype(F32)
    lat = _latent_norm(c, gl_ref[...])
    kn = jnp.dot(lat, wk_ref[...], preferred_element_type=F32)
    vt_ref[...] = lax.dot_general(wvt_ref[...], lat, (((1,), (1,)), ((), ())),
                                  preferred_element_type=F32).astype(vt_ref.dtype)
    lane = lax.broadcasted_iota(jnp.int32, (1, LANES), 1)
    raw = z_ref[:, Z_KPE:Z_KPE + LANES].astype(F32)
    t1 = jnp.where(lane < ROPE_HALF, raw, 0.0)
    t2 = jnp.where((lane >= ROPE_HALF) & (lane < MLA_ROPE), raw, 0.0)
    kpe = t1 + pltpu.roll(t2, ROPE_HALF, 1)
    ss_pe = jnp.sum(kpe * kpe, axis=-1, keepdims=True)
    pe_rot = _mla_rope(kpe * gb_ref[...], cos_ref[...], sin_ref[...])
    ga = ga_ref[...]
    for h in range(MLA_HEADS):
        a = kn[:, h * LANES:(h + 1) * LANES]
        ss = jnp.sum(a * a, axis=-1, keepdims=True) + ss_pe
        r = lax.rsqrt(ss * (1.0 / MLA_QK) + EPS)
        lo = h * MLA_HEAD_PAD
        k_ref[:, lo:lo + LANES] = (a * r * ga).astype(k_ref.dtype)
        k_ref[:, lo + LANES:lo + 2 * LANES] = (pe_rot * r).astype(k_ref.dtype)


def _row_spec(tm, n, col=0):
    return pl.BlockSpec((tm, n), lambda i: (i, col))


def _const_spec(shape):
    return pl.BlockSpec(shape, lambda i: (0, 0))


def _q_prep(z, gl, w, ga, gb, cos, sin, tm):
    m = z.shape[0]
    n = MLA_HEADS * MLA_HEAD_PAD
    return pl.pallas_call(
        _q_prep_kernel,
        grid=(m // tm,),
        in_specs=[_row_spec(tm, Z_LAT), _const_spec(gl.shape), _const_spec(w.shape),
                  _const_spec(ga.shape), _const_spec(gb.shape),
                  _row_spec(tm, LANES), _row_spec(tm, LANES)],
        out_specs=_row_spec(tm, n),
        out_shape=jax.ShapeDtypeStruct((m, n), BF16),
        compiler_params=_cparams(("parallel",)),
        name="mla_q_prep",
    )(z, gl, w, ga, gb, cos, sin)


def _kv_prep(z, gl, wk, wvt, ga, gb, cos, sin, tm):
    m = z.shape[0]
    nk = MLA_HEADS * MLA_HEAD_PAD
    nv = MLA_HEADS * MLA_V
    return pl.pallas_call(
        _kv_prep_kernel,
        grid=(m // tm,),
        in_specs=[_row_spec(tm, Z_LAT), _const_spec(gl.shape), _const_spec(wk.shape),
                  _const_spec(wvt.shape), _const_spec(ga.shape), _const_spec(gb.shape),
                  _row_spec(tm, LANES), _row_spec(tm, LANES)],
        out_specs=[_row_spec(tm, nk), pl.BlockSpec((nv, tm), lambda i: (0, i))],
        out_shape=[jax.ShapeDtypeStruct((m, nk), BF16), jax.ShapeDtypeStruct((nv, m), BF16)],
        compiler_params=_cparams(("parallel",)),
        name="mla_kv_prep",
    )(z, gl, wk, wvt, ga, gb, cos, sin)


DIFF_PREP_COLS = 1024


def _diff_prep_kernel(x_ref, g_ref, cos_ref, sin_ref, o_ref):
    cos, sin = cos_ref[...], sin_ref[...]
    for c in range(DIFF_PREP_COLS // LANES):
        sl = slice(c * LANES, (c + 1) * LANES)
        x = x_ref[:, sl].astype(F32)
        ms = jnp.mean(x * x, axis=-1, keepdims=True)
        y = x * lax.rsqrt(ms + EPS) * g_ref[:, sl]
        o_ref[:, sl] = (y * cos + pltpu.roll(y, 64, 1) * sin).astype(o_ref.dtype)


def _diff_prep(z, g, cos, sin, tm):
    m = z.shape[0]
    n = 2 * DIFF_QK_COLS
    nb = n // DIFF_PREP_COLS
    return pl.pallas_call(
        _diff_prep_kernel,
        grid=(m // tm, nb),
        in_specs=[pl.BlockSpec((tm, DIFF_PREP_COLS), lambda i, j: (i, j)),
                  pl.BlockSpec((1, DIFF_PREP_COLS), lambda i, j: (0, j)),
                  pl.BlockSpec((tm, LANES), lambda i, j: (i, 0)),
                  pl.BlockSpec((tm, LANES), lambda i, j: (i, 0))],
        out_specs=pl.BlockSpec((tm, DIFF_PREP_COLS), lambda i, j: (i, j)),
        out_shape=jax.ShapeDtypeStruct((m, n), BF16),
        compiler_params=_cparams(("parallel", "arbitrary")),
        name="diff_qk_prep",
    )(z, g, cos, sin)


def _scores_t(k, q):
    return lax.dot_general(k, q, (((1,), (1,)), ((), ())), preferred_element_type=F32)


class _Buf(NamedTuple):
    s: Any
    bmax: Any

    def cols(self, sl):
        return _Buf(self.s.at[:, sl], self.bmax.at[:, sl])


def _scores_into(buf, k, q):
    s = _scores_t(k, q)
    buf.s[...] = s
    buf.bmax[...] = jnp.max(s, axis=0, keepdims=True)


def _mask_diag_block(buf, c0, tk):
    upper = lax.broadcasted_iota(jnp.int32, (CHUNK, LANES), 1) >= CHUNK
    for g in range(tk // LANES):
        cols = pl.ds(c0 + g * LANES, LANES)
        r0 = g * LANES + CHUNK
        edge = jnp.where(upper, buf.s[r0:r0 + CHUNK, cols], NEG)
        buf.s[r0:r0 + CHUNK, cols] = edge
        if r0 + CHUNK < tk:
            buf.s[r0 + CHUNK:tk, cols] = jnp.full((tk - r0 - CHUNK, LANES), NEG, F32)
        buf.bmax[:, cols] = jnp.maximum(jnp.max(buf.s[0:r0, cols], axis=0, keepdims=True),
                                        jnp.max(edge, axis=0, keepdims=True))


ONES_ROWS = 16


def _online_update_t(buf, vt, m_sc, acc_sc):
    tk, tq = buf.s.shape
    m_prev = m_sc[...]
    m_new = jnp.maximum(m_prev, buf.bmax[...])
    alpha = jnp.exp2(m_prev - m_new)
    p_t = jnp.exp2(buf.s[...] - m_new)
    v_aug = jnp.concatenate([vt, jnp.ones((ONES_ROWS, tk), BF16)], axis=0)
    pv = jnp.dot(v_aug, p_t.astype(BF16), preferred_element_type=F32)
    acc_sc[...] = alpha * acc_sc[...] + pv
    m_sc[...] = m_new


class _Chain(NamedTuple):
    q: Any
    k_blk: Any
    vt_blk: Any
    m: Any
    acc: Any
    buf_a: Any
    buf_b: Any


def _flash_pairs(i, tq, tk, chains):
    assert tq == 2 * tk

    def k_start(j):
        return pl.multiple_of(j * tk, tk)

    def qk(use_b, j):
        k0 = k_start(j)
        for c in chains:
            _scores_into(c.buf_b if use_b else c.buf_a, c.k_blk(k0), c.q)

    def update(use_b, j):
        k0 = k_start(j)
        for c in chains:
            _online_update_t(c.buf_b if use_b else c.buf_a, c.vt_blk(k0), c.m, c.acc)

    for c in chains:
        c.m[...] = jnp.full(c.m.shape, NEG, F32)
        c.acc[...] = jnp.zeros(c.acc.shape, F32)

    qk(False, 0)

    def body(p, carry):
        qk(True, 2 * p + 1)
        update(False, 2 * p)
        qk(False, 2 * p + 2)
        update(True, 2 * p + 1)
        return carry

    lax.fori_loop(0, i, body, 0)
    half = pl.ds(tk, tk)
    k_last = k_start(2 * i + 1)
    for c in chains:
        _mask_diag_block(c.buf_a, 0, tk)
        _scores_into(c.buf_b.cols(half), c.k_blk(k_last), c.q[tk:, :])
        _mask_diag_block(c.buf_b, tk, tk)
    update(False, 2 * i)
    for c in chains:
        _online_update_t(c.buf_b.cols(half), c.vt_blk(k_last), c.m.at[:, half],
                         c.acc.at[:, half])


def _score_scratch(n, tq, tk):
    one = [pltpu.VMEM((n, tk, tq), F32), pltpu.VMEM((n, 1, tq), F32)]
    return one + one


def _bufs(refs, idx):
    sa, ma, sb, mb = refs
    return _Buf(sa.at[idx], ma.at[idx]), _Buf(sb.at[idx], mb.at[idx])


MLA_HEADS_PER_STEP = 2


def _mla_attn_kernel(q_ref, k_ref, vt_ref, o_ref, m_sc, acc_sc, *score_refs, tq, tk):
    nh = m_sc.shape[0]
    chains = []
    for h in range(nh):
        qk_cols = slice(h * MLA_HEAD_PAD, (h + 1) * MLA_HEAD_PAD)
        v_rows = slice(h * MLA_V, (h + 1) * MLA_V)
        buf_a, buf_b = _bufs(score_refs, h)
        chains.append(_Chain(
            q=q_ref[:, qk_cols],
            k_blk=functools.partial(lambda k0, cols: k_ref[pl.ds(k0, tk), cols], cols=qk_cols),
            vt_blk=functools.partial(lambda k0, rows: vt_ref[rows, pl.ds(k0, tk)], rows=v_rows),
            m=m_sc.at[h], acc=acc_sc.at[h], buf_a=buf_a, buf_b=buf_b))
    _flash_pairs(pl.program_id(1), tq, tk, chains)
    for h in range(nh):
        acc = acc_sc[h]
        o = acc[:MLA_V] / acc[MLA_V:MLA_V + 1]
        o_ref[:, h * MLA_V:(h + 1) * MLA_V] = o.T.astype(o_ref.dtype)


def _mla_attn(q, k, vt, tq, tk):
    s = q.shape[0]
    nh = MLA_HEADS_PER_STEP
    kern = functools.partial(_mla_attn_kernel, tq=tq, tk=tk)
    return pl.pallas_call(
        kern,
        grid=(MLA_HEADS // nh, s // tq),
        in_specs=[pl.BlockSpec((tq, nh * MLA_HEAD_PAD), lambda h, i: (i, h)),
                  pl.BlockSpec((s, nh * MLA_HEAD_PAD), lambda h, i: (0, h)),
                  pl.BlockSpec((nh * MLA_V, s), lambda h, i: (h, 0))],
        out_specs=pl.BlockSpec((tq, nh * MLA_V), lambda h, i: (i, h)),
        out_shape=jax.ShapeDtypeStruct((s, MLA_HEADS * MLA_V), BF16),
        scratch_shapes=[pltpu.VMEM((nh, 1, tq), F32),
                        pltpu.VMEM((nh, MLA_V + ONES_ROWS, tq), F32)]
        + _score_scratch(nh, tq, tk),
        compiler_params=_cparams(("parallel", "arbitrary")),
        name="mla_attn",
    )(q, k, vt)


def _diff_attn_kernel(q_ref, k_ref, vt_ref, lq1_ref, lk1_ref, lq2_ref, lk2_ref, g_ref, o_ref,
                      m_sc, acc_sc, *score_refs, tq, tk):
    chains = []
    for t in range(2):
        cols = slice(t * DIFF_HEAD_DIM, (t + 1) * DIFF_HEAD_DIM)
        buf_a, buf_b = _bufs(score_refs, t)
        chains.append(_Chain(
            q=q_ref[:, cols],
            k_blk=functools.partial(lambda k0, cols: k_ref[pl.ds(k0, tk), cols], cols=cols),
            vt_blk=lambda k0: vt_ref[:, pl.ds(k0, tk)],
            m=m_sc.at[t], acc=acc_sc.at[t], buf_a=buf_a, buf_b=buf_b))
    _flash_pairs(pl.program_id(1), tq, tk, chains)

    lam = (jnp.exp(jnp.sum(lq1_ref[...] * lk1_ref[...], axis=-1, keepdims=True))
           - jnp.exp(jnp.sum(lq2_ref[...] * lk2_ref[...], axis=-1, keepdims=True))
           + LAMBDA_INIT)
    a1, a2 = acc_sc[0], acc_sc[1]
    o = (a1[:DIFF_V] / a1[DIFF_V:DIFF_V + 1]
         - lam * (a2[:DIFF_V] / a2[DIFF_V:DIFF_V + 1])).T
    ms = jnp.mean(o * o, axis=-1, keepdims=True)
    o = o * lax.rsqrt(ms + EPS) * g_ref[...]
    o_ref[...] = (o * (1.0 - LAMBDA_INIT)).astype(o_ref.dtype)


def _diff_attn(qk, vt, lq1, lk1, lq2, lk2, g, tq, tk):
    s = qk.shape[0]
    kern = functools.partial(_diff_attn_kernel, tq=tq, tk=tk)
    k_off = DIFF_QK_COLS // DIFF_V
    vec = pl.BlockSpec((1, DIFF_HEAD_DIM), lambda h, i: (0, 0))
    return pl.pallas_call(
        kern,
        grid=(DIFF_HEADS, s // tq),
        in_specs=[pl.BlockSpec((tq, DIFF_V), lambda h, i: (i, h)),
                  pl.BlockSpec((s, DIFF_V), lambda h, i: (0, h + k_off)),
                  pl.BlockSpec((DIFF_V, s), lambda h, i: (h, 0)),
                  vec, vec, vec, vec,
                  pl.BlockSpec((1, DIFF_V), lambda h, i: (0, 0))],
        out_specs=pl.BlockSpec((tq, DIFF_V), lambda h, i: (i, h)),
        out_shape=jax.ShapeDtypeStruct((s, DIFF_HEADS * DIFF_V), BF16),
        scratch_shapes=[pltpu.VMEM((2, 1, tq), F32),
                        pltpu.VMEM((2, DIFF_V + ONES_ROWS, tq), F32)]
        + _score_scratch(2, tq, tk),
        compiler_params=_cparams(("parallel", "arbitrary")),
        name="diff_attn",
    )(qk, qk, vt, lq1, lk1, lq2, lk2, g)


def _wo_kernel(oa_ref, ob_ref, wa_ref, wb_ref, x_ref, h_ref):
    acc = jnp.dot(oa_ref[...], wa_ref[...], preferred_element_type=F32)
    acc = acc + jnp.dot(ob_ref[...], wb_ref[...], preferred_element_type=F32)
    h_ref[...] = x_ref[...] + acc


def _wo_proj(oa, ob, w, x, tm, tn):
    m, ka = oa.shape
    kb = ob.shape[1]
    assert ka == kb and w.shape[0] == ka + kb
    n = w.shape[1]
    return pl.pallas_call(
        _wo_kernel,
        grid=(m // tm, n // tn),
        in_specs=[pl.BlockSpec((tm, ka), lambda i, j: (i, 0)),
                  pl.BlockSpec((tm, kb), lambda i, j: (i, 0)),
                  pl.BlockSpec((ka, tn), lambda i, j: (0, j)),
                  pl.BlockSpec((kb, tn), lambda i, j: (1, j)),
                  pl.BlockSpec((tm, tn), lambda i, j: (i, j))],
        out_specs=pl.BlockSpec((tm, tn), lambda i, j: (i, j)),
        out_shape=jax.ShapeDtypeStruct((m, n), F32),
        compiler_params=_cparams(("parallel", "arbitrary")),
        name="wo_proj",
    )(oa, ob, w, w, x)


def _gate_up_kernel(a_ref, wg_ref, wu_ref, wd_ref, o_ref, wd_o_ref):
    a = a_ref[...]
    g = jnp.dot(a, wg_ref[...].astype(BF16), preferred_element_type=F32)
    u = jnp.dot(a, wu_ref[...].astype(BF16), preferred_element_type=F32)
    hg = 0.5 * g
    o_ref[...] = (hg * (1.0 + jnp.tanh(hg)) * u).astype(o_ref.dtype)
    wd_o_ref[...] = wd_ref[...].astype(wd_o_ref.dtype)


def _gate_up(a, wg, wu, wd, tm, tn):
    m, k = a.shape
    n = wg.shape[1]
    ni, nj = m // tm, n // tn
    kd, nd = wd.shape
    assert kd % (ni * nj) == 0
    slab = kd // (ni * nj)
    return pl.pallas_call(
        _gate_up_kernel,
        grid=(ni, nj),
        in_specs=[pl.BlockSpec((tm, k), lambda i, j: (i, 0), pipeline_mode=pl.Buffered(1)),
                  pl.BlockSpec((k, tn), lambda i, j: (0, j)),
                  pl.BlockSpec((k, tn), lambda i, j: (0, j)),
                  pl.BlockSpec((slab, nd), lambda i, j: (i * nj + j, 0))],
        out_specs=[pl.BlockSpec((tm, tn), lambda i, j: (i, j)),
                   pl.BlockSpec((slab, nd), lambda i, j: (i * nj + j, 0))],
        out_shape=[jax.ShapeDtypeStruct((m, n), BF16), jax.ShapeDtypeStruct((kd, nd), BF16)],
        compiler_params=_cparams(("arbitrary", "arbitrary")),
        name="ffn_gate_up",
    )(a, wg, wu, wd)


def _down_kernel(a_ref, w_ref, h_ref, o_ref):
    o_ref[...] = h_ref[...] + jnp.dot(a_ref[...], w_ref[...], preferred_element_type=F32)


def _down_proj(a, w, h, tm, tn):
    m, k = a.shape
    n = w.shape[1]
    return pl.pallas_call(
        _down_kernel,
        grid=(m // tm, n // tn),
        in_specs=[pl.BlockSpec((tm, k), lambda i, j: (i, 0)),
                  pl.BlockSpec((k, tn), lambda i, j: (0, j)),
                  pl.BlockSpec((tm, tn), lambda i, j: (i, j))],
        out_specs=pl.BlockSpec((tm, tn), lambda i, j: (i, j)),
        out_shape=jax.ShapeDtypeStruct((m, n), F32),
        compiler_params=_cparams(("parallel", "arbitrary")),
        name="ffn_down",
    )(a, w, h)


def _rope_tables(s):
    pos = np.arange(s, dtype=np.float64)

    def cs(half):
        inv_freq = ROPE_THETA ** (-np.arange(half, dtype=np.float64) / half)
        ang = pos[:, None] * inv_freq[None, :]
        return np.cos(ang), np.sin(ang)

    c, sn = cs(ROPE_HALF)
    gap = np.zeros_like(c)
    mla_cos = np.concatenate([c, gap, c, gap], axis=-1)
    mla_sin = np.concatenate([-sn, gap, sn, gap], axis=-1)
    c, sn = cs(DIFF_HEAD_DIM // 2)
    diff_cos = np.concatenate([c, c], axis=-1)
    diff_sin = np.concatenate([-sn, sn], axis=-1)
    return tuple(jnp.asarray(t.astype(np.float32))
                 for t in (mla_cos, mla_sin, diff_cos, diff_sin))


def _pad_gain(g, lo, hi, width):
    return jnp.pad(g[lo:hi], (0, width - (hi - lo))).reshape(1, width)


def _spread_rope(t, axis):
    t1, t2 = jnp.split(t, 2, axis=axis)
    gap = jnp.zeros_like(t1)
    return jnp.concatenate([t1, gap, t2, gap], axis=axis)


def kernel(x, attn_norm_g, w_in, q_latent_norm_g, kv_latent_norm_g, w_uq, w_ukv, mla_q_norm_g, mla_k_norm_g, diff_q_norm_g, diff_k_norm_g, lambda_q1, lambda_k1, lambda_q2, lambda_k2, diff_subln_g, w_o, ffn_norm_g, w_gate, w_up, w_down):
    b, s, d = x.shape
    assert b == 1 and d == D_MODEL and s % 512 == 0
    x2 = x.reshape(s, d)
    tm = min(1024, s)
    tp = min(512, s)
    tq = min(1024, s)
    tk = tq // 2

    w_in_t = jnp.swapaxes(w_in[0], 0, 1)
    n_diff = 2 * DIFF_QK_COLS + DIFF_V_COLS
    w_lat_t = _cast_rows_bf16(w_in_t, 0, Z_LAT, Z_LAT // 2)
    w_diff_t = _cast_rows_bf16(w_in_t, Z_LAT_USED, n_diff, math.gcd(Z_LAT_USED, n_diff))
    w_uq3 = w_uq[0].reshape(MLA_Q_LORA, MLA_HEADS, MLA_QK)
    w_uq_p = jnp.concatenate([w_uq3[:, :, :MLA_NOPE], _spread_rope(w_uq3[:, :, MLA_NOPE:], 2)],
                             axis=2).reshape(MLA_Q_LORA, MLA_HEADS * MLA_HEAD_PAD).astype(BF16)
    w_ukv3 = w_ukv[0].reshape(MLA_KV_LORA, MLA_HEADS, MLA_NOPE + MLA_V)
    w_uk = w_ukv3[:, :, :MLA_NOPE].reshape(MLA_KV_LORA, MLA_HEADS * MLA_NOPE).astype(BF16)
    w_uv_t = w_ukv3[:, :, MLA_NOPE:].reshape(MLA_KV_LORA, MLA_HEADS * MLA_V).T.astype(BF16)

    mla_cos, mla_sin, diff_cos, diff_sin = _rope_tables(s)
    gq_a = _pad_gain(mla_q_norm_g[0], 0, MLA_NOPE, LANES)
    gq_b = _spread_rope(mla_q_norm_g[0][MLA_NOPE:], 0).reshape(1, LANES)
    gk_a = _pad_gain(mla_k_norm_g[0], 0, MLA_NOPE, LANES)
    gk_b = _spread_rope(mla_k_norm_g[0][MLA_NOPE:], 0).reshape(1, LANES)
    g_diff = jnp.concatenate([jnp.tile(diff_q_norm_g[0] * DIFF_QSCALE, 2 * DIFF_HEADS),
                              jnp.tile(diff_k_norm_g[0], 2 * DIFF_HEADS)]).reshape(1, -1)

    n = _rmsnorm(x2, attn_norm_g[0], min(512, s))
    z_lat = _matmul_nt(n, w_lat_t, tm, Z_LAT // 2, BF16, "in_proj_lat")
    z_qk, w_o_b = _matmul_nt(n, w_diff_t, tm, 1024, BF16, "in_proj_diff",
                             ride=w_o[0], n=2 * DIFF_QK_COLS)
    dv_t = _proj_t(w_diff_t, 2 * DIFF_QK_COLS // DIFF_V_COLS, DIFF_V_COLS, n, tp,
                   "in_proj_dv_t")
    q_a = _q_prep(z_lat, q_latent_norm_g[0].reshape(1, -1), w_uq_p, gq_a, gq_b,
                  mla_cos, mla_sin, tm)
    k_a, vt_a = _kv_prep(z_lat, kv_latent_norm_g[0].reshape(1, -1), w_uk, w_uv_t, gk_a, gk_b,
                        mla_cos, mla_sin, tp)
    o_a = _mla_attn(q_a, k_a, vt_a, tq, tk)
    qk_b = _diff_prep(z_qk, g_diff, diff_cos, diff_sin, tm)
    o_b = _diff_attn(qk_b, dv_t, lambda_q1[0].reshape(1, -1), lambda_k1[0].reshape(1, -1),
                     lambda_q2[0].reshape(1, -1), lambda_k2[0].reshape(1, -1),
                     diff_subln_g[0].reshape(1, -1), tq, tk)
    h = _wo_proj(o_a, o_b, w_o_b, x2, tm, 1024)

    m = _rmsnorm(h, ffn_norm_g[0], min(512, s))
    a, w_down_b = _gate_up(m, w_gate[0], w_up[0], w_down[0], min(2048, s), 256)
    out = _down_proj(a, w_down_b, h, min(512, s), 512)
    return out.reshape(b, s, d)
```

```python
import functools
import math
from typing import Any, NamedTuple

import jax
import jax.numpy as jnp
import numpy as np
from jax import lax
from jax.experimental import pallas as pl
from jax.experimental.pallas import tpu as pltpu

F32 = jnp.float32
BF16 = jnp.bfloat16

D_MODEL = 4096
CHUNK = 64
ROPE_THETA = 10000.0
EPS = 1e-6
MLA_HEADS = 16
MLA_Q_LORA = 768
MLA_KV_LORA = 512
MLA_NOPE = 128
MLA_ROPE = 64
MLA_QK = MLA_NOPE + MLA_ROPE
MLA_V = 128
LOG2E = math.log2(math.e)
MLA_SCALE = 1.0 / math.sqrt(MLA_QK)
MLA_QSCALE = MLA_SCALE * LOG2E
DIFF_HEADS = 8
DIFF_HEAD_DIM = 128
DIFF_V = 2 * DIFF_HEAD_DIM
DIFF_SCALE = 1.0 / math.sqrt(DIFF_HEAD_DIM)
DIFF_QSCALE = DIFF_SCALE * LOG2E
DIFF_QK_COLS = 2 * DIFF_HEADS * DIFF_HEAD_DIM
DIFF_V_COLS = DIFF_HEADS * DIFF_V
D_FF = -(-8 * D_MODEL // (3 * 256)) * 256
LAMBDA_INIT = 0.8 - 0.6 * math.exp(-0.3 * 0)

LANES = 128
MLA_HEAD_PAD = 2 * LANES
Z_LAT = 1536
Z_KPE = MLA_Q_LORA + MLA_KV_LORA
Z_LAT_USED = Z_KPE + MLA_ROPE
NEG = -1e30
VMEM_LIMIT = 56 * 1024 * 1024


def _cparams(sem):
    return pltpu.CompilerParams(dimension_semantics=sem, vmem_limit_bytes=VMEM_LIMIT)


def _cast_kernel(w_ref, o_ref):
    o_ref[...] = w_ref[...].astype(o_ref.dtype)


def _cast_rows_bf16(w, row0, n_rows, tr):
    c = w.shape[1]
    assert row0 % tr == 0 and n_rows % tr == 0 and row0 + n_rows <= w.shape[0]
    blk0 = row0 // tr
    return pl.pallas_call(
        _cast_kernel,
        grid=(n_rows // tr,),
        in_specs=[pl.BlockSpec((tr, c), lambda i: (i + blk0, 0))],
        out_specs=pl.BlockSpec((tr, c), lambda i: (i, 0)),
        out_shape=jax.ShapeDtypeStruct((n_rows, c), BF16),
        compiler_params=_cparams(("parallel",)),
        name="cast_bf16",
    )(w)


def _rmsnorm_kernel(x_ref, g_ref, o_ref):
    x = x_ref[...]
    ms = jnp.mean(x * x, axis=-1, keepdims=True)
    o_ref[...] = (x * lax.rsqrt(ms + EPS) * g_ref[...]).astype(o_ref.dtype)


def _rmsnorm(x, g, tm):
    m, d = x.shape
    return pl.pallas_call(
        _rmsnorm_kernel,
        grid=(m // tm,),
        in_specs=[pl.BlockSpec((tm, d), lambda i: (i, 0)),
                  pl.BlockSpec((1, d), lambda i: (0, 0))],
        out_specs=pl.BlockSpec((tm, d), lambda i: (i, 0)),
        out_shape=jax.ShapeDtypeStruct((m, d), BF16),
        compiler_params=_cparams(("parallel",)),
        name="rmsnorm",
    )(x, g.reshape(1, d))


def _mm_nt_kernel(a_ref, bt_ref, o_ref):
    o_ref[...] = lax.dot_general(a_ref[...], bt_ref[...], (((1,), (1,)), ((), ())),
                                 preferred_element_type=F32).astype(o_ref.dtype)


def _mm_nt_ride_kernel(a_ref, bt_ref, w_ref, o_ref, w_o_ref):
    _mm_nt_kernel(a_ref, bt_ref, o_ref)
    w_o_ref[...] = w_ref[...].astype(w_o_ref.dtype)


def _matmul_nt(a, bt, tm, tn, out_dtype, name, ride=None, n=None):
    m, k = a.shape
    n = bt.shape[0] if n is None else n
    assert bt.shape[1] == k and n % tn == 0 and n <= bt.shape[0]
    ni, nj = m // tm, n // tn
    in_specs = [pl.BlockSpec((tm, k), lambda i, j: (i, 0)),
                pl.BlockSpec((tn, k), lambda i, j: (j, 0))]
    out_spec = pl.BlockSpec((tm, tn), lambda i, j: (i, j))
    out_shape = jax.ShapeDtypeStruct((m, n), out_dtype)
    if ride is None:
        return pl.pallas_call(
            _mm_nt_kernel, grid=(ni, nj), in_specs=in_specs, out_specs=out_spec,
            out_shape=out_shape, compiler_params=_cparams(("parallel", "arbitrary")),
            name=name)(a, bt)
    rr, rc = ride.shape
    assert rr % (ni * nj) == 0
    slab = pl.BlockSpec((rr // (ni * nj), rc), lambda i, j: (i * nj + j, 0))
    return pl.pallas_call(
        _mm_nt_ride_kernel, grid=(ni, nj), in_specs=in_specs + [slab],
        out_specs=[out_spec, slab],
        out_shape=[out_shape, jax.ShapeDtypeStruct((rr, rc), BF16)],
        compiler_params=_cparams(("arbitrary", "arbitrary")),
        name=name)(a, bt, ride)


def _proj_t(wt, blk, rows, a, tm, name):
    m, k = a.shape
    return pl.pallas_call(
        _mm_nt_kernel,
        grid=(m // tm,),
        in_specs=[pl.BlockSpec((rows, k), lambda i: (blk, 0), pipeline_mode=pl.Buffered(1)),
                  pl.BlockSpec((tm, k), lambda i: (i, 0))],
        out_specs=pl.BlockSpec((rows, tm), lambda i: (0, i)),
        out_shape=jax.ShapeDtypeStruct((rows, m), BF16),
        compiler_params=_cparams(("parallel",)),
        name=name,
    )(wt, a)


def _latent_norm(c, g):
    ms = jnp.mean(c * c, axis=-1, keepdims=True)
    return (c * lax.rsqrt(ms + EPS) * g).astype(BF16)


ROPE_HALF = MLA_ROPE // 2


def _mla_rope(y, cos, sin):
    return y * cos + pltpu.roll(y, 64, 1) * sin


def _q_prep_kernel(z_ref, gl_ref, w_ref, ga_ref, gb_ref, cos_ref, sin_ref, o_ref):
    ga, gb = ga_ref[...], gb_ref[...]
    half = z_ref.shape[0] // 2
    for rows in (slice(0, half), slice(half, 2 * half)):
        c = z_ref[rows, :MLA_Q_LORA].astype(F32)
        lat = _latent_norm(c, gl_ref[...])
        q = jnp.dot(lat, w_ref[...], preferred_element_type=F32)
        cos, sin = cos_ref[rows, :], sin_ref[rows, :]
        for h in range(MLA_HEADS):
            lo = h * MLA_HEAD_PAD
            qa = q[:, lo:lo + LANES]
            qb = q[:, lo + LANES:lo + 2 * LANES]
            ss = jnp.sum(qa * qa + qb * qb, axis=-1, keepdims=True)
            r = lax.rsqrt(ss * (1.0 / MLA_QK) + EPS)
            ya = qa * r * ga
            yb = _mla_rope(qb * r * gb, cos, sin)
            o_ref[rows, lo:lo + LANES] = (ya * MLA_QSCALE).astype(o_ref.dtype)
            o_ref[rows, lo + LANES:lo + 2 * LANES] = (yb * MLA_QSCALE).astype(o_ref.dtype)


def _kv_prep_kernel(z_ref, gl_ref, wk_ref, wvt_ref, ga_ref, gb_ref, cos_ref, sin_ref,
                    k_ref, vt_ref):
    c = z_ref[:, MLA_Q_LORA:Z_KPE].astype(F32)
    lat = _latent_norm(c, gl_ref[...])
    kn = jnp.dot(lat, wk_ref[...], preferred_element_type=F32)
    vt_ref[...] = lax.dot_general(wvt_ref[...], lat, (((1,), (1,)), ((), ())),
                                  preferred_element_type=F32).astype(vt_ref.dtype)
    lane = lax.broadcasted_iota(jnp.int32, (1, LANES), 1)
    raw = z_ref[:, Z_KPE:Z_KPE + LANES].astype(F32)
    t1 = jnp.where(lane < ROPE_HALF, raw, 0.0)
    t2 = jnp.where((lane >= ROPE_HALF) & (lane < MLA_ROPE), raw, 0.0)
    kpe = t1 + pltpu.roll(t2, ROPE_HALF, 1)
    ss_pe = jnp.sum(kpe * kpe, axis=-1, keepdims=True)
    pe_rot = _mla_rope(kpe * gb_ref[...], cos_ref[...], sin_ref[...])
    ga = ga_ref[...]
    for h in range(MLA_HEADS):
        a = kn[:, h * LANES:(h + 1) * LANES]
        ss = jnp.sum(a * a, axis=-1, keepdims=True) + ss_pe
        r = lax.rsqrt(ss * (1.0 / MLA_QK) + EPS)
        lo = h * MLA_HEAD_PAD
        k_ref[:, lo:lo + LANES] = (a * r * ga).astype(k_ref.dtype)
        k_ref[:, lo + LANES:lo + 2 * LANES] = (pe_rot * r).astype(k_ref.dtype)


def _row_spec(tm, n, col=0):
    return pl.BlockSpec((tm, n), lambda i: (i, col))


def _const_spec(shape):
    return pl.BlockSpec(shape, lambda i: (0, 0))


def _q_prep(z, gl, w, ga, gb, cos, sin, tm):
    m = z.shape[0]
    n = MLA_HEADS * MLA_HEAD_PAD
    return pl.pallas_call(
        _q_prep_kernel,
        grid=(m // tm,),
        in_specs=[_row_spec(tm, Z_LAT), _const_spec(gl.shape), _const_spec(w.shape),
                  _const_spec(ga.shape), _const_spec(gb.shape),
                  _row_spec(tm, LANES), _row_spec(tm, LANES)],
        out_specs=_row_spec(tm, n),
        out_shape=jax.ShapeDtypeStruct((m, n), BF16),
        compiler_params=_cparams(("parallel",)),
        name="mla_q_prep",
    )(z, gl, w, ga, gb, cos, sin)


def _kv_prep(z, gl, wk, wvt, ga, gb, cos, sin, tm):
    m = z.shape[0]
    nk = MLA_HEADS * MLA_HEAD_PAD
    nv = MLA_HEADS * MLA_V
    return pl.pallas_call(
        _kv_prep_kernel,
        grid=(m // tm,),
        in_specs=[_row_spec(tm, Z_LAT), _const_spec(gl.shape), _const_spec(wk.shape),
                  _const_spec(wvt.shape), _const_spec(ga.shape), _const_spec(gb.shape),
                  _row_spec(tm, LANES), _row_spec(tm, LANES)],
        out_specs=[_row_spec(tm, nk), pl.BlockSpec((nv, tm), lambda i: (0, i))],
        out_shape=[jax.ShapeDtypeStruct((m, nk), BF16), jax.ShapeDtypeStruct((nv, m), BF16)],
        compiler_params=_cparams(("parallel",)),
        name="mla_kv_prep",
    )(z, gl, wk, wvt, ga, gb, cos, sin)


DIFF_PREP_COLS = 1024


def _diff_prep_kernel(x_ref, g_ref, cos_ref, sin_ref, o_ref):
    cos, sin = cos_ref[...], sin_ref[...]
    for c in range(DIFF_PREP_COLS // LANES):
        sl = slice(c * LANES, (c + 1) * LANES)
        x = x_ref[:, sl].astype(F32)
        ms = jnp.mean(x * x, axis=-1, keepdims=True)
        y = x * lax.rsqrt(ms + EPS) * g_ref[:, sl]
        o_ref[:, sl] = (y * cos + pltpu.roll(y, 64, 1) * sin).astype(o_ref.dtype)


def _diff_prep(z, g, cos, sin, tm):
    m = z.shape[0]
    n = 2 * DIFF_QK_COLS
    nb = n // DIFF_PREP_COLS
    return pl.pallas_call(
        _diff_prep_kernel,
        grid=(m // tm, nb),
        in_specs=[pl.BlockSpec((tm, DIFF_PREP_COLS), lambda i, j: (i, j)),
                  pl.BlockSpec((1, DIFF_PREP_COLS), lambda i, j: (0, j)),
                  pl.BlockSpec((tm, LANES), lambda i, j: (i, 0)),
                  pl.BlockSpec((tm, LANES), lambda i, j: (i, 0))],
        out_specs=pl.BlockSpec((tm, DIFF_PREP_COLS), lambda i, j: (i, j)),
        out_shape=jax.ShapeDtypeStruct((m, n), BF16),
        compiler_params=_cparams(("parallel", "arbitrary")),
        name="diff_qk_prep",
    )(z, g, cos, sin)


def _scores_t(k, q):
    return lax.dot_general(k, q, (((1,), (1,)), ((), ())), preferred_element_type=F32)


class _Buf(NamedTuple):
    s: Any
    bmax: Any

    def cols(self, sl):
        return _Buf(self.s.at[:, sl], self.bmax.at[:, sl])


def _scores_into(buf, k, q):
    s = _scores_t(k, q)
    buf.s[...] = s
    buf.bmax[...] = jnp.max(s, axis=0, keepdims=True)


def _mask_diag_block(buf, c0, tk):
    upper = lax.broadcasted_iota(jnp.int32, (CHUNK, LANES), 1) >= CHUNK
    for g in range(tk // LANES):
        cols = pl.ds(c0 + g * LANES, LANES)
        r0 = g * LANES + CHUNK
        edge = jnp.where(upper, buf.s[r0:r0 + CHUNK, cols], NEG)
        buf.s[r0:r0 + CHUNK, cols] = edge
        if r0 + CHUNK < tk:
            buf.s[r0 + CHUNK:tk, cols] = jnp.full((tk - r0 - CHUNK, LANES), NEG, F32)
        buf.bmax[:, cols] = jnp.maximum(jnp.max(buf.s[0:r0, cols], axis=0, keepdims=True),
                                        jnp.max(edge, axis=0, keepdims=True))


ONES_ROWS = 16


def _online_update_t(buf, vt, m_sc, acc_sc):
    tk, tq = buf.s.shape
    m_prev = m_sc[...]
    m_new = jnp.maximum(m_prev, buf.bmax[...])
    alpha = jnp.exp2(m_prev - m_new)
    p_t = jnp.exp2(buf.s[...] - m_new)
    v_aug = jnp.concatenate([vt, jnp.ones((ONES_ROWS, tk), BF16)], axis=0)
    pv = jnp.dot(v_aug, p_t.astype(BF16), preferred_element_type=F32)
    acc_sc[...] = alpha * acc_sc[...] + pv
    m_sc[...] = m_new


class _Chain(NamedTuple):
    q: Any
    k_blk: Any
    vt_blk: Any
    m: Any
    acc: Any
    buf_a: Any
    buf_b: Any


def _flash_pairs(i, tq, tk, chains):
    assert tq == 2 * tk

    def k_start(j):
        return pl.multiple_of(j * tk, tk)

    def qk(use_b, j):
        k0 = k_start(j)
        for c in chains:
            _scores_into(c.buf_b if use_b else c.buf_a, c.k_blk(k0), c.q)

    def update(use_b, j):
        k0 = k_start(j)
        for c in chains:
            _online_update_t(c.buf_b if use_b else c.buf_a, c.vt_blk(k0), c.m, c.acc)

    for c in chains:
        c.m[...] = jnp.full(c.m.shape, NEG, F32)
        c.acc[...] = jnp.zeros(c.acc.shape, F32)

    qk(False, 0)

    def body(p, carry):
        qk(True, 2 * p + 1)
        update(False, 2 * p)
        qk(False, 2 * p + 2)
        update(True, 2 * p + 1)
        return carry

    lax.fori_loop(0, i, body, 0)
    half = pl.ds(tk, tk)
    k_last = k_start(2 * i + 1)
    for c in chains:
        _mask_diag_block(c.buf_a, 0, tk)
        _scores_into(c.buf_b.cols(half), c.k_blk(k_last), c.q[tk:, :])
        _mask_diag_block(c.buf_b, tk, tk)
    update(False, 2 * i)
    for c in chains:
        _online_update_t(c.buf_b.cols(half), c.vt_blk(k_last), c.m.at[:, half],
                         c.acc.at[:, half])


def _score_scratch(n, tq, tk):
    one = [pltpu.VMEM((n, tk, tq + LANES), F32), pltpu.VMEM((n, 1, tq), F32)]
    return one + one


def _bufs(refs, idx):
    sa, ma, sb, mb = refs
    tq = ma.shape[-1]
    return (_Buf(sa.at[idx, :, pl.ds(0, tq)], ma.at[idx]),
            _Buf(sb.at[idx, :, pl.ds(0, tq)], mb.at[idx]))


MLA_HEADS_PER_STEP = 2


def _mla_attn_kernel(q_ref, k_ref, vt_ref, o_ref, m_sc, acc_sc, *score_refs, tq, tk):
    nh = m_sc.shape[0]
    chains = []
    for h in range(nh):
        qk_cols = slice(h * MLA_HEAD_PAD, (h + 1) * MLA_HEAD_PAD)
        v_rows = slice(h * MLA_V, (h + 1) * MLA_V)
        buf_a, buf_b = _bufs(score_refs, h)
        chains.append(_Chain(
            q=q_ref[:, qk_cols],
            k_blk=functools.partial(lambda k0, cols: k_ref[pl.ds(k0, tk), cols], cols=qk_cols),
            vt_blk=functools.partial(lambda k0, rows: vt_ref[rows, pl.ds(k0, tk)], rows=v_rows),
            m=m_sc.at[h], acc=acc_sc.at[h], buf_a=buf_a, buf_b=buf_b))
    _flash_pairs(pl.program_id(1), tq, tk, chains)
    for h in range(nh):
        acc = acc_sc[h]
        o = acc[:MLA_V] / acc[MLA_V:MLA_V + 1]
        o_ref[:, h * MLA_V:(h + 1) * MLA_V] = o.T.astype(o_ref.dtype)


def _mla_attn(q, k, vt, tq, tk):
    s = q.shape[0]
    nh = MLA_HEADS_PER_STEP
    kern = functools.partial(_mla_attn_kernel, tq=tq, tk=tk)
    return pl.pallas_call(
        kern,
        grid=(MLA_HEADS // nh, s // tq),
        in_specs=[pl.BlockSpec((tq, nh * MLA_HEAD_PAD), lambda h, i: (i, h)),
                  pl.BlockSpec((s, nh * MLA_HEAD_PAD), lambda h, i: (0, h)),
                  pl.BlockSpec((nh * MLA_V, s), lambda h, i: (h, 0))],
        out_specs=pl.BlockSpec((tq, nh * MLA_V), lambda h, i: (i, h)),
        out_shape=jax.ShapeDtypeStruct((s, MLA_HEADS * MLA_V), BF16),
        scratch_shapes=[pltpu.VMEM((nh, 1, tq), F32),
                        pltpu.VMEM((nh, MLA_V + ONES_ROWS, tq), F32)]
        + _score_scratch(nh, tq, tk),
        compiler_params=_cparams(("parallel", "arbitrary")),
        name="mla_attn",
    )(q, k, vt)


def _diff_attn_kernel(q_ref, k_ref, vt_ref, lq1_ref, lk1_ref, lq2_ref, lk2_ref, g_ref, o_ref,
                      m_sc, acc_sc, *score_refs, tq, tk):
    chains = []
    for t in range(2):
        cols = slice(t * DIFF_HEAD_DIM, (t + 1) * DIFF_HEAD_DIM)
        buf_a, buf_b = _bufs(score_refs, t)
        chains.append(_Chain(
            q=q_ref[:, cols],
            k_blk=functools.partial(lambda k0, cols: k_ref[pl.ds(k0, tk), cols], cols=cols),
            vt_blk=lambda k0: vt_ref[:, pl.ds(k0, tk)],
            m=m_sc.at[t], acc=acc_sc.at[t], buf_a=buf_a, buf_b=buf_b))
    _flash_pairs(pl.program_id(1), tq, tk, chains)

    lam = (jnp.exp(jnp.sum(lq1_ref[...] * lk1_ref[...], axis=-1, keepdims=True))
           - jnp.exp(jnp.sum(lq2_ref[...] * lk2_ref[...], axis=-1, keepdims=True))
           + LAMBDA_INIT)
    a1, a2 = acc_sc[0], acc_sc[1]
    o = (a1[:DIFF_V] / a1[DIFF_V:DIFF_V + 1]
         - lam * (a2[:DIFF_V] / a2[DIFF_V:DIFF_V + 1])).T
    ms = jnp.mean(o * o, axis=-1, keepdims=True)
    o = o * lax.rsqrt(ms + EPS) * g_ref[...]
    o_ref[...] = (o * (1.0 - LAMBDA_INIT)).astype(o_ref.dtype)


def _diff_attn(qk, vt, lq1, lk1, lq2, lk2, g, tq, tk):
    s = qk.shape[0]
    kern = functools.partial(_diff_attn_kernel, tq=tq, tk=tk)
    k_off = DIFF_QK_COLS // DIFF_V
    vec = pl.BlockSpec((1, DIFF_HEAD_DIM), lambda h, i: (0, 0))
    return pl.pallas_call(
        kern,
        grid=(DIFF_HEADS, s // tq),
        in_specs=[pl.BlockSpec((tq, DIFF_V), lambda h, i: (i, h)),
                  pl.BlockSpec((s, DIFF_V), lambda h, i: (0, h + k_off)),
                  pl.BlockSpec((DIFF_V, s), lambda h, i: (h, 0)),
                  vec, vec, vec, vec,
                  pl.BlockSpec((1, DIFF_V), lambda h, i: (0, 0))],
        out_specs=pl.BlockSpec((tq, DIFF_V), lambda h, i: (i, h)),
        out_shape=jax.ShapeDtypeStruct((s, DIFF_HEADS * DIFF_V), BF16),
        scratch_shapes=[pltpu.VMEM((2, 1, tq), F32),
                        pltpu.VMEM((2, DIFF_V + ONES_ROWS, tq), F32)]
        + _score_scratch(2, tq, tk),
        compiler_params=_cparams(("parallel", "arbitrary")),
        name="diff_attn",
    )(qk, qk, vt, lq1, lk1, lq2, lk2, g)


def _wo_kernel(oa_ref, ob_ref, wa_ref, wb_ref, x_ref, h_ref):
    acc = jnp.dot(oa_ref[...], wa_ref[...], preferred_element_type=F32)
    acc = acc + jnp.dot(ob_ref[...], wb_ref[...], preferred_element_type=F32)
    h_ref[...] = x_ref[...] + acc


def _wo_proj(oa, ob, w, x, tm, tn):
    m, ka = oa.shape
    kb = ob.shape[1]
    assert ka == kb and w.shape[0] == ka + kb
    n = w.shape[1]
    return pl.pallas_call(
        _wo_kernel,
        grid=(m // tm, n // tn),
        in_specs=[pl.BlockSpec((tm, ka), lambda i, j: (i, 0)),
                  pl.BlockSpec((tm, kb), lambda i, j: (i, 0)),
                  pl.BlockSpec((ka, tn), lambda i, j: (0, j)),
                  pl.BlockSpec((kb, tn), lambda i, j: (1, j)),
                  pl.BlockSpec((tm, tn), lambda i, j: (i, j))],
        out_specs=pl.BlockSpec((tm, tn), lambda i, j: (i, j)),
        out_shape=jax.ShapeDtypeStruct((m, n), F32),
        compiler_params=_cparams(("parallel", "arbitrary")),
        name="wo_proj",
    )(oa, ob, w, w, x)


def _gate_up_kernel(a_ref, wg_ref, wu_ref, wd_ref, o_ref, wd_o_ref):
    a = a_ref[...]
    g = jnp.dot(a, wg_ref[...].astype(BF16), preferred_element_type=F32)
    u = jnp.dot(a, wu_ref[...].astype(BF16), preferred_element_type=F32)
    hg = 0.5 * g
    o_ref[...] = (hg * (1.0 + jnp.tanh(hg)) * u).astype(o_ref.dtype)
    wd_o_ref[...] = wd_ref[...].astype(wd_o_ref.dtype)


def _gate_up(a, wg, wu, wd, tm, tn):
    m, k = a.shape
    n = wg.shape[1]
    ni, nj = m // tm, n // tn
    kd, nd = wd.shape
    assert kd % (ni * nj) == 0
    slab = kd // (ni * nj)
    return pl.pallas_call(
        _gate_up_kernel,
        grid=(ni, nj),
        in_specs=[pl.BlockSpec((tm, k), lambda i, j: (i, 0), pipeline_mode=pl.Buffered(1)),
                  pl.BlockSpec((k, tn), lambda i, j: (0, j)),
                  pl.BlockSpec((k, tn), lambda i, j: (0, j)),
                  pl.BlockSpec((slab, nd), lambda i, j: (i * nj + j, 0))],
        out_specs=[pl.BlockSpec((tm, tn), lambda i, j: (i, j)),
                   pl.BlockSpec((slab, nd), lambda i, j: (i * nj + j, 0))],
        out_shape=[jax.ShapeDtypeStruct((m, n), BF16), jax.ShapeDtypeStruct((kd, nd), BF16)],
        compiler_params=_cparams(("arbitrary", "arbitrary")),
        name="ffn_gate_up",
    )(a, wg, wu, wd)


def _down_kernel(a_ref, w_ref, h_ref, o_ref):
    o_ref[...] = h_ref[...] + jnp.dot(a_ref[...], w_ref[...], preferred_element_type=F32)


def _down_proj(a, w, h, tm, tn):
    m, k = a.shape
    n = w.shape[1]
    return pl.pallas_call(
        _down_kernel,
        grid=(m // tm, n // tn),
        in_specs=[pl.BlockSpec((tm, k), lambda i, j: (i, 0)),
                  pl.BlockSpec((k, tn), lambda i, j: (0, j)),
                  pl.BlockSpec((tm, tn), lambda i, j: (i, j))],
        out_specs=pl.BlockSpec((tm, tn), lambda i, j: (i, j)),
        out_shape=jax.ShapeDtypeStruct((m, n), F32),
        compiler_params=_cparams(("parallel", "arbitrary")),
        name="ffn_down",
    )(a, w, h)


def _rope_tables(s):
    pos = np.arange(s, dtype=np.float64)

    def cs(half):
        inv_freq = ROPE_THETA ** (-np.arange(half, dtype=np.float64) / half)
        ang = pos[:, None] * inv_freq[None, :]
        return np.cos(ang), np.sin(ang)

    c, sn = cs(ROPE_HALF)
    gap = np.zeros_like(c)
    mla_cos = np.concatenate([c, gap, c, gap], axis=-1)
    mla_sin = np.concatenate([-sn, gap, sn, gap], axis=-1)
    c, sn = cs(DIFF_HEAD_DIM // 2)
    diff_cos = np.concatenate([c, c], axis=-1)
    diff_sin = np.concatenate([-sn, sn], axis=-1)
    return tuple(jnp.asarray(t.astype(np.float32))
                 for t in (mla_cos, mla_sin, diff_cos, diff_sin))


def _pad_gain(g, lo, hi, width):
    return jnp.pad(g[lo:hi], (0, width - (hi - lo))).reshape(1, width)


def _spread_rope(t, axis):
    t1, t2 = jnp.split(t, 2, axis=axis)
    gap = jnp.zeros_like(t1)
    return jnp.concatenate([t1, gap, t2, gap], axis=axis)


def kernel(x, attn_norm_g, w_in, q_latent_norm_g, kv_latent_norm_g, w_uq, w_ukv, mla_q_norm_g, mla_k_norm_g, diff_q_norm_g, diff_k_norm_g, lambda_q1, lambda_k1, lambda_q2, lambda_k2, diff_subln_g, w_o, ffn_norm_g, w_gate, w_up, w_down):
    b, s, d = x.shape
    assert b == 1 and d == D_MODEL and s % 512 == 0
    x2 = x.reshape(s, d)
    tm = min(1024, s)
    tp = min(512, s)
    tq = min(1024, s)
    tk = tq // 2

    w_in_t = jnp.swapaxes(w_in[0], 0, 1)
    n_diff = 2 * DIFF_QK_COLS + DIFF_V_COLS
    w_lat_t = _cast_rows_bf16(w_in_t, 0, Z_LAT, Z_LAT // 2)
    w_diff_t = _cast_rows_bf16(w_in_t, Z_LAT_USED, n_diff, math.gcd(Z_LAT_USED, n_diff))
    w_uq3 = w_uq[0].reshape(MLA_Q_LORA, MLA_HEADS, MLA_QK)
    w_uq_p = jnp.concatenate([w_uq3[:, :, :MLA_NOPE], _spread_rope(w_uq3[:, :, MLA_NOPE:], 2)],
                             axis=2).reshape(MLA_Q_LORA, MLA_HEADS * MLA_HEAD_PAD).astype(BF16)
    w_ukv3 = w_ukv[0].reshape(MLA_KV_LORA, MLA_HEADS, MLA_NOPE + MLA_V)
    w_uk = w_ukv3[:, :, :MLA_NOPE].reshape(MLA_KV_LORA, MLA_HEADS * MLA_NOPE).astype(BF16)
    w_uv_t = w_ukv3[:, :, MLA_NOPE:].reshape(MLA_KV_LORA, MLA_HEADS * MLA_V).T.astype(BF16)

    mla_cos, mla_sin, diff_cos, diff_sin = _rope_tables(s)
    gq_a = _pad_gain(mla_q_norm_g[0], 0, MLA_NOPE, LANES)
    gq_b = _spread_rope(mla_q_norm_g[0][MLA_NOPE:], 0).reshape(1, LANES)
    gk_a = _pad_gain(mla_k_norm_g[0], 0, MLA_NOPE, LANES)
    gk_b = _spread_rope(mla_k_norm_g[0][MLA_NOPE:], 0).reshape(1, LANES)
    g_diff = jnp.concatenate([jnp.tile(diff_q_norm_g[0] * DIFF_QSCALE, 2 * DIFF_HEADS),
                              jnp.tile(diff_k_norm_g[0], 2 * DIFF_HEADS)]).reshape(1, -1)

    n = _rmsnorm(x2, attn_norm_g[0], min(512, s))
    z_lat = _matmul_nt(n, w_lat_t, tm, Z_LAT // 2, BF16, "in_proj_lat")
    z_qk, w_o_b = _matmul_nt(n, w_diff_t, tm, 1024, BF16, "in_proj_diff",
                             ride=w_o[0], n=2 * DIFF_QK_COLS)
    dv_t = _proj_t(w_diff_t, 2 * DIFF_QK_COLS // DIFF_V_COLS, DIFF_V_COLS, n, tp,
                   "in_proj_dv_t")
    q_a = _q_prep(z_lat, q_latent_norm_g[0].reshape(1, -1), w_uq_p, gq_a, gq_b,
                  mla_cos, mla_sin, tm)
    k_a, vt_a = _kv_prep(z_lat, kv_latent_norm_g[0].reshape(1, -1), w_uk, w_uv_t, gk_a, gk_b,
                        mla_cos, mla_sin, tp)
    o_a = _mla_attn(q_a, k_a, vt_a, tq, tk)
    qk_b = _diff_prep(z_qk, g_diff, diff_cos, diff_sin, tm)
    o_b = _diff_attn(qk_b, dv_t, lambda_q1[0].reshape(1, -1), lambda_k1[0].reshape(1, -1),
                     lambda_q2[0].reshape(1, -1), lambda_k2[0].reshape(1, -1),
                     diff_subln_g[0].reshape(1, -1), tq, tk)
    h = _wo_proj(o_a, o_b, w_o_b, x2, tm, 1024)

    m = _rmsnorm(h, ffn_norm_g[0], min(512, s))
    a, w_down_b = _gate_up(m, w_gate[0], w_up[0], w_down[0], min(2048, s), 256)
    out = _down_proj(a, w_down_b, h, min(512, s), 512)
    return out.reshape(b, s, d)
```

```python
import functools
import math
from typing import Any, NamedTuple

import jax
import jax.numpy as jnp
import numpy as np
from jax import lax
from jax.experimental import pallas as pl
from jax.experimental.pallas import tpu as pltpu

F32 = jnp.float32
BF16 = jnp.bfloat16

D_MODEL = 4096
CHUNK = 64
ROPE_THETA = 10000.0
EPS = 1e-6
MLA_HEADS = 16
MLA_Q_LORA = 768
MLA_KV_LORA = 512
MLA_NOPE = 128
MLA_ROPE = 64
MLA_QK = MLA_NOPE + MLA_ROPE
MLA_V = 128
LOG2E = math.log2(math.e)
MLA_SCALE = 1.0 / math.sqrt(MLA_QK)
MLA_QSCALE = MLA_SCALE * LOG2E
DIFF_HEADS = 8
DIFF_HEAD_DIM = 128
DIFF_V = 2 * DIFF_HEAD_DIM
DIFF_SCALE = 1.0 / math.sqrt(DIFF_HEAD_DIM)
DIFF_QSCALE = DIFF_SCALE * LOG2E
DIFF_QK_COLS = 2 * DIFF_HEADS * DIFF_HEAD_DIM
DIFF_V_COLS = DIFF_HEADS * DIFF_V
D_FF = -(-8 * D_MODEL // (3 * 256)) * 256
LAMBDA_INIT = 0.8 - 0.6 * math.exp(-0.3 * 0)

LANES = 128
MLA_HEAD_PAD = 2 * LANES
Z_LAT = 1536
Z_KPE = MLA_Q_LORA + MLA_KV_LORA
Z_LAT_USED = Z_KPE + MLA_ROPE
NEG = -1e30
VMEM_LIMIT = 56 * 1024 * 1024


def _cparams(sem):
    return pltpu.CompilerParams(dimension_semantics=sem, vmem_limit_bytes=VMEM_LIMIT)


def _cast_kernel(w_ref, o_ref):
    o_ref[...] = w_ref[...].astype(o_ref.dtype)


def _cast_rows_bf16(w, row0, n_rows, tr):
    c = w.shape[1]
    assert row0 % tr == 0 and n_rows % tr == 0 and row0 + n_rows <= w.shape[0]
    blk0 = row0 // tr
    return pl.pallas_call(
        _cast_kernel,
        grid=(n_rows // tr,),
        in_specs=[pl.BlockSpec((tr, c), lambda i: (i + blk0, 0))],
        out_specs=pl.BlockSpec((tr, c), lambda i: (i, 0)),
        out_shape=jax.ShapeDtypeStruct((n_rows, c), BF16),
        compiler_params=_cparams(("parallel",)),
        name="cast_bf16",
    )(w)


def _rmsnorm_kernel(x_ref, g_ref, o_ref):
    x = x_ref[...]
    ms = jnp.mean(x * x, axis=-1, keepdims=True)
    o_ref[...] = (x * lax.rsqrt(ms + EPS) * g_ref[...]).astype(o_ref.dtype)


def _rmsnorm(x, g, tm):
    m, d = x.shape
    return pl.pallas_call(
        _rmsnorm_kernel,
        grid=(m // tm,),
        in_specs=[pl.BlockSpec((tm, d), lambda i: (i, 0)),
                  pl.BlockSpec((1, d), lambda i: (0, 0))],
        out_specs=pl.BlockSpec((tm, d), lambda i: (i, 0)),
        out_shape=jax.ShapeDtypeStruct((m, d), BF16),
        compiler_params=_cparams(("parallel",)),
        name="rmsnorm",
    )(x, g.reshape(1, d))


def _mm_nt_kernel(a_ref, bt_ref, o_ref):
    o_ref[...] = lax.dot_general(a_ref[...], bt_ref[...], (((1,), (1,)), ((), ())),
                                 preferred_element_type=F32).astype(o_ref.dtype)


def _mm_nt_ride_kernel(a_ref, bt_ref, w_ref, o_ref, w_o_ref):
    _mm_nt_kernel(a_ref, bt_ref, o_ref)
    w_o_ref[...] = w_ref[...].astype(w_o_ref.dtype)


def _matmul_nt(a, bt, tm, tn, out_dtype, name, ride=None, n=None):
    m, k = a.shape
    n = bt.shape[0] if n is None else n
    assert bt.shape[1] == k and n % tn == 0 and n <= bt.shape[0]
    ni, nj = m // tm, n // tn
    in_specs = [pl.BlockSpec((tm, k), lambda i, j: (i, 0)),
                pl.BlockSpec((tn, k), lambda i, j: (j, 0))]
    out_spec = pl.BlockSpec((tm, tn), lambda i, j: (i, j))
    out_shape = jax.ShapeDtypeStruct((m, n), out_dtype)
    if ride is None:
        return pl.pallas_call(
            _mm_nt_kernel, grid=(ni, nj), in_specs=in_specs, out_specs=out_spec,
            out_shape=out_shape, compiler_params=_cparams(("parallel", "arbitrary")),
            name=name)(a, bt)
    rr, rc = ride.shape
    assert rr % (ni * nj) == 0
    slab = pl.BlockSpec((rr // (ni * nj), rc), lambda i, j: (i * nj + j, 0))
    return pl.pallas_call(
        _mm_nt_ride_kernel, grid=(ni, nj), in_specs=in_specs + [slab],
        out_specs=[out_spec, slab],
        out_shape=[out_shape, jax.ShapeDtypeStruct((rr, rc), BF16)],
        compiler_params=_cparams(("arbitrary", "arbitrary")),
        name=name)(a, bt, ride)


def _proj_t(wt, blk, rows, a, tm, name):
    m, k = a.shape
    return pl.pallas_call(
        _mm_nt_kernel,
        grid=(m // tm,),
        in_specs=[pl.BlockSpec((rows, k), lambda i: (blk, 0), pipeline_mode=pl.Buffered(1)),
                  pl.BlockSpec((tm, k), lambda i: (i, 0))],
        out_specs=pl.BlockSpec((rows, tm), lambda i: (0, i)),
        out_shape=jax.ShapeDtypeStruct((rows, m), BF16),
        compiler_params=_cparams(("parallel",)),
        name=name,
    )(wt, a)


def _latent_norm(c, g):
    ms = jnp.mean(c * c, axis=-1, keepdims=True)
    return (c * lax.rsqrt(ms + EPS) * g).astype(BF16)


ROPE_HALF = MLA_ROPE // 2


def _mla_rope(y, cos, sin):
    return y * cos + pltpu.roll(y, 64, 1) * sin


def _q_prep_kernel(z_ref, gl_ref, w_ref, ga_ref, gb_ref, cos_ref, sin_ref, o_ref):
    ga, gb = ga_ref[...], gb_ref[...]
    half = z_ref.shape[0] // 2
    for rows in (slice(0, half), slice(half, 2 * half)):
        c = z_ref[rows, :MLA_Q_LORA].astype(F32)
        lat = _latent_norm(c, gl_ref[...])
        q = jnp.dot(lat, w_ref[...], preferred_element_type=F32)
        cos, sin = cos_ref[rows, :], sin_ref[rows, :]
        for h in range(MLA_HEADS):
            lo = h * MLA_HEAD_PAD
            qa = q[:, lo:lo + LANES]
            qb = q[:, lo + LANES:lo + 2 * LANES]
            ss = jnp.sum(qa * qa + qb * qb, axis=-1, keepdims=True)
            r = lax.rsqrt(ss * (1.0 / MLA_QK) + EPS)
            ya = qa * r * ga
            yb = _mla_rope(qb * r * gb, cos, sin)
            o_ref[rows, lo:lo + LANES] = (ya * MLA_QSCALE).astype(o_ref.dtype)
            o_ref[rows, lo + LANES:lo + 2 * LANES] = (yb * MLA_QSCALE).astype(o_ref.dtype)


def _kv_prep_kernel(z_ref, gl_ref, wk_ref, wvt_ref, ga_ref, gb_ref, cos_ref, sin_ref,
                    k_ref, vt_ref):
    c = z_ref[:, MLA_Q_LORA:Z_KPE].astype(F32)
    lat = _latent_norm(c, gl_ref[...])
    kn = jnp.dot(lat, wk_ref[...], preferred_element_type=F32)
    vt_ref[...] = lax.dot_general(wvt_ref[...], lat, (((1,), (1,)), ((), ())),
                                  preferred_element_type=F32).astype(vt_ref.dtype)
    lane = lax.broadcasted_iota(jnp.int32, (1, LANES), 1)
    raw = z_ref[:, Z_KPE:Z_KPE + LANES].astype(F32)
    t1 = jnp.where(lane < ROPE_HALF, raw, 0.0)
    t2 = jnp.where((lane >= ROPE_HALF) & (lane < MLA_ROPE), raw, 0.0)
    kpe = t1 + pltpu.roll(t2, ROPE_HALF, 1)
    ss_pe = jnp.sum(kpe * kpe, axis=-1, keepdims=True)
    pe_rot = _mla_rope(kpe * gb_ref[...], cos_ref[...], sin_ref[...])
    ga = ga_ref[...]
    for h in range(MLA_HEADS):
        a = kn[:, h * LANES:(h + 1) * LANES]
        ss = jnp.sum(a * a, axis=-1, keepdims=True) + ss_pe
        r = lax.rsqrt(ss * (1.0 / MLA_QK) + EPS)
        lo = h * MLA_HEAD_PAD
        k_ref[:, lo:lo + LANES] = (a * r * ga).astype(k_ref.dtype)
        k_ref[:, lo + LANES:lo + 2 * LANES] = (pe_rot * r).astype(k_ref.dtype)


def _row_spec(tm, n, col=0):
    return pl.BlockSpec((tm, n), lambda i: (i, col))


def _const_spec(shape):
    return pl.BlockSpec(shape, lambda i: (0, 0))


def _q_prep(z, gl, w, ga, gb, cos, sin, tm):
    m = z.shape[0]
    n = MLA_HEADS * MLA_HEAD_PAD
    return pl.pallas_call(
        _q_prep_kernel,
        grid=(m // tm,),
        in_specs=[_row_spec(tm, Z_LAT), _const_spec(gl.shape), _const_spec(w.shape),
                  _const_spec(ga.shape), _const_spec(gb.shape),
                  _row_spec(tm, LANES), _row_spec(tm, LANES)],
        out_specs=_row_spec(tm, n),
        out_shape=jax.ShapeDtypeStruct((m, n), BF16),
        compiler_params=_cparams(("parallel",)),
        name="mla_q_prep",
    )(z, gl, w, ga, gb, cos, sin)


def _kv_prep(z, gl, wk, wvt, ga, gb, cos, sin, tm):
    m = z.shape[0]
    nk = MLA_HEADS * MLA_HEAD_PAD
    nv = MLA_HEADS * MLA_V
    return pl.pallas_call(
        _kv_prep_kernel,
        grid=(m // tm,),
        in_specs=[_row_spec(tm, Z_LAT), _const_spec(gl.shape), _const_spec(wk.shape),
                  _const_spec(wvt.shape), _const_spec(ga.shape), _const_spec(gb.shape),
                  _row_spec(tm, LANES), _row_spec(tm, LANES)],
        out_specs=[_row_spec(tm, nk), pl.BlockSpec((nv, tm), lambda i: (0, i))],
        out_shape=[jax.ShapeDtypeStruct((m, nk), BF16), jax.ShapeDtypeStruct((nv, m), BF16)],
        compiler_params=_cparams(("parallel",)),
        name="mla_kv_prep",
    )(z, gl, wk, wvt, ga, gb, cos, sin)


DIFF_PREP_COLS = 1024


def _diff_prep_kernel(x_ref, g_ref, cos_ref, sin_ref, o_ref):
    cos, sin = cos_ref[...], sin_ref[...]
    for c in range(DIFF_PREP_COLS // LANES):
        sl = slice(c * LANES, (c + 1) * LANES)
        x = x_ref[:, sl].astype(F32)
        ms = jnp.mean(x * x, axis=-1, keepdims=True)
        y = x * lax.rsqrt(ms + EPS) * g_ref[:, sl]
        o_ref[:, sl] = (y * cos + pltpu.roll(y, 64, 1) * sin).astype(o_ref.dtype)


def _diff_prep(z, g, cos, sin, tm):
    m = z.shape[0]
    n = 2 * DIFF_QK_COLS
    nb = n // DIFF_PREP_COLS
    return pl.pallas_call(
        _diff_prep_kernel,
        grid=(m // tm, nb),
        in_specs=[pl.BlockSpec((tm, DIFF_PREP_COLS), lambda i, j: (i, j)),
                  pl.BlockSpec((1, DIFF_PREP_COLS), lambda i, j: (0, j)),
                  pl.BlockSpec((tm, LANES), lambda i, j: (i, 0)),
                  pl.BlockSpec((tm, LANES), lambda i, j: (i, 0))],
        out_specs=pl.BlockSpec((tm, DIFF_PREP_COLS), lambda i, j: (i, j)),
        out_shape=jax.ShapeDtypeStruct((m, n), BF16),
        compiler_params=_cparams(("parallel", "arbitrary")),
        name="diff_qk_prep",
    )(z, g, cos, sin)


def _scores_t(k, q):
    return lax.dot_general(k, q, (((1,), (1,)), ((), ())), preferred_element_type=F32)


class _Buf(NamedTuple):
    s: Any
    bmax: Any

    def cols(self, sl):
        return _Buf(self.s.at[:, sl], self.bmax.at[:, sl])


def _scores_into(buf, k, q):
    s = _scores_t(k, q)
    buf.s[...] = s
    buf.bmax[...] = jnp.max(s, axis=0, keepdims=True)


def _mask_diag_block(buf, c0, tk):
    upper = lax.broadcasted_iota(jnp.int32, (CHUNK, LANES), 1) >= CHUNK
    for g in range(tk // LANES):
        cols = pl.ds(c0 + g * LANES, LANES)
        r0 = g * LANES + CHUNK
        edge = jnp.where(upper, buf.s[r0:r0 + CHUNK, cols], NEG)
        buf.s[r0:r0 + CHUNK, cols] = edge
        if r0 + CHUNK < tk:
            buf.s[r0 + CHUNK:tk, cols] = jnp.full((tk - r0 - CHUNK, LANES), NEG, F32)
        buf.bmax[:, cols] = jnp.maximum(jnp.max(buf.s[0:r0, cols], axis=0, keepdims=True),
                                        jnp.max(edge, axis=0, keepdims=True))


ONES_ROWS = 16


def _online_update_t(buf, vt, m_sc, acc_sc):
    tk, tq = buf.s.shape
    m_prev = m_sc[...]
    m_new = jnp.maximum(m_prev, buf.bmax[...])
    alpha = jnp.exp2(m_prev - m_new)
    p_t = jnp.exp2(buf.s[...] - m_new)
    v_aug = jnp.concatenate([vt, jnp.ones((ONES_ROWS, tk), BF16)], axis=0)
    pv = jnp.dot(v_aug, p_t.astype(BF16), preferred_element_type=F32)
    acc_sc[...] = alpha * acc_sc[...] + pv
    m_sc[...] = m_new


class _Chain(NamedTuple):
    q: Any
    k_blk: Any
    vt_blk: Any
    m: Any
    acc: Any
    buf_a: Any
    buf_b: Any


def _flash_pairs(i, tq, tk, chains):
    assert tq == 2 * tk

    def k_start(j):
        return pl.multiple_of(j * tk, tk)

    def qk(use_b, j):
        k0 = k_start(j)
        for c in chains:
            _scores_into(c.buf_b if use_b else c.buf_a, c.k_blk(k0), c.q)

    def update(use_b, j):
        k0 = k_start(j)
        for cols in (pl.ds(0, tk), pl.ds(tk, tk)):
            for c in chains:
                buf = c.buf_b if use_b else c.buf_a
                _online_update_t(buf.cols(cols), c.vt_blk(k0), c.m.at[:, cols],
                                 c.acc.at[:, cols])

    for c in chains:
        c.m[...] = jnp.full(c.m.shape, NEG, F32)
        c.acc[...] = jnp.zeros(c.acc.shape, F32)

    qk(False, 0)

    def body(p, carry):
        qk(True, 2 * p + 1)
        update(False, 2 * p)
        qk(False, 2 * p + 2)
        update(True, 2 * p + 1)
        return carry

    lax.fori_loop(0, i, body, 0)
    half = pl.ds(tk, tk)
    k_last = k_start(2 * i + 1)
    for c in chains:
        _mask_diag_block(c.buf_a, 0, tk)
        _scores_into(c.buf_b.cols(half), c.k_blk(k_last), c.q[tk:, :])
        _mask_diag_block(c.buf_b, tk, tk)
    update(False, 2 * i)
    for c in chains:
        _online_update_t(c.buf_b.cols(half), c.vt_blk(k_last), c.m.at[:, half],
                         c.acc.at[:, half])


def _score_scratch(n, tq, tk):
    one = [pltpu.VMEM((n, tk, tq), F32), pltpu.VMEM((n, 1, tq), F32)]
    return one + one


def _bufs(refs, idx):
    sa, ma, sb, mb = refs
    return _Buf(sa.at[idx], ma.at[idx]), _Buf(sb.at[idx], mb.at[idx])


MLA_HEADS_PER_STEP = 2


def _mla_attn_kernel(q_ref, k_ref, vt_ref, o_ref, m_sc, acc_sc, *score_refs, tq, tk):
    nh = m_sc.shape[0]
    chains = []
    for h in range(nh):
        qk_cols = slice(h * MLA_HEAD_PAD, (h + 1) * MLA_HEAD_PAD)
        v_rows = slice(h * MLA_V, (h + 1) * MLA_V)
        buf_a, buf_b = _bufs(score_refs, h)
        chains.append(_Chain(
            q=q_ref[:, qk_cols],
            k_blk=functools.partial(lambda k0, cols: k_ref[pl.ds(k0, tk), cols], cols=qk_cols),
            vt_blk=functools.partial(lambda k0, rows: vt_ref[rows, pl.ds(k0, tk)], rows=v_rows),
            m=m_sc.at[h], acc=acc_sc.at[h], buf_a=buf_a, buf_b=buf_b))
    _flash_pairs(pl.program_id(1), tq, tk, chains)
    for h in range(nh):
        acc = acc_sc[h]
        o = acc[:MLA_V] / acc[MLA_V:MLA_V + 1]
        o_ref[:, h * MLA_V:(h + 1) * MLA_V] = o.T.astype(o_ref.dtype)


def _mla_attn(q, k, vt, tq, tk):
    s = q.shape[0]
    nh = MLA_HEADS_PER_STEP
    kern = functools.partial(_mla_attn_kernel, tq=tq, tk=tk)
    return pl.pallas_call(
        kern,
        grid=(MLA_HEADS // nh, s // tq),
        in_specs=[pl.BlockSpec((tq, nh * MLA_HEAD_PAD), lambda h, i: (i, h)),
                  pl.BlockSpec((s, nh * MLA_HEAD_PAD), lambda h, i: (0, h)),
                  pl.BlockSpec((nh * MLA_V, s), lambda h, i: (h, 0))],
        out_specs=pl.BlockSpec((tq, nh * MLA_V), lambda h, i: (i, h)),
        out_shape=jax.ShapeDtypeStruct((s, MLA_HEADS * MLA_V), BF16),
        scratch_shapes=[pltpu.VMEM((nh, 1, tq), F32),
                        pltpu.VMEM((nh, MLA_V + ONES_ROWS, tq), F32)]
        + _score_scratch(nh, tq, tk),
        compiler_params=_cparams(("parallel", "arbitrary")),
        name="mla_attn",
    )(q, k, vt)


def _diff_attn_kernel(q_ref, k_ref, vt_ref, lq1_ref, lk1_ref, lq2_ref, lk2_ref, g_ref, o_ref,
                      m_sc, acc_sc, *score_refs, tq, tk):
    chains = []
    for t in range(2):
        cols = slice(t * DIFF_HEAD_DIM, (t + 1) * DIFF_HEAD_DIM)
        buf_a, buf_b = _bufs(score_refs, t)
        chains.append(_Chain(
            q=q_ref[:, cols],
            k_blk=functools.partial(lambda k0, cols: k_ref[pl.ds(k0, tk), cols], cols=cols),
            vt_blk=lambda k0: vt_ref[:, pl.ds(k0, tk)],
            m=m_sc.at[t], acc=acc_sc.at[t], buf_a=buf_a, buf_b=buf_b))
    _flash_pairs(pl.program_id(1), tq, tk, chains)

    lam = (jnp.exp(jnp.sum(lq1_ref[...] * lk1_ref[...], axis=-1, keepdims=True))
           - jnp.exp(jnp.sum(lq2_ref[...] * lk2_ref[...], axis=-1, keepdims=True))
           + LAMBDA_INIT)
    a1, a2 = acc_sc[0], acc_sc[1]
    o = (a1[:DIFF_V] / a1[DIFF_V:DIFF_V + 1]
         - lam * (a2[:DIFF_V] / a2[DIFF_V:DIFF_V + 1])).T
    ms = jnp.mean(o * o, axis=-1, keepdims=True)
    o = o * lax.rsqrt(ms + EPS) * g_ref[...]
    o_ref[...] = (o * (1.0 - LAMBDA_INIT)).astype(o_ref.dtype)


def _diff_attn(qk, vt, lq1, lk1, lq2, lk2, g, tq, tk):
    s = qk.shape[0]
    kern = functools.partial(_diff_attn_kernel, tq=tq, tk=tk)
    k_off = DIFF_QK_COLS // DIFF_V
    vec = pl.BlockSpec((1, DIFF_HEAD_DIM), lambda h, i: (0, 0))
    return pl.pallas_call(
        kern,
        grid=(DIFF_HEADS, s // tq),
        in_specs=[pl.BlockSpec((tq, DIFF_V), lambda h, i: (i, h)),
                  pl.BlockSpec((s, DIFF_V), lambda h, i: (0, h + k_off)),
                  pl.BlockSpec((DIFF_V, s), lambda h, i: (h, 0)),
                  vec, vec, vec, vec,
                  pl.BlockSpec((1, DIFF_V), lambda h, i: (0, 0))],
        out_specs=pl.BlockSpec((tq, DIFF_V), lambda h, i: (i, h)),
        out_shape=jax.ShapeDtypeStruct((s, DIFF_HEADS * DIFF_V), BF16),
        scratch_shapes=[pltpu.VMEM((2, 1, tq), F32),
                        pltpu.VMEM((2, DIFF_V + ONES_ROWS, tq), F32)]
        + _score_scratch(2, tq, tk),
        compiler_params=_cparams(("parallel", "arbitrary")),
        name="diff_attn",
    )(qk, qk, vt, lq1, lk1, lq2, lk2, g)


def _wo_kernel(oa_ref, ob_ref, wa_ref, wb_ref, x_ref, h_ref):
    acc = jnp.dot(oa_ref[...], wa_ref[...], preferred_element_type=F32)
    acc = acc + jnp.dot(ob_ref[...], wb_ref[...], preferred_element_type=F32)
    h_ref[...] = x_ref[...] + acc


def _wo_proj(oa, ob, w, x, tm, tn):
    m, ka = oa.shape
    kb = ob.shape[1]
    assert ka == kb and w.shape[0] == ka + kb
    n = w.shape[1]
    return pl.pallas_call(
        _wo_kernel,
        grid=(m // tm, n // tn),
        in_specs=[pl.BlockSpec((tm, ka), lambda i, j: (i, 0)),
                  pl.BlockSpec((tm, kb), lambda i, j: (i, 0)),
                  pl.BlockSpec((ka, tn), lambda i, j: (0, j)),
                  pl.BlockSpec((kb, tn), lambda i, j: (1, j)),
                  pl.BlockSpec((tm, tn), lambda i, j: (i, j))],
        out_specs=pl.BlockSpec((tm, tn), lambda i, j: (i, j)),
        out_shape=jax.ShapeDtypeStruct((m, n), F32),
        compiler_params=_cparams(("parallel", "arbitrary")),
        name="wo_proj",
    )(oa, ob, w, w, x)


def _gate_up_kernel(a_ref, wg_ref, wu_ref, wd_ref, o_ref, wd_o_ref):
    a = a_ref[...]
    g = jnp.dot(a, wg_ref[...].astype(BF16), preferred_element_type=F32)
    u = jnp.dot(a, wu_ref[...].astype(BF16), preferred_element_type=F32)
    hg = 0.5 * g
    o_ref[...] = (hg * (1.0 + jnp.tanh(hg)) * u).astype(o_ref.dtype)
    wd_o_ref[...] = wd_ref[...].astype(wd_o_ref.dtype)


def _gate_up(a, wg, wu, wd, tm, tn):
    m, k = a.shape
    n = wg.shape[1]
    ni, nj = m // tm, n // tn
    kd, nd = wd.shape
    assert kd % (ni * nj) == 0
    slab = kd // (ni * nj)
    return pl.pallas_call(
        _gate_up_kernel,
        grid=(ni, nj),
        in_specs=[pl.BlockSpec((tm, k), lambda i, j: (i, 0), pipeline_mode=pl.Buffered(1)),
                  pl.BlockSpec((k, tn), lambda i, j: (0, j)),
                  pl.BlockSpec((k, tn), lambda i, j: (0, j)),
                  pl.BlockSpec((slab, nd), lambda i, j: (i * nj + j, 0))],
        out_specs=[pl.BlockSpec((tm, tn), lambda i, j: (i, j)),
                   pl.BlockSpec((slab, nd), lambda i, j: (i * nj + j, 0))],
        out_shape=[jax.ShapeDtypeStruct((m, n), BF16), jax.ShapeDtypeStruct((kd, nd), BF16)],
        compiler_params=_cparams(("arbitrary", "arbitrary")),
        name="ffn_gate_up",
    )(a, wg, wu, wd)


def _down_kernel(a_ref, w_ref, h_ref, o_ref):
    o_ref[...] = h_ref[...] + jnp.dot(a_ref[...], w_ref[...], preferred_element_type=F32)


def _down_proj(a, w, h, tm, tn):
    m, k = a.shape
    n = w.shape[1]
    return pl.pallas_call(
        _down_kernel,
        grid=(m // tm, n // tn),
        in_specs=[pl.BlockSpec((tm, k), lambda i, j: (i, 0)),
                  pl.BlockSpec((k, tn), lambda i, j: (0, j)),
                  pl.BlockSpec((tm, tn), lambda i, j: (i, j))],
        out_specs=pl.BlockSpec((tm, tn), lambda i, j: (i, j)),
        out_shape=jax.ShapeDtypeStruct((m, n), F32),
        compiler_params=_cparams(("parallel", "arbitrary")),
        name="ffn_down",
    )(a, w, h)


def _rope_tables(s):
    pos = np.arange(s, dtype=np.float64)

    def cs(half):
        inv_freq = ROPE_THETA ** (-np.arange(half, dtype=np.float64) / half)
        ang = pos[:, None] * inv_freq[None, :]
        return np.cos(ang), np.sin(ang)

    c, sn = cs(ROPE_HALF)
    gap = np.zeros_like(c)
    mla_cos = np.concatenate([c, gap, c, gap], axis=-1)
    mla_sin = np.concatenate([-sn, gap, sn, gap], axis=-1)
    c, sn = cs(DIFF_HEAD_DIM // 2)
    diff_cos = np.concatenate([c, c], axis=-1)
    diff_sin = np.concatenate([-sn, sn], axis=-1)
    return tuple(jnp.asarray(t.astype(np.float32))
                 for t in (mla_cos, mla_sin, diff_cos, diff_sin))


def _pad_gain(g, lo, hi, width):
    return jnp.pad(g[lo:hi], (0, width - (hi - lo))).reshape(1, width)


def _spread_rope(t, axis):
    t1, t2 = jnp.split(t, 2, axis=axis)
    gap = jnp.zeros_like(t1)
    return jnp.concatenate([t1, gap, t2, gap], axis=axis)


def kernel(x, attn_norm_g, w_in, q_latent_norm_g, kv_latent_norm_g, w_uq, w_ukv, mla_q_norm_g, mla_k_norm_g, diff_q_norm_g, diff_k_norm_g, lambda_q1, lambda_k1, lambda_q2, lambda_k2, diff_subln_g, w_o, ffn_norm_g, w_gate, w_up, w_down):
    b, s, d = x.shape
    assert b == 1 and d == D_MODEL and s % 512 == 0
    x2 = x.reshape(s, d)
    tm = min(1024, s)
    tp = min(512, s)
    tq = min(1024, s)
    tk = tq // 2

    w_in_t = jnp.swapaxes(w_in[0], 0, 1)
    n_diff = 2 * DIFF_QK_COLS + DIFF_V_COLS
    w_lat_t = _cast_rows_bf16(w_in_t, 0, Z_LAT, Z_LAT // 2)
    w_diff_t = _cast_rows_bf16(w_in_t, Z_LAT_USED, n_diff, math.gcd(Z_LAT_USED, n_diff))
    w_uq3 = w_uq[0].reshape(MLA_Q_LORA, MLA_HEADS, MLA_QK)
    w_uq_p = jnp.concatenate([w_uq3[:, :, :MLA_NOPE], _spread_rope(w_uq3[:, :, MLA_NOPE:], 2)],
                             axis=2).reshape(MLA_Q_LORA, MLA_HEADS * MLA_HEAD_PAD).astype(BF16)
    w_ukv3 = w_ukv[0].reshape(MLA_KV_LORA, MLA_HEADS, MLA_NOPE + MLA_V)
    w_uk = w_ukv3[:, :, :MLA_NOPE].reshape(MLA_KV_LORA, MLA_HEADS * MLA_NOPE).astype(BF16)
    w_uv_t = w_ukv3[:, :, MLA_NOPE:].reshape(MLA_KV_LORA, MLA_HEADS * MLA_V).T.astype(BF16)

    mla_cos, mla_sin, diff_cos, diff_sin = _rope_tables(s)
    gq_a = _pad_gain(mla_q_norm_g[0], 0, MLA_NOPE, LANES)
    gq_b = _spread_rope(mla_q_norm_g[0][MLA_NOPE:], 0).reshape(1, LANES)
    gk_a = _pad_gain(mla_k_norm_g[0], 0, MLA_NOPE, LANES)
    gk_b = _spread_rope(mla_k_norm_g[0][MLA_NOPE:], 0).reshape(1, LANES)
    g_diff = jnp.concatenate([jnp.tile(diff_q_norm_g[0] * DIFF_QSCALE, 2 * DIFF_HEADS),
                              jnp.tile(diff_k_norm_g[0], 2 * DIFF_HEADS)]).reshape(1, -1)

    n = _rmsnorm(x2, attn_norm_g[0], min(512, s))
    z_lat = _matmul_nt(n, w_lat_t, tm, Z_LAT // 2, BF16, "in_proj_lat")
    z_qk, w_o_b = _matmul_nt(n, w_diff_t, tm, 1024, BF16, "in_proj_diff",
                             ride=w_o[0], n=2 * DIFF_QK_COLS)
    dv_t = _proj_t(w_diff_t, 2 * DIFF_QK_COLS // DIFF_V_COLS, DIFF_V_COLS, n, tp,
                   "in_proj_dv_t")
    q_a = _q_prep(z_lat, q_latent_norm_g[0].reshape(1, -1), w_uq_p, gq_a, gq_b,
                  mla_cos, mla_sin, tm)
    k_a, vt_a = _kv_prep(z_lat, kv_latent_norm_g[0].reshape(1, -1), w_uk, w_uv_t, gk_a, gk_b,
                        mla_cos, mla_sin, tp)
    o_a = _mla_attn(q_a, k_a, vt_a, tq, tk)
    qk_b = _diff_prep(z_qk, g_diff, diff_cos, diff_sin, tm)
    o_b = _diff_attn(qk_b, dv_t, lambda_q1[0].reshape(1, -1), lambda_k1[0].reshape(1, -1),
                     lambda_q2[0].reshape(1, -1), lambda_k2[0].reshape(1, -1),
                     diff_subln_g[0].reshape(1, -1), tq, tk)
    h = _wo_proj(o_a, o_b, w_o_b, x2, tm, 1024)

    m = _rmsnorm(h, ffn_norm_g[0], min(512, s))
    a, w_down_b = _gate_up(m, w_gate[0], w_up[0], w_down[0], min(2048, s), 256)
    out = _down_proj(a, w_down_b, h, min(512, s), 512)
    return out.reshape(b, s, d)
```
